```python
import math
import jax
import jax.numpy as jnp
from jax import lax
import numpy as np

D_MODEL = 4096
BATCH = 4
SEQ = 4096
DEPTH = 4

N_EVEN = (DEPTH + 1) // 2
N_ODD = DEPTH // 2
HEAD_DIM = 128
MIX_DIM = D_MODEL

NSA_HEADS = 16
NSA_KV_HEADS = 4
NSA_REP = NSA_HEADS // NSA_KV_HEADS
NSA_Q_DIM = NSA_HEADS * HEAD_DIM
NSA_KV_DIM = NSA_KV_HEADS * HEAD_DIM
CMP_BLOCK = 32
CMP_STRIDE = 16
SLC_BLOCK = 64
N_SELECT = 16
WINDOW = 512
SEL_QBLOCK = 32
WIN_QBLOCK = 128
ROPE_THETA = 10000.0

SSM_DIM = MIX_DIM - NSA_Q_DIM
SSM_HEADDIM = 64
SSM_HEADS = SSM_DIM // SSM_HEADDIM
SSM_GROUPS = 8
SSM_STATE = 128
SSM_CONV = 4
SSM_CHUNK = 128
SSM_CONV_DIM = SSM_DIM + 2 * SSM_GROUPS * SSM_STATE

FOX_HEADS = 16
FOX_DIM = FOX_HEADS * HEAD_DIM
FOX_QBLOCK = 128

CONF_DIM = MIX_DIM - FOX_DIM
CONF_KERNEL = 31

D_FF = 11008
N_EXPERTS = 8
TOP_K = 2
D_EXPERT = 1408

DEEPNORM_ALPHA = (2.0 * DEPTH) ** 0.25
DEEPNORM_BETA = (8.0 * DEPTH) ** -0.25
LN_EPS = 1e-5

E_SPLIT_SIZES = (NSA_Q_DIM,) + (NSA_KV_DIM,) * 6 + (3 * NSA_HEADS, SSM_DIM, SSM_CONV_DIM, SSM_HEADS)
O_SPLIT_SIZES = (FOX_DIM, FOX_DIM, FOX_DIM, FOX_HEADS, 2 * CONF_DIM)
E_IN_DIM = sum(E_SPLIT_SIZES)
O_IN_DIM = sum(O_SPLIT_SIZES)

kernel_name = 'hybrid_nsa_ssd_fox_conformer_moe'


def _split(a, sizes):
    return jnp.split(a, np.cumsum(sizes)[:-1].tolist(), axis=-1)


def layer_norm(x, g, b):
    xf = x.astype(jnp.float32)
    mu = jnp.mean(xf, axis=-1, keepdims=True)
    var = jnp.mean(jnp.square(xf - mu), axis=-1, keepdims=True)
    return ((xf - mu) * lax.rsqrt(var + LN_EPS)).astype(x.dtype) * g + b


def masked_softmax(s, mask):
    s = jnp.where(mask, s.astype(jnp.float32), -jnp.inf)
    m = jnp.max(s, axis=-1, keepdims=True)
    m = jnp.where(jnp.isfinite(m), m, 0.0)
    e = jnp.exp(s - m)
    d = jnp.sum(e, axis=-1, keepdims=True)
    return e / jnp.where(d > 0, d, 1.0)


def rope(x, pos):
    half = x.shape[-1] // 2
    inv = ROPE_THETA ** (-jnp.arange(half, dtype=jnp.float32) / half)
    ang = pos.astype(jnp.float32)[..., None] * inv
    cos = jnp.cos(ang)[..., None, :]
    sin = jnp.sin(ang)[..., None, :]
    xf = x.astype(jnp.float32)
    x1, x2 = xf[..., :half], xf[..., half:]
    return jnp.concatenate([x1 * cos - x2 * sin, x2 * cos + x1 * sin], axis=-1).astype(x.dtype)


def causal_dwconv(x, w, b):
    k, ch = w.shape
    y = lax.conv_general_dilated(x, w[:, None, :].astype(x.dtype), window_strides=(1,), padding=[(k - 1, 0)],
                                 dimension_numbers=('NWC', 'WIO', 'NWC'), feature_group_count=ch)
    return y + b


def swiglu(h, w_gate, w_up, w_down):
    return (jax.nn.silu(h @ w_gate) * (h @ w_up)) @ w_down


def nsa_compress(k_raw, blk, pos_emb, w1, w2):
    B, _, G, _ = k_raw.shape
    n_cmp = blk.shape[0]
    blocks = k_raw[:, blk] + pos_emb[None, None, :, None, :]
    blocks = blocks.transpose(0, 1, 3, 2, 4).reshape(B, n_cmp, G, CMP_BLOCK * HEAD_DIM)
    return jax.nn.gelu(blocks @ w1) @ w2


def nsa_compressed_attention(q, k_cmp, v_cmp, cmp_end):
    S = q.shape[1]
    s = jnp.einsum('bsgrd,bngd->bgrsn', q, k_cmp) * (HEAD_DIM ** -0.5)
    mask = cmp_end[None, :] <= np.arange(S)[:, None]
    p = masked_softmax(s, mask)
    o = jnp.einsum('bgrsn,bngd->bsgrd', p.astype(v_cmp.dtype), v_cmp)
    return o, p


def nsa_select_blocks(p_cmp, cmp_start):
    S = p_cmp.shape[3]
    n_slc = S // SLC_BLOCK
    n_sel = min(N_SELECT, n_slc)
    slc_start = np.arange(n_slc) * SLC_BLOCK
    overlap = ((cmp_start[:, None] < slc_start[None, :] + SLC_BLOCK)
               & (cmp_start[:, None] + CMP_BLOCK > slc_start[None, :])).astype(np.float32)
    imp = jnp.einsum('bgrsn,nj->bgsj', p_cmp, jnp.asarray(overlap))
    qblk = np.arange(S)[:, None] // SLC_BLOCK
    j = np.arange(n_slc)[None, :]
    forced = (j == 0) | (j == qblk) | (j == qblk - 1)
    future = j > qblk
    imp = jnp.where(forced, jnp.inf, jnp.where(future, -jnp.inf, imp))
    return lax.top_k(imp, n_sel)[1]


def nsa_selected_attention(q, k, v, sel_idx):
    B, S, G, R, D = q.shape
    n_sel = sel_idx.shape[-1]
    nqb = S // SEL_QBLOCK
    kt = k.transpose(0, 2, 1, 3)
    vt = v.transpose(0, 2, 1, 3)
    bi = jnp.arange(B)[:, None, None]
    gi = jnp.arange(G)[None, :, None]
    q_blocks = q.reshape(B, nqb, SEL_QBLOCK, G, R, D).transpose(1, 0, 2, 3, 4, 5)
    idx_blocks = sel_idx.reshape(B, G, nqb, SEL_QBLOCK, n_sel).transpose(2, 0, 1, 3, 4)
    t_blocks = jnp.arange(S).reshape(nqb, SEL_QBLOCK)
    n_keys = n_sel * SLC_BLOCK

    def block(args):
        qb, ib, tb = args
        tok = (ib[..., None] * SLC_BLOCK + jnp.arange(SLC_BLOCK)).reshape(B, G, SEL_QBLOCK * n_keys)
        ksel = kt[bi, gi, tok].reshape(B, G, SEL_QBLOCK, n_keys, D)
        vsel = vt[bi, gi, tok].reshape(B, G, SEL_QBLOCK, n_keys, D)
        s = jnp.einsum('bqgrd,bgqkd->bgrqk', qb, ksel) * (D ** -0.5)
        mask = tok.reshape(B, G, SEL_QBLOCK, n_keys) <= tb[None, None, :, None]
        p = masked_softmax(s, mask[:, :, None])
        return jnp.einsum('bgrqk,bgqkd->bqgrd', p.astype(vsel.dtype), vsel)

    out = lax.map(block, (q_blocks, idx_blocks, t_blocks))
    return out.transpose(1, 0, 2, 3, 4, 5).reshape(B, S, G, R, D)


def nsa_window_attention(q, k, v):
    B, S, G, R, D = q.shape
    nqb = S // WIN_QBLOCK
    kw = WINDOW + WIN_QBLOCK
    kp = jnp.pad(k, ((0, 0), (WINDOW, 0), (0, 0), (0, 0)))
    vp = jnp.pad(v, ((0, 0), (WINDOW, 0), (0, 0), (0, 0)))
    q_blocks = q.reshape(B, nqb, WIN_QBLOCK, G, R, D).transpose(1, 0, 2, 3, 4, 5)

    def block(args):
        i, qb = args
        s0 = i * WIN_QBLOCK
        kb = lax.dynamic_slice_in_dim(kp, s0, kw, axis=1)
        vb = lax.dynamic_slice_in_dim(vp, s0, kw, axis=1)
        tq = s0 + jnp.arange(WIN_QBLOCK)
        tk = s0 - WINDOW + jnp.arange(kw)
        diff = tq[:, None] - tk[None, :]
        mask = (diff >= 0) & (diff < WINDOW) & (tk[None, :] >= 0)
        s = jnp.einsum('bqgrd,bkgd->bgrqk', qb, kb) * (D ** -0.5)
        p = masked_softmax(s, mask)
        return jnp.einsum('bgrqk,bkgd->bqgrd', p.astype(vb.dtype), vb)

    out = lax.map(block, (jnp.arange(nqb), q_blocks))
    return out.transpose(1, 0, 2, 3, 4, 5).reshape(B, S, G, R, D)


def ssd_chunked(x, dt, a, bm, cm):
    B, S, G, E, P = x.shape
    N = bm.shape[-1]
    nc, l = S // SSM_CHUNK, SSM_CHUNK
    xd = (x.astype(jnp.float32) * dt[..., None]).reshape(B, nc, l, G, E, P)
    ad = (dt * a).reshape(B, nc, l, G, E).transpose(0, 3, 4, 1, 2)
    bc = bm.astype(jnp.float32).reshape(B, nc, l, G, N)
    cc = cm.astype(jnp.float32).reshape(B, nc, l, G, N)
    a_cum = jnp.cumsum(ad, axis=-1)
    tril = np.tril(np.ones((l, l), dtype=bool))
    seg = jnp.where(tril, a_cum[..., :, None] - a_cum[..., None, :], -jnp.inf)
    L = jnp.exp(seg)
    y_diag = jnp.einsum('bclgn,bcsgn,bgecls,bcsgep->bclgep', cc, bc, L, xd)
    decay_states = jnp.exp(a_cum[..., -1:] - a_cum)
    states = jnp.einsum('bclgn,bgecl,bclgep->bcgepn', bc, decay_states, xd)
    chunk_decay = jnp.exp(a_cum[..., -1])

    def step(h, inp):
        st, dec = inp
        return dec[..., None, None] * h + st, h

    _, prev = lax.scan(step, jnp.zeros((B, G, E, P, N), jnp.float32),
                       (states.transpose(1, 0, 2, 3, 4, 5), chunk_decay.transpose(3, 0, 1, 2)))
    prev = prev.transpose(1, 0, 2, 3, 4, 5)
    y_off = jnp.einsum('bclgn,bcgepn,bgecl->bclgep', cc, prev, jnp.exp(a_cum))
    return (y_diag + y_off).reshape(B, S, G, E, P).astype(x.dtype)


def gated_rmsnorm(y, z, w):
    B, S, _ = y.shape
    yz = (y * jax.nn.silu(z)).astype(jnp.float32).reshape(B, S, SSM_GROUPS, -1)
    yz = yz * lax.rsqrt(jnp.mean(jnp.square(yz), axis=-1, keepdims=True) + LN_EPS)
    return yz.reshape(B, S, SSM_DIM).astype(y.dtype) * w


def nsa_ssd_mixer(h, positions, w_in, cmp_pos, cmp_w1, cmp_w2, conv_w, conv_b, dt_bias, a_log, d_skip, norm_w, w_out):
    B, S, _ = h.shape
    G, R = NSA_KV_HEADS, NSA_REP
    q, kc, vc, ksl, vsl, kw, vw, gates, z, xbc, dt = _split(h @ w_in, E_SPLIT_SIZES)
    kv = lambda t: t.reshape(B, S, G, HEAD_DIM)
    q = rope(q.reshape(B, S, NSA_HEADS, HEAD_DIM), positions).reshape(B, S, G, R, HEAD_DIM)
    n_cmp = (S - CMP_BLOCK) // CMP_STRIDE + 1
    blk = np.arange(n_cmp)[:, None] * CMP_STRIDE + np.arange(CMP_BLOCK)[None, :]
    k_cmp = rope(nsa_compress(kv(kc), blk, cmp_pos[0], cmp_w1[0], cmp_w2[0]), positions[:, blk[:, -1]])
    v_cmp = nsa_compress(kv(vc), blk, cmp_pos[1], cmp_w1[1], cmp_w2[1])
    o_cmp, p_cmp = nsa_compressed_attention(q, k_cmp, v_cmp, blk[:, -1])
    sel_idx = nsa_select_blocks(p_cmp, blk[:, 0])
    o_slc = nsa_selected_attention(q, rope(kv(ksl), positions), kv(vsl), sel_idx)
    o_win = nsa_window_attention(q, rope(kv(kw), positions), kv(vw))
    g = jax.nn.sigmoid(gates.astype(jnp.float32)).reshape(B, S, G, R, 3).astype(h.dtype)
    o_nsa = (g[..., 0:1] * o_cmp + g[..., 1:2] * o_slc + g[..., 2:3] * o_win).reshape(B, S, NSA_Q_DIM)
    xbc = jax.nn.silu(causal_dwconv(xbc, conv_w, conv_b))
    xs, bm, cm = _split(xbc, (SSM_DIM, SSM_GROUPS * SSM_STATE, SSM_GROUPS * SSM_STATE))
    E = SSM_HEADS // SSM_GROUPS
    dt = jax.nn.softplus(dt.astype(jnp.float32) + dt_bias.astype(jnp.float32))
    a = -jnp.exp(a_log.astype(jnp.float32))
    y = ssd_chunked(xs.reshape(B, S, SSM_GROUPS, E, SSM_HEADDIM), dt.reshape(B, S, SSM_GROUPS, E),
                    a.reshape(SSM_GROUPS, E), bm.reshape(B, S, SSM_GROUPS, SSM_STATE),
                    cm.reshape(B, S, SSM_GROUPS, SSM_STATE))
    y = y.reshape(B, S, SSM_HEADS, SSM_HEADDIM) + d_skip[:, None] * xs.reshape(B, S, SSM_HEADS, SSM_HEADDIM)
    o_ssm = gated_rmsnorm(y.reshape(B, S, SSM_DIM), z, norm_w)
    return jnp.concatenate([o_nsa, o_ssm], axis=-1) @ w_out


def fox_attention(q, k, v, logf):
    B, S, H, D = q.shape
    F = jnp.cumsum(logf, axis=1).transpose(0, 2, 1)
    kpos = jnp.arange(S)

    def block(i):
        s0 = i * FOX_QBLOCK
        qb = lax.dynamic_slice_in_dim(q, s0, FOX_QBLOCK, axis=1)
        fq = lax.dynamic_slice_in_dim(F, s0, FOX_QBLOCK, axis=2)
        s = (jnp.einsum('bqhd,bkhd->bhqk', qb, k).astype(jnp.float32) * (D ** -0.5)
             + fq[..., :, None] - F[..., None, :])
        mask = kpos[None, :] <= (s0 + jnp.arange(FOX_QBLOCK))[:, None]
        p = masked_softmax(s, mask)
        return jnp.einsum('bhqk,bkhd->bqhd', p.astype(v.dtype), v)

    out = lax.map(block, jnp.arange(S // FOX_QBLOCK))
    return out.transpose(1, 0, 2, 3, 4).reshape(B, S, H, D)


def fox_conformer_mixer(h, w_in, f_bias, conv_w, conv_b, conf_ln_g, conf_ln_b, w_out):
    B, S, _ = h.shape
    q, k, v, f, glu = _split(h @ w_in, O_SPLIT_SIZES)
    hs = lambda t: t.reshape(B, S, FOX_HEADS, HEAD_DIM)
    logf = jax.nn.log_sigmoid(f.astype(jnp.float32) + f_bias.astype(jnp.float32))
    o_fox = fox_attention(hs(q), hs(k), hs(v), logf).reshape(B, S, FOX_DIM)
    u_val, u_gate = jnp.split(glu, 2, axis=-1)
    u = causal_dwconv(u_val * jax.nn.sigmoid(u_gate), conv_w, conv_b)
    u = jax.nn.silu(layer_norm(u, conf_ln_g, conf_ln_b))
    return jnp.concatenate([o_fox, u], axis=-1) @ w_out


def moe_swiglu(h, router, w_gate, w_up, w_down):
    B, S, D = h.shape
    t = h.reshape(B * S, D)
    probs = jax.nn.softmax((t @ router).astype(jnp.float32), axis=-1)
    top_p, top_i = lax.top_k(probs, TOP_K)
    top_p = top_p / jnp.sum(top_p, axis=-1, keepdims=True)
    gates = jnp.sum(jax.nn.one_hot(top_i, N_EXPERTS, dtype=jnp.float32) * top_p[..., None], axis=1).astype(h.dtype)
    out = jnp.zeros_like(t)
    for e in range(N_EXPERTS):
        out = out + gates[:, e:e + 1] * swiglu(t, w_gate[e], w_up[e], w_down[e])
    return out.reshape(B, S, D)


def _modulation(mod, table):
    m = mod + table[None]
    return [m[:, k, None, :] for k in range(6)]


def setup_inputs(seed: int = 0) -> dict:
    key = jax.random.key(seed)
    ks = iter(jax.random.split(key, 64))
    D, NE, NO = D_MODEL, N_EVEN, N_ODD

    def nrm(shape, scale):
        return scale * jax.random.normal(next(ks), shape, jnp.float32)

    def unif(shape, lo, hi):
        return jax.random.uniform(next(ks), shape, jnp.float32, lo, hi)

    x = nrm((BATCH, SEQ, D), 1.0)
    c = nrm((BATCH, D), 1.0)
    positions = (jnp.arange(SEQ, dtype=jnp.int32)[None, :]
                 + jax.random.randint(next(ks), (BATCH, 1), 0, SEQ, jnp.int32))
    dt0 = jnp.exp(unif((NE, SSM_HEADS), math.log(1e-3), math.log(1e-1)))
    return dict(
        x=x, c=c, positions=positions,
        w_ada=nrm((D, 6 * D), 0.5 * D ** -0.5), b_ada=nrm((6 * D,), 0.02),
        e_ada_table=nrm((NE, 6, D), D ** -0.5),
        e_ln1_g=1.0 + nrm((NE, D), 0.02), e_ln1_b=nrm((NE, D), 0.02),
        e_ln2_g=1.0 + nrm((NE, D), 0.02), e_ln2_b=nrm((NE, D), 0.02),
        e_w_in=nrm((NE, D, E_IN_DIM), D ** -0.5),
        e_cmp_pos=nrm((NE, 2, CMP_BLOCK, HEAD_DIM), 0.1),
        e_cmp_w1=nrm((NE, 2, CMP_BLOCK * HEAD_DIM, HEAD_DIM), (CMP_BLOCK * HEAD_DIM) ** -0.5),
        e_cmp_w2=nrm((NE, 2, HEAD_DIM, HEAD_DIM), HEAD_DIM ** -0.5),
        e_conv_w=nrm((NE, SSM_CONV, SSM_CONV_DIM), SSM_CONV ** -0.5),
        e_conv_b=nrm((NE, SSM_CONV_DIM), 0.02),
        e_dt_bias=dt0 + jnp.log(-jnp.expm1(-dt0)),
        e_a_log=jnp.log(unif((NE, SSM_HEADS), 1.0, 16.0)),
        e_d_skip=1.0 + nrm((NE, SSM_HEADS), 0.1),
        e_ssm_norm_w=1.0 + nrm((NE, SSM_DIM), 0.02),
        e_w_out=nrm((NE, MIX_DIM, D), DEEPNORM_BETA * MIX_DIM ** -0.5),
        e_ffn_w_gate=nrm((NE, D, D_FF), D ** -0.5),
        e_ffn_w_up=nrm((NE, D, D_FF), D ** -0.5),
        e_ffn_w_down=nrm((NE, D_FF, D), DEEPNORM_BETA * D_FF ** -0.5),
        o_ada_table=nrm((NO, 6, D), D ** -0.5),
        o_ln1_g=1.0 + nrm((NO, D), 0.02), o_ln1_b=nrm((NO, D), 0.02),
        o_ln2_g=1.0 + nrm((NO, D), 0.02), o_ln2_b=nrm((NO, D), 0.02),
        o_w_in=nrm((NO, D, O_IN_DIM), D ** -0.5),
        o_fox_f_bias=unif((NO, FOX_HEADS), 2.0, 6.0),
        o_conf_conv_w=nrm((NO, CONF_KERNEL, CONF_DIM), CONF_KERNEL ** -0.5),
        o_conf_conv_b=nrm((NO, CONF_DIM), 0.02),
        o_conf_ln_g=1.0 + nrm((NO, CONF_DIM), 0.02), o_conf_ln_b=nrm((NO, CONF_DIM), 0.02),
        o_w_out=nrm((NO, MIX_DIM, D), DEEPNORM_BETA * MIX_DIM ** -0.5),
        o_moe_router=nrm((NO, D, N_EXPERTS), D ** -0.5),
        o_moe_w_gate=nrm((NO, N_EXPERTS, D, D_EXPERT), D ** -0.5),
        o_moe_w_up=nrm((NO, N_EXPERTS, D, D_EXPERT), D ** -0.5),
        o_moe_w_down=nrm((NO, N_EXPERTS, D_EXPERT, D), DEEPNORM_BETA * D_EXPERT ** -0.5),
    )


def reference(x, c, positions, w_ada, b_ada,
              e_ada_table, e_ln1_g, e_ln1_b, e_ln2_g, e_ln2_b, e_w_in, e_cmp_pos, e_cmp_w1, e_cmp_w2,
              e_conv_w, e_conv_b, e_dt_bias, e_a_log, e_d_skip, e_ssm_norm_w, e_w_out,
              e_ffn_w_gate, e_ffn_w_up, e_ffn_w_down,
              o_ada_table, o_ln1_g, o_ln1_b, o_ln2_g, o_ln2_b, o_w_in, o_fox_f_bias,
              o_conf_conv_w, o_conf_conv_b, o_conf_ln_g, o_conf_ln_b, o_w_out,
              o_moe_router, o_moe_w_gate, o_moe_w_up, o_moe_w_down):
    B = x.shape[0]
    mod = (jax.nn.silu(c) @ w_ada + b_ada).reshape(B, 6, D_MODEL)
    for i in range(DEPTH):
        j = i // 2
        if i % 2 == 0:
            sh1, sc1, g1, sh2, sc2, g2 = _modulation(mod, e_ada_table[j])
            y = nsa_ssd_mixer(x * (1.0 + sc1) + sh1, positions, e_w_in[j], e_cmp_pos[j], e_cmp_w1[j], e_cmp_w2[j],
                              e_conv_w[j], e_conv_b[j], e_dt_bias[j], e_a_log[j], e_d_skip[j], e_ssm_norm_w[j],
                              e_w_out[j])
            x = layer_norm(DEEPNORM_ALPHA * x + (1.0 + g1) * y, e_ln1_g[j], e_ln1_b[j])
            y = swiglu(x * (1.0 + sc2) + sh2, e_ffn_w_gate[j], e_ffn_w_up[j], e_ffn_w_down[j])
            x = layer_norm(DEEPNORM_ALPHA * x + (1.0 + g2) * y, e_ln2_g[j], e_ln2_b[j])
        else:
            sh1, sc1, g1, sh2, sc2, g2 = _modulation(mod, o_ada_table[j])
            y = fox_conformer_mixer(x * (1.0 + sc1) + sh1, o_w_in[j], o_fox_f_bias[j], o_conf_conv_w[j],
                                    o_conf_conv_b[j], o_conf_ln_g[j], o_conf_ln_b[j], o_w_out[j])
            x = layer_norm(DEEPNORM_ALPHA * x + (1.0 + g1) * y, o_ln1_g[j], o_ln1_b[j])
            y = moe_swiglu(x * (1.0 + sc2) + sh2, o_moe_router[j], o_moe_w_gate[j], o_moe_w_up[j], o_moe_w_down[j])
            x = layer_norm(DEEPNORM_ALPHA * x + (1.0 + g2) * y, o_ln2_g[j], o_ln2_b[j])
    return x
```

```python
import functools
import math

import jax
import jax.numpy as jnp
import numpy as np
from jax import lax
from jax.experimental import pallas as pl
from jax.experimental.pallas import tpu as pltpu

D_MODEL = 4096
DEPTH = 4
HEAD_DIM = 128
MIX_DIM = D_MODEL

NSA_HEADS = 16
NSA_KV_HEADS = 4
NSA_REP = NSA_HEADS // NSA_KV_HEADS
NSA_Q_DIM = NSA_HEADS * HEAD_DIM
NSA_KV_DIM = NSA_KV_HEADS * HEAD_DIM
CMP_BLOCK = 32
CMP_STRIDE = 16
SLC_BLOCK = 64
N_SELECT = 16
WINDOW = 512
SEL_QBLOCK = 32
WIN_QBLOCK = 128
ROPE_THETA = 10000.0

SSM_DIM = MIX_DIM - NSA_Q_DIM
SSM_HEADDIM = 64
SSM_HEADS = SSM_DIM // SSM_HEADDIM
SSM_GROUPS = 8
SSM_STATE = 128
SSM_CONV = 4
SSM_CHUNK = 128
SSM_CONV_DIM = SSM_DIM + 2 * SSM_GROUPS * SSM_STATE

FOX_HEADS = 16
FOX_DIM = FOX_HEADS * HEAD_DIM
FOX_QBLOCK = 128

CONF_DIM = MIX_DIM - FOX_DIM
CONF_KERNEL = 31

D_FF = 11008
N_EXPERTS = 8
TOP_K = 2
D_EXPERT = 1408

DEEPNORM_ALPHA = (2.0 * DEPTH) ** 0.25
LN_EPS = 1e-5

E_SPLIT_SIZES = (NSA_Q_DIM,) + (NSA_KV_DIM,) * 6 + (3 * NSA_HEADS, SSM_DIM, SSM_CONV_DIM, SSM_HEADS)
O_SPLIT_SIZES = (FOX_DIM, FOX_DIM, FOX_DIM, FOX_HEADS, 2 * CONF_DIM)

LANES = 128
MXU_WIDTH = 256
VMEM_LIMIT_BYTES = 56 * 1024 * 1024


def _round_up(n, m):
    return (n + m - 1) // m * m


def _params(*sem):
    return pltpu.CompilerParams(dimension_semantics=sem, vmem_limit_bytes=VMEM_LIMIT_BYTES)


def _matmul_kernel(x_ref, w_ref, o_ref):
    o_ref[...] = jnp.dot(x_ref[...], w_ref[...], preferred_element_type=jnp.float32).astype(o_ref.dtype)


def _matmul_acc_kernel(x_ref, w_ref, o_ref, acc_ref):
    k = pl.program_id(2)

    @pl.when(k == 0)
    def _():
        acc_ref[...] = jnp.zeros_like(acc_ref)

    acc_ref[...] += jnp.dot(x_ref[...], w_ref[...], preferred_element_type=jnp.float32)

    @pl.when(k == pl.num_programs(2) - 1)
    def _():
        o_ref[...] = acc_ref[...].astype(o_ref.dtype)


def matmul(x, w, out_dtype, tm, tn, tk=None):
    m, k = x.shape
    _, n = w.shape
    tk = k if tk is None else tk
    assert m % tm == 0 and n % tn == 0 and k % tk == 0
    if tk == k:
        return pl.pallas_call(
            _matmul_kernel,
            grid=(n // tn, m // tm),
            in_specs=[pl.BlockSpec((tm, k), lambda j, i: (i, 0)),
                      pl.BlockSpec((k, tn), lambda j, i: (0, j))],
            out_specs=pl.BlockSpec((tm, tn), lambda j, i: (i, j)),
            out_shape=jax.ShapeDtypeStruct((m, n), out_dtype),
            compiler_params=_params("parallel", "parallel"),
            name="matmul",
        )(x, w)
    return pl.pallas_call(
        _matmul_acc_kernel,
        grid=(n // tn, m // tm, k // tk),
        in_specs=[pl.BlockSpec((tm, tk), lambda j, i, kk: (i, kk)),
                  pl.BlockSpec((tk, tn), lambda j, i, kk: (kk, j))],
        out_specs=pl.BlockSpec((tm, tn), lambda j, i, kk: (i, j)),
        out_shape=jax.ShapeDtypeStruct((m, n), out_dtype),
        scratch_shapes=[pltpu.VMEM((tm, tn), jnp.float32)],
        compiler_params=_params("parallel", "parallel", "arbitrary"),
        name="matmul_acc",
    )(x, w)


def _swiglu_up_kernel(x_ref, wg_ref, wu_ref, o_ref):
    x = x_ref[...]
    g = jnp.dot(x, wg_ref[...], preferred_element_type=jnp.float32)
    u = jnp.dot(x, wu_ref[...], preferred_element_type=jnp.float32)
    o_ref[...] = (g * jax.nn.sigmoid(g) * u).astype(o_ref.dtype)


def swiglu_up(x, wg, wu, tm, tn):
    m, k = x.shape
    _, n = wg.shape
    assert m % tm == 0 and n % tn == 0
    return pl.pallas_call(
        _swiglu_up_kernel,
        grid=(n // tn, m // tm),
        in_specs=[pl.BlockSpec((tm, k), lambda j, i: (i, 0)),
                  pl.BlockSpec((k, tn), lambda j, i: (0, j)),
                  pl.BlockSpec((k, tn), lambda j, i: (0, j))],
        out_specs=pl.BlockSpec((tm, tn), lambda j, i: (i, j)),
        out_shape=jax.ShapeDtypeStruct((m, n), jnp.bfloat16),
        compiler_params=_params("parallel", "parallel"),
        name="swiglu_up",
    )(x, wg, wu)


def _ada_kernel(c_ref, w_ref, b_ref, o_ref):
    c = c_ref[...]
    a = c * jax.nn.sigmoid(c)
    o_ref[...] = jnp.dot(a, w_ref[...], preferred_element_type=jnp.float32,
                         precision=lax.Precision.HIGHEST) + b_ref[...]


def ada_projection(c, w_ada, b_ada):
    b, d = c.shape
    n = w_ada.shape[1]
    rows = _round_up(b, 8)
    c_pad = jnp.pad(c, ((0, rows - b), (0, 0)))
    tn = 512
    out = pl.pallas_call(
        _ada_kernel,
        grid=(n // tn,),
        in_specs=[pl.BlockSpec((rows, d), lambda j: (0, 0)),
                  pl.BlockSpec((d, tn), lambda j: (0, j)),
                  pl.BlockSpec((1, tn), lambda j: (0, j))],
        out_specs=pl.BlockSpec((rows, tn), lambda j: (0, j)),
        out_shape=jax.ShapeDtypeStruct((rows, n), jnp.float32),
        compiler_params=_params("parallel"),
        name="ada_projection",
    )(c_pad, w_ada, b_ada.reshape(1, n))
    return out[:b]


def _modulate_kernel(x_ref, sc_ref, sh_ref, h_ref):
    h_ref[...] = (x_ref[...] * (1.0 + sc_ref[0]) + sh_ref[0]).astype(h_ref.dtype)


def modulate(x, sc, sh, seq, tr=256):
    t, d = x.shape
    per_b = seq // tr
    vec = pl.BlockSpec((1, 1, d), lambda i: (i // per_b, 0, 0))
    return pl.pallas_call(
        _modulate_kernel,
        grid=(t // tr,),
        in_specs=[pl.BlockSpec((tr, d), lambda i: (i, 0)), vec, vec],
        out_specs=pl.BlockSpec((tr, d), lambda i: (i, 0)),
        out_shape=jax.ShapeDtypeStruct((t, d), jnp.bfloat16),
        compiler_params=_params("parallel"),
        name="modulate",
    )(x, sc, sh)


def _res_ln_kernel(x_ref, y_ref, gate_ref, g_ref, b_ref, sc_ref, sh_ref, xn_ref, h_ref):
    v = DEEPNORM_ALPHA * x_ref[...] + (1.0 + gate_ref[0]) * y_ref[...].astype(jnp.float32)
    mu = jnp.mean(v, axis=-1, keepdims=True)
    vc = v - mu
    var = jnp.mean(vc * vc, axis=-1, keepdims=True)
    xn = vc * lax.rsqrt(var + LN_EPS) * g_ref[...] + b_ref[...]
    xn_ref[...] = xn
    h_ref[...] = (xn * (1.0 + sc_ref[0]) + sh_ref[0]).astype(h_ref.dtype)


def residual_layer_norm(x, y, gate, ln_g, ln_b, sc_next, sh_next, seq, tr=256):
    t, d = x.shape
    per_b = seq // tr
    row = pl.BlockSpec((tr, d), lambda i: (i, 0))
    vec_b = pl.BlockSpec((1, 1, d), lambda i: (i // per_b, 0, 0))
    vec = pl.BlockSpec((1, d), lambda i: (0, 0))
    return pl.pallas_call(
        _res_ln_kernel,
        grid=(t // tr,),
        in_specs=[row, row, vec_b, vec, vec, vec_b, vec_b],
        out_specs=[row, row],
        out_shape=[jax.ShapeDtypeStruct((t, d), jnp.float32), jax.ShapeDtypeStruct((t, d), jnp.bfloat16)],
        compiler_params=_params("parallel"),
        name="residual_layer_norm",
    )(x, y, gate, ln_g.reshape(1, d), ln_b.reshape(1, d), sc_next, sh_next)


def _split(a, sizes):
    return jnp.split(a, np.cumsum(sizes)[:-1].tolist(), axis=-1)


def _layer_norm(x, g, b):
    mu = jnp.mean(x, axis=-1, keepdims=True)
    var = jnp.mean(jnp.square(x - mu), axis=-1, keepdims=True)
    return (x - mu) * lax.rsqrt(var + LN_EPS) * g + b


def _masked_softmax(s, mask):
    s = jnp.where(mask, s.astype(jnp.float32), -jnp.inf)
    m = jnp.max(s, axis=-1, keepdims=True)
    m = jnp.where(jnp.isfinite(m), m, 0.0)
    e = jnp.exp(s - m)
    d = jnp.sum(e, axis=-1, keepdims=True)
    return e / jnp.where(d > 0, d, 1.0)


def _rope(x, pos):
    half = x.shape[-1] // 2
    inv = ROPE_THETA ** (-jnp.arange(half, dtype=jnp.float32) / half)
    ang = pos.astype(jnp.float32)[..., None] * inv
    cos = jnp.cos(ang)[..., None, :]
    sin = jnp.sin(ang)[..., None, :]
    x1, x2 = x[..., :half], x[..., half:]
    return jnp.concatenate([x1 * cos - x2 * sin, x2 * cos + x1 * sin], axis=-1)


def _causal_dwconv(x, w, b):
    k, ch = w.shape
    y = lax.conv_general_dilated(x, w[:, None, :], window_strides=(1,), padding=[(k - 1, 0)],
                                 dimension_numbers=('NWC', 'WIO', 'NWC'), feature_group_count=ch)
    return y + b


def _nsa_compress(k_raw, blk, pos_emb, w1, w2):
    B, _, G, _ = k_raw.shape
    n_cmp = blk.shape[0]
    blocks = k_raw[:, blk] + pos_emb[None, None, :, None, :]
    blocks = blocks.transpose(0, 1, 3, 2, 4).reshape(B, n_cmp, G, CMP_BLOCK * HEAD_DIM)
    return jax.nn.gelu(blocks @ w1) @ w2


def _nsa_compressed_attention(q, k_cmp, v_cmp, cmp_end):
    S = q.shape[1]
    s = jnp.einsum('bsgrd,bngd->bgrsn', q, k_cmp) * (HEAD_DIM ** -0.5)
    mask = cmp_end[None, :] <= np.arange(S)[:, None]
    p = _masked_softmax(s, mask)
    o = jnp.einsum('bgrsn,bngd->bsgrd', p, v_cmp)
    return o, p


def _nsa_select_blocks(p_cmp, cmp_start):
    S = p_cmp.shape[3]
    n_slc = S // SLC_BLOCK
    n_sel = min(N_SELECT, n_slc)
    slc_start = np.arange(n_slc) * SLC_BLOCK
    overlap = ((cmp_start[:, None] < slc_start[None, :] + SLC_BLOCK)
               & (cmp_start[:, None] + CMP_BLOCK > slc_start[None, :])).astype(np.float32)
    imp = jnp.einsum('bgrsn,nj->bgsj', p_cmp, jnp.asarray(overlap))
    qblk = np.arange(S)[:, None] // SLC_BLOCK
    j = np.arange(n_slc)[None, :]
    forced = (j == 0) | (j == qblk) | (j == qblk - 1)
    future = j > qblk
    imp = jnp.where(forced, jnp.inf, jnp.where(future, -jnp.inf, imp))
    return lax.top_k(imp, n_sel)[1]


def _nsa_selected_attention(q, k, v, sel_idx):
    B, S, G, R, D = q.shape
    n_sel = sel_idx.shape[-1]
    nqb = S // SEL_QBLOCK
    kt = k.transpose(0, 2, 1, 3)
    vt = v.transpose(0, 2, 1, 3)
    bi = jnp.arange(B)[:, None, None]
    gi = jnp.arange(G)[None, :, None]
    q_blocks = q.reshape(B, nqb, SEL_QBLOCK, G, R, D).transpose(1, 0, 2, 3, 4, 5)
    idx_blocks = sel_idx.reshape(B, G, nqb, SEL_QBLOCK, n_sel).transpose(2, 0, 1, 3, 4)
    t_blocks = jnp.arange(S).reshape(nqb, SEL_QBLOCK)
    n_keys = n_sel * SLC_BLOCK

    def block(args):
        qb, ib, tb = args
        tok = (ib[..., None] * SLC_BLOCK + jnp.arange(SLC_BLOCK)).reshape(B, G, SEL_QBLOCK * n_keys)
        ksel = kt[bi, gi, tok].reshape(B, G, SEL_QBLOCK, n_keys, D)
        vsel = vt[bi, gi, tok].reshape(B, G, SEL_QBLOCK, n_keys, D)
        s = jnp.einsum('bqgrd,bgqkd->bgrqk', qb, ksel) * (D ** -0.5)
        mask = tok.reshape(B, G, SEL_QBLOCK, n_keys) <= tb[None, None, :, None]
        p = _masked_softmax(s, mask[:, :, None])
        return jnp.einsum('bgrqk,bgqkd->bqgrd', p, vsel)

    out = lax.map(block, (q_blocks, idx_blocks, t_blocks))
    return out.transpose(1, 0, 2, 3, 4, 5).reshape(B, S, G, R, D)


def _nsa_window_attention(q, k, v):
    B, S, G, R, D = q.shape
    nqb = S // WIN_QBLOCK
    kw = WINDOW + WIN_QBLOCK
    kp = jnp.pad(k, ((0, 0), (WINDOW, 0), (0, 0), (0, 0)))
    vp = jnp.pad(v, ((0, 0), (WINDOW, 0), (0, 0), (0, 0)))
    q_blocks = q.reshape(B, nqb, WIN_QBLOCK, G, R, D).transpose(1, 0, 2, 3, 4, 5)

    def block(args):
        i, qb = args
        s0 = i * WIN_QBLOCK
        kb = lax.dynamic_slice_in_dim(kp, s0, kw, axis=1)
        vb = lax.dynamic_slice_in_dim(vp, s0, kw, axis=1)
        tq = s0 + jnp.arange(WIN_QBLOCK)
        tk = s0 - WINDOW + jnp.arange(kw)
        diff = tq[:, None] - tk[None, :]
        mask = (diff >= 0) & (diff < WINDOW) & (tk[None, :] >= 0)
        s = jnp.einsum('bqgrd,bkgd->bgrqk', qb, kb) * (D ** -0.5)
        p = _masked_softmax(s, mask)
        return jnp.einsum('bgrqk,bkgd->bqgrd', p, vb)

    out = lax.map(block, (jnp.arange(nqb), q_blocks))
    return out.transpose(1, 0, 2, 3, 4, 5).reshape(B, S, G, R, D)


def _ssd_chunked(x, dt, a, bm, cm):
    B, S, G, E, P = x.shape
    N = bm.shape[-1]
    nc, l = S // SSM_CHUNK, SSM_CHUNK
    xd = (x * dt[..., None]).reshape(B, nc, l, G, E, P)
    ad = (dt * a).reshape(B, nc, l, G, E).transpose(0, 3, 4, 1, 2)
    bc = bm.reshape(B, nc, l, G, N)
    cc = cm.reshape(B, nc, l, G, N)
    a_cum = jnp.cumsum(ad, axis=-1)
    tril = np.tril(np.ones((l, l), dtype=bool))
    seg = jnp.where(tril, a_cum[..., :, None] - a_cum[..., None, :], -jnp.inf)
    L = jnp.exp(seg)
    y_diag = jnp.einsum('bclgn,bcsgn,bgecls,bcsgep->bclgep', cc, bc, L, xd)
    decay_states = jnp.exp(a_cum[..., -1:] - a_cum)
    states = jnp.einsum('bclgn,bgecl,bclgep->bcgepn', bc, decay_states, xd)
    chunk_decay = jnp.exp(a_cum[..., -1])

    def step(h, inp):
        st, dec = inp
        return dec[..., None, None] * h + st, h

    _, prev = lax.scan(step, jnp.zeros((B, G, E, P, N), jnp.float32),
                       (states.transpose(1, 0, 2, 3, 4, 5), chunk_decay.transpose(3, 0, 1, 2)))
    prev = prev.transpose(1, 0, 2, 3, 4, 5)
    y_off = jnp.einsum('bclgn,bcgepn,bgecl->bclgep', cc, prev, jnp.exp(a_cum))
    return (y_diag + y_off).reshape(B, S, G, E, P)


def _gated_rmsnorm(y, z, w):
    B, S, _ = y.shape
    yz = (y * jax.nn.silu(z)).reshape(B, S, SSM_GROUPS, -1)
    yz = yz * lax.rsqrt(jnp.mean(jnp.square(yz), axis=-1, keepdims=True) + LN_EPS)
    return yz.reshape(B, S, SSM_DIM) * w


def _nsa_ssd_heads(proj, positions, cmp_pos, cmp_w1, cmp_w2, conv_w, conv_b, dt_bias, a_log, d_skip, norm_w):
    B, S, _ = proj.shape
    G, R = NSA_KV_HEADS, NSA_REP
    q, kc, vc, ksl, vsl, kw, vw, gates, z, xbc, dt = _split(proj, E_SPLIT_SIZES)
    kv = lambda t: t.reshape(B, S, G, HEAD_DIM)
    q = _rope(q.reshape(B, S, NSA_HEADS, HEAD_DIM), positions).reshape(B, S, G, R, HEAD_DIM)
    n_cmp = (S - CMP_BLOCK) // CMP_STRIDE + 1
    blk = np.arange(n_cmp)[:, None] * CMP_STRIDE + np.arange(CMP_BLOCK)[None, :]
    k_cmp = _rope(_nsa_compress(kv(kc), blk, cmp_pos[0], cmp_w1[0], cmp_w2[0]), positions[:, blk[:, -1]])
    v_cmp = _nsa_compress(kv(vc), blk, cmp_pos[1], cmp_w1[1], cmp_w2[1])
    o_cmp, p_cmp = _nsa_compressed_attention(q, k_cmp, v_cmp, blk[:, -1])
    sel_idx = _nsa_select_blocks(p_cmp, blk[:, 0])
    o_slc = _nsa_selected_attention(q, _rope(kv(ksl), positions), kv(vsl), sel_idx)
    o_win = _nsa_window_attention(q, _rope(kv(kw), positions), kv(vw))
    g = jax.nn.sigmoid(gates).reshape(B, S, G, R, 3)
    o_nsa = (g[..., 0:1] * o_cmp + g[..., 1:2] * o_slc + g[..., 2:3] * o_win).reshape(B, S, NSA_Q_DIM)
    xbc = jax.nn.silu(_causal_dwconv(xbc, conv_w, conv_b))
    xs, bm, cm = _split(xbc, (SSM_DIM, SSM_GROUPS * SSM_STATE, SSM_GROUPS * SSM_STATE))
    E = SSM_HEADS // SSM_GROUPS
    dt = jax.nn.softplus(dt + dt_bias)
    a = -jnp.exp(a_log)
    y = _ssd_chunked(xs.reshape(B, S, SSM_GROUPS, E, SSM_HEADDIM), dt.reshape(B, S, SSM_GROUPS, E),
                     a.reshape(SSM_GROUPS, E), bm.reshape(B, S, SSM_GROUPS, SSM_STATE),
                     cm.reshape(B, S, SSM_GROUPS, SSM_STATE))
    y = y.reshape(B, S, SSM_HEADS, SSM_HEADDIM) + d_skip[:, None] * xs.reshape(B, S, SSM_HEADS, SSM_HEADDIM)
    o_ssm = _gated_rmsnorm(y.reshape(B, S, SSM_DIM), z, norm_w)
    return jnp.concatenate([o_nsa, o_ssm], axis=-1)


def _fox_attention(q, k, v, logf):
    B, S, H, D = q.shape
    F = jnp.cumsum(logf, axis=1).transpose(0, 2, 1)
    kpos = jnp.arange(S)

    def block(i):
        s0 = i * FOX_QBLOCK
        qb = lax.dynamic_slice_in_dim(q, s0, FOX_QBLOCK, axis=1)
        fq = lax.dynamic_slice_in_dim(F, s0, FOX_QBLOCK, axis=2)
        s = (jnp.einsum('bqhd,bkhd->bhqk', qb, k) * (D ** -0.5) + fq[..., :, None] - F[..., None, :])
        mask = kpos[None, :] <= (s0 + jnp.arange(FOX_QBLOCK))[:, None]
        p = _masked_softmax(s, mask)
        return jnp.einsum('bhqk,bkhd->bqhd', p, v)

    out = lax.map(block, jnp.arange(S // FOX_QBLOCK))
    return out.transpose(1, 0, 2, 3, 4).reshape(B, S, H, D)


def _fox_conformer_heads(proj, f_bias, conv_w, conv_b, conf_ln_g, conf_ln_b):
    B, S, _ = proj.shape
    q, k, v, f, glu = _split(proj, O_SPLIT_SIZES)
    hs = lambda t: t.reshape(B, S, FOX_HEADS, HEAD_DIM)
    logf = jax.nn.log_sigmoid(f + f_bias)
    o_fox = _fox_attention(hs(q), hs(k), hs(v), logf).reshape(B, S, FOX_DIM)
    u_val, u_gate = jnp.split(glu, 2, axis=-1)
    u = _causal_dwconv(u_val * jax.nn.sigmoid(u_gate), conv_w, conv_b)
    u = jax.nn.silu(_layer_norm(u, conf_ln_g, conf_ln_b))
    return jnp.concatenate([o_fox, u], axis=-1)


def _moe_gates(t, router):
    probs = jax.nn.softmax(t.astype(jnp.float32) @ router, axis=-1)
    top_p, top_i = lax.top_k(probs, TOP_K)
    top_p = top_p / jnp.sum(top_p, axis=-1, keepdims=True)
    return jnp.sum(jax.nn.one_hot(top_i, N_EXPERTS, dtype=jnp.float32) * top_p[..., None], axis=1)


TM = 512
FF_PAD = _round_up(D_FF, 1024)


def _bf16(w):
    return w.astype(jnp.bfloat16)


def kernel(x, c, positions, w_ada, b_ada, e_ada_table, e_ln1_g, e_ln1_b, e_ln2_g, e_ln2_b, e_w_in, e_cmp_pos, e_cmp_w1, e_cmp_w2, e_conv_w, e_conv_b, e_dt_bias, e_a_log, e_d_skip, e_ssm_norm_w, e_w_out, e_ffn_w_gate, e_ffn_w_up, e_ffn_w_down, o_ada_table, o_ln1_g, o_ln1_b, o_ln2_g, o_ln2_b, o_w_in, o_fox_f_bias, o_conf_conv_w, o_conf_conv_b, o_conf_ln_g, o_conf_ln_b, o_w_out, o_moe_router, o_moe_w_gate, o_moe_w_up, o_moe_w_down):
    B, S, D = x.shape
    T = B * S
    mod = ada_projection(c, w_ada, b_ada).reshape(B, 6, D)

    def mods(table):
        m = mod + table[None]
        return [m[:, k, None, :] for k in range(6)]

    layer_mods = [mods(e_ada_table[i // 2]) if i % 2 == 0 else mods(o_ada_table[i // 2]) for i in range(DEPTH)]
    xf = x.reshape(T, D)
    h = modulate(xf, layer_mods[0][1], layer_mods[0][0], S)
    for i in range(DEPTH):
        j = i // 2
        sh1, sc1, g1, sh2, sc2, g2 = layer_mods[i]
        if i % 2 == 0:
            n_in = _round_up(sum(E_SPLIT_SIZES), 1280)
            w_in = _bf16(jnp.pad(e_w_in[j], ((0, 0), (0, n_in - sum(E_SPLIT_SIZES)))))
            proj = matmul(h, w_in, jnp.float32, TM, 1280)[:, :sum(E_SPLIT_SIZES)]
            mix = _nsa_ssd_heads(proj.reshape(B, S, -1), positions, e_cmp_pos[j], e_cmp_w1[j], e_cmp_w2[j],
                                 e_conv_w[j], e_conv_b[j], e_dt_bias[j], e_a_log[j], e_d_skip[j], e_ssm_norm_w[j])
            y = matmul(_bf16(mix.reshape(T, MIX_DIM)), _bf16(e_w_out[j]), jnp.float32, TM, 1024)
            xf, h = residual_layer_norm(xf, y, g1, e_ln1_g[j], e_ln1_b[j], sc2, sh2, S)
            pad_ff = ((0, 0), (0, FF_PAD - D_FF))
            hid = swiglu_up(h, _bf16(jnp.pad(e_ffn_w_gate[j], pad_ff)), _bf16(jnp.pad(e_ffn_w_up[j], pad_ff)), TM, 1024)
            w_down = _bf16(jnp.pad(e_ffn_w_down[j], ((0, FF_PAD - D_FF), (0, 0))))
            y = matmul(hid, w_down, jnp.float32, TM, 1024, tk=FF_PAD // 4)
            ln_g, ln_b = e_ln2_g[j], e_ln2_b[j]
        else:
            n_in = _round_up(sum(O_SPLIT_SIZES), 1024)
            w_in = _bf16(jnp.pad(o_w_in[j], ((0, 0), (0, n_in - sum(O_SPLIT_SIZES)))))
            proj = matmul(h, w_in, jnp.float32, TM, 1024)[:, :sum(O_SPLIT_SIZES)]
            mix = _fox_conformer_heads(proj.reshape(B, S, -1), o_fox_f_bias[j], o_conf_conv_w[j], o_conf_conv_b[j],
                                       o_conf_ln_g[j], o_conf_ln_b[j])
            y = matmul(_bf16(mix.reshape(T, MIX_DIM)), _bf16(o_w_out[j]), jnp.float32, TM, 1024)
            xf, h = residual_layer_norm(xf, y, g1, o_ln1_g[j], o_ln1_b[j], sc2, sh2, S)
            gates = _moe_gates(h, o_moe_router[j])
            y = jnp.zeros((T, D), jnp.float32)
            for e in range(N_EXPERTS):
                hid = swiglu_up(h, _bf16(o_moe_w_gate[j, e]), _bf16(o_moe_w_up[j, e]), TM, D_EXPERT)
                hid = (hid.astype(jnp.float32) * gates[:, e:e + 1]).astype(jnp.bfloat16)
                y = y + matmul(hid, _bf16(o_moe_w_down[j, e]), jnp.float32, TM, 1024)
            ln_g, ln_b = o_ln2_g[j], o_ln2_b[j]
        if i + 1 < DEPTH:
            nsh, nsc = layer_mods[i + 1][0], layer_mods[i + 1][1]
        else:
            nsh, nsc = sh2, sc2
        xf, h = residual_layer_norm(xf, y, g2, ln_g, ln_b, nsc, nsh, S)
    return xf.reshape(B, S, D)
```

```python
import functools

import jax
import jax.numpy as jnp
import numpy as np
from jax import lax
from jax.experimental import pallas as pl
from jax.experimental.pallas import tpu as pltpu

D_MODEL = 4096
DEPTH = 4
HEAD_DIM = 128
MIX_DIM = D_MODEL

NSA_HEADS = 16
NSA_KV_HEADS = 4
NSA_REP = NSA_HEADS // NSA_KV_HEADS
NSA_Q_DIM = NSA_HEADS * HEAD_DIM
NSA_KV_DIM = NSA_KV_HEADS * HEAD_DIM
CMP_BLOCK = 32
CMP_STRIDE = 16
SLC_BLOCK = 64
N_SELECT = 16
WINDOW = 512
ROPE_THETA = 10000.0

SSM_DIM = MIX_DIM - NSA_Q_DIM
SSM_HEADDIM = 64
SSM_HEADS = SSM_DIM // SSM_HEADDIM
SSM_GROUPS = 8
SSM_STATE = 128
SSM_CONV = 4
SSM_CHUNK = 128
SSM_CONV_DIM = SSM_DIM + 2 * SSM_GROUPS * SSM_STATE

FOX_HEADS = 16
FOX_DIM = FOX_HEADS * HEAD_DIM

CONF_DIM = MIX_DIM - FOX_DIM
CONF_KERNEL = 31

D_FF = 11008
N_EXPERTS = 8
TOP_K = 2
D_EXPERT = 1408

DEEPNORM_ALPHA = (2.0 * DEPTH) ** 0.25
LN_EPS = 1e-5

E_SPLIT_SIZES = (NSA_Q_DIM,) + (NSA_KV_DIM,) * 6 + (3 * NSA_HEADS, SSM_DIM, SSM_CONV_DIM, SSM_HEADS)
O_SPLIT_SIZES = (FOX_DIM, FOX_DIM, FOX_DIM, FOX_HEADS, 2 * CONF_DIM)

LANES = 128
VMEM_LIMIT_BYTES = 56 * 1024 * 1024


def _round_up(n, m):
    return (n + m - 1) // m * m


def _params(*sem):
    return pltpu.CompilerParams(dimension_semantics=sem, vmem_limit_bytes=VMEM_LIMIT_BYTES)


def _matmul_kernel(x_ref, w_ref, o_ref):
    o_ref[...] = jnp.dot(x_ref[...], w_ref[...], preferred_element_type=jnp.float32).astype(o_ref.dtype)


def _matmul_acc_kernel(x_ref, w_ref, o_ref, acc_ref):
    k = pl.program_id(2)

    @pl.when(k == 0)
    def _():
        acc_ref[...] = jnp.zeros_like(acc_ref)

    acc_ref[...] += jnp.dot(x_ref[...], w_ref[...], preferred_element_type=jnp.float32)

    @pl.when(k == pl.num_programs(2) - 1)
    def _():
        o_ref[...] = acc_ref[...].astype(o_ref.dtype)


def matmul(x, w, out_dtype, tm, tn, tk=None):
    m, k = x.shape
    _, n = w.shape
    tk = k if tk is None else tk
    assert m % tm == 0 and n % tn == 0 and k % tk == 0
    if tk == k:
        return pl.pallas_call(
            _matmul_kernel,
            grid=(n // tn, m // tm),
            in_specs=[pl.BlockSpec((tm, k), lambda j, i: (i, 0)),
                      pl.BlockSpec((k, tn), lambda j, i: (0, j))],
            out_specs=pl.BlockSpec((tm, tn), lambda j, i: (i, j)),
            out_shape=jax.ShapeDtypeStruct((m, n), out_dtype),
            compiler_params=_params("parallel", "parallel"),
            name="matmul",
        )(x, w)
    return pl.pallas_call(
        _matmul_acc_kernel,
        grid=(n // tn, m // tm, k // tk),
        in_specs=[pl.BlockSpec((tm, tk), lambda j, i, kk: (i, kk)),
                  pl.BlockSpec((tk, tn), lambda j, i, kk: (kk, j))],
        out_specs=pl.BlockSpec((tm, tn), lambda j, i, kk: (i, j)),
        out_shape=jax.ShapeDtypeStruct((m, n), out_dtype),
        scratch_shapes=[pltpu.VMEM((tm, tn), jnp.float32)],
        compiler_params=_params("parallel", "parallel", "arbitrary"),
        name="matmul_acc",
    )(x, w)


def _swiglu_up_kernel(x_ref, wg_ref, wu_ref, o_ref):
    x = x_ref[...]
    g = jnp.dot(x, wg_ref[...], preferred_element_type=jnp.float32)
    u = jnp.dot(x, wu_ref[...], preferred_element_type=jnp.float32)
    o_ref[...] = (g * jax.nn.sigmoid(g) * u).astype(o_ref.dtype)


def swiglu_up(x, wg, wu, tm, tn):
    m, k = x.shape
    _, n = wg.shape
    assert m % tm == 0 and n % tn == 0
    return pl.pallas_call(
        _swiglu_up_kernel,
        grid=(n // tn, m // tm),
        in_specs=[pl.BlockSpec((tm, k), lambda j, i: (i, 0)),
                  pl.BlockSpec((k, tn), lambda j, i: (0, j)),
                  pl.BlockSpec((k, tn), lambda j, i: (0, j))],
        out_specs=pl.BlockSpec((tm, tn), lambda j, i: (i, j)),
        out_shape=jax.ShapeDtypeStruct((m, n), jnp.bfloat16),
        compiler_params=_params("parallel", "parallel"),
        name="swiglu_up",
    )(x, wg, wu)


def _ada_kernel(c_ref, w_ref, b_ref, o_ref):
    c = c_ref[...]
    a = c * jax.nn.sigmoid(c)
    o_ref[...] = jnp.dot(a, w_ref[...], preferred_element_type=jnp.float32,
                         precision=lax.Precision.HIGHEST) + b_ref[...]


def ada_projection(c, w_ada, b_ada):
    b, d = c.shape
    n = w_ada.shape[1]
    rows = _round_up(b, 8)
    c_pad = jnp.pad(c, ((0, rows - b), (0, 0)))
    tn = 512
    out = pl.pallas_call(
        _ada_kernel,
        grid=(n // tn,),
        in_specs=[pl.BlockSpec((rows, d), lambda j: (0, 0)),
                  pl.BlockSpec((d, tn), lambda j: (0, j)),
                  pl.BlockSpec((1, tn), lambda j: (0, j))],
        out_specs=pl.BlockSpec((rows, tn), lambda j: (0, j)),
        out_shape=jax.ShapeDtypeStruct((rows, n), jnp.float32),
        compiler_params=_params("parallel"),
        name="ada_projection",
    )(c_pad, w_ada, b_ada.reshape(1, n))
    return out[:b]


def _modulate_kernel(x_ref, sc_ref, sh_ref, h_ref):
    h_ref[...] = (x_ref[...] * (1.0 + sc_ref[0]) + sh_ref[0]).astype(h_ref.dtype)


def modulate(x, sc, sh, seq, tr=256):
    t, d = x.shape
    per_b = seq // tr
    vec = pl.BlockSpec((1, 1, d), lambda i: (i // per_b, 0, 0))
    return pl.pallas_call(
        _modulate_kernel,
        grid=(t // tr,),
        in_specs=[pl.BlockSpec((tr, d), lambda i: (i, 0)), vec, vec],
        out_specs=pl.BlockSpec((tr, d), lambda i: (i, 0)),
        out_shape=jax.ShapeDtypeStruct((t, d), jnp.bfloat16),
        compiler_params=_params("parallel"),
        name="modulate",
    )(x, sc, sh)


def _res_ln_kernel(x_ref, y_ref, gate_ref, g_ref, b_ref, sc_ref, sh_ref, xn_ref, h_ref):
    v = DEEPNORM_ALPHA * x_ref[...] + (1.0 + gate_ref[0]) * y_ref[...].astype(jnp.float32)
    mu = jnp.mean(v, axis=-1, keepdims=True)
    vc = v - mu
    var = jnp.mean(vc * vc, axis=-1, keepdims=True)
    xn = vc * lax.rsqrt(var + LN_EPS) * g_ref[...] + b_ref[...]
    xn_ref[...] = xn
    h_ref[...] = (xn * (1.0 + sc_ref[0]) + sh_ref[0]).astype(h_ref.dtype)


def residual_layer_norm(x, y, gate, ln_g, ln_b, sc_next, sh_next, seq, tr=256):
    t, d = x.shape
    per_b = seq // tr
    row = pl.BlockSpec((tr, d), lambda i: (i, 0))
    vec_b = pl.BlockSpec((1, 1, d), lambda i: (i // per_b, 0, 0))
    vec = pl.BlockSpec((1, d), lambda i: (0, 0))
    return pl.pallas_call(
        _res_ln_kernel,
        grid=(t // tr,),
        in_specs=[row, row, vec_b, vec, vec, vec_b, vec_b],
        out_specs=[row, row],
        out_shape=[jax.ShapeDtypeStruct((t, d), jnp.float32), jax.ShapeDtypeStruct((t, d), jnp.bfloat16)],
        compiler_params=_params("parallel"),
        name="residual_layer_norm",
    )(x, y, gate, ln_g.reshape(1, d), ln_b.reshape(1, d), sc_next, sh_next)


def _rope_kernel(pos_ref, inv_ref, sign_ref, q_ref, k1_ref, k2_ref, qo_ref, k1o_ref, k2o_ref, *, q_scale):
    ang = pos_ref[...].astype(jnp.float32) * inv_ref[...]
    cos = jnp.cos(ang)
    sin = jnp.sin(ang) * sign_ref[...]

    def rot(x_ref, o_ref, scale):
        for h in range(x_ref.shape[1] // HEAD_DIM):
            x = x_ref[:, h * HEAD_DIM:(h + 1) * HEAD_DIM].astype(jnp.float32)
            y = x * cos + pltpu.roll(x, HEAD_DIM // 2, 1) * sin
            o_ref[:, h * HEAD_DIM:(h + 1) * HEAD_DIM] = (y * scale).astype(o_ref.dtype)

    rot(q_ref, qo_ref, q_scale)
    rot(k1_ref, k1o_ref, 1.0)
    rot(k2_ref, k2o_ref, 1.0)


def rope_qkk(proj, pos_col, q_cols, k1_cols, k2_cols, tr=512):
    t = proj.shape[0]
    half = HEAD_DIM // 2
    inv = ROPE_THETA ** (-jnp.arange(half, dtype=jnp.float32) / half)
    inv = jnp.concatenate([inv, inv]).reshape(1, HEAD_DIM)
    sign = jnp.concatenate([-jnp.ones((half,), jnp.float32), jnp.ones((half,), jnp.float32)]).reshape(1, HEAD_DIM)

    def cols(c):
        start, width = c
        assert start % width == 0
        return pl.BlockSpec((tr, width), lambda i: (i, start // width))

    def out(c):
        return pl.BlockSpec((tr, c[1]), lambda i: (i, 0))

    vec = pl.BlockSpec((1, HEAD_DIM), lambda i: (0, 0))
    return pl.pallas_call(
        functools.partial(_rope_kernel, q_scale=HEAD_DIM ** -0.5),
        grid=(t // tr,),
        in_specs=[pl.BlockSpec((tr, 1), lambda i: (i, 0)), vec, vec, cols(q_cols), cols(k1_cols), cols(k2_cols)],
        out_specs=[out(q_cols), out(k1_cols), out(k2_cols)],
        out_shape=[jax.ShapeDtypeStruct((t, c[1]), jnp.bfloat16) for c in (q_cols, k1_cols, k2_cols)],
        compiler_params=_params("parallel"),
        name="rope_qkk",
    )(pos_col, inv, sign, proj, proj, proj)


MASK_VALUE = -1e30
ATT_TILE = 512


def _flash_kernel(*refs, mode, heads, shared_kv, tile, n_win_tiles, q_scale):
    if mode == "fox":
        q_ref, k_ref, v_ref, f_ref, o_ref, m_sc, l_sc, acc_sc = refs
    elif mode == "sel":
        q_ref, k_ref, v_ref, sel_ref, o_ref, m_sc, l_sc, acc_sc = refs
    else:
        q_ref, k_ref, v_ref, o_ref, m_sc, l_sc, acc_sc = refs
    qi = pl.program_id(2)
    kj = pl.program_id(3)
    if mode == "win":
        kt = qi - (n_win_tiles - 1) + kj
        active = kt >= 0
        last = kj == n_win_tiles - 1
    else:
        kt = kj
        active = kj <= qi
        last = kj == qi

    @pl.when(kj == 0)
    def _():
        m_sc[...] = jnp.full_like(m_sc, MASK_VALUE)
        l_sc[...] = jnp.zeros_like(l_sc)
        acc_sc[...] = jnp.zeros_like(acc_sc)

    def step(masked):
        mask = None
        if masked:
            row = qi * tile + lax.broadcasted_iota(jnp.int32, (tile, tile), 0)
            col = kt * tile + lax.broadcasted_iota(jnp.int32, (tile, tile), 1)
            mask = col <= row
            if mode == "win":
                mask = mask & (row - col < WINDOW)
            if mode == "sel":
                n_blk = sel_ref.shape[-1]
                blk = lax.broadcasted_iota(jnp.int32, (n_blk, tile), 0)
                key_blk = (kt * tile + lax.broadcasted_iota(jnp.int32, (n_blk, tile), 1)) // SLC_BLOCK
                expand = jnp.where(blk == key_blk, 1.0, 0.0).astype(jnp.bfloat16)
                member = jnp.dot(sel_ref[0, 0], expand, preferred_element_type=jnp.float32)
                mask = mask & (member > 0.5)
        for r in range(heads):
            q = q_ref[:, r * HEAD_DIM:(r + 1) * HEAD_DIM]
            if q_scale != 1.0:
                q = (q.astype(jnp.float32) * q_scale).astype(q.dtype)
            kv = 0 if shared_kv else r
            k = k_ref[:, kv * HEAD_DIM:(kv + 1) * HEAD_DIM]
            v = v_ref[:, kv * HEAD_DIM:(kv + 1) * HEAD_DIM]
            s = lax.dot_general(q, k, (((1,), (1,)), ((), ())), preferred_element_type=jnp.float32)
            if mode == "fox":
                s = s - f_ref[0, r]
            if masked:
                s = jnp.where(mask, s, MASK_VALUE)
            m_prev = m_sc[r]
            m_new = jnp.maximum(m_prev, jnp.max(s, axis=1, keepdims=True))
            alpha = jnp.exp(m_prev - m_new)
            p = jnp.exp(s - m_new[:, :1])
            if masked:
                p = jnp.where(mask, p, 0.0)
            l_sc[r] = alpha * l_sc[r] + jnp.sum(p, axis=1, keepdims=True)
            acc_sc[r] = alpha * acc_sc[r] + jnp.dot(p.astype(v.dtype), v, preferred_element_type=jnp.float32)
            m_sc[r] = m_new

    if mode == "fox":
        @pl.when(kj < qi)
        def _():
            step(False)

        @pl.when(kj == qi)
        def _():
            step(True)
    else:
        @pl.when(active)
        def _():
            step(True)

    @pl.when(last)
    def _():
        for r in range(heads):
            inv_l = 1.0 / l_sc[r]
            o_ref[:, r * HEAD_DIM:(r + 1) * HEAD_DIM] = (acc_sc[r] * inv_l).astype(o_ref.dtype)


def flash_attention(mode, q_arr, q_col0, k_arr, k_col0, v_arr, v_col0, batch, seq, n_groups, heads,
                    shared_kv, extra=None, q_scale=1.0, tile=ATT_TILE):
    t = batch * seq
    nq = seq // tile
    qw = heads * HEAD_DIM
    kvw = HEAD_DIM if shared_kv else qw
    assert q_col0 % qw == 0 and k_col0 % kvw == 0 and v_col0 % kvw == 0 and seq % tile == 0
    n_win_tiles = WINDOW // tile + 1
    nk = n_win_tiles if mode == "win" else nq

    def k_tile(qi, kj):
        if mode == "win":
            return jnp.maximum(qi - (n_win_tiles - 1) + kj, 0)
        return jnp.minimum(kj, qi)

    in_specs = [
        pl.BlockSpec((tile, qw), lambda b, g, qi, kj: (b * nq + qi, q_col0 // qw + g)),
        pl.BlockSpec((tile, kvw), lambda b, g, qi, kj: (b * nq + k_tile(qi, kj), k_col0 // kvw + g)),
        pl.BlockSpec((tile, kvw), lambda b, g, qi, kj: (b * nq + k_tile(qi, kj), v_col0 // kvw + g)),
    ]
    args = [q_arr, k_arr, v_arr]
    if mode == "sel":
        in_specs.append(pl.BlockSpec((1, 1, tile, extra.shape[-1]), lambda b, g, qi, kj: (b, g, qi, 0)))
        args.append(extra)
    elif mode == "fox":
        in_specs.append(pl.BlockSpec((1, heads, 1, tile), lambda b, g, qi, kj: (b, g, 0, k_tile(qi, kj))))
        args.append(extra)
    stat = pltpu.VMEM((heads, tile, HEAD_DIM), jnp.float32)
    return pl.pallas_call(
        functools.partial(_flash_kernel, mode=mode, heads=heads, shared_kv=shared_kv, tile=tile,
                          n_win_tiles=n_win_tiles, q_scale=q_scale),
        grid=(batch, n_groups, nq, nk),
        in_specs=in_specs,
        out_specs=pl.BlockSpec((tile, qw), lambda b, g, qi, kj: (b * nq + qi, g)),
        out_shape=jax.ShapeDtypeStruct((t, n_groups * qw), jnp.bfloat16),
        scratch_shapes=[stat, stat, stat],
        compiler_params=_params("parallel", "parallel", "parallel", "arbitrary"),
        name="flash_" + mode,
    )(*args)


def _split(a, sizes):
    return jnp.split(a, np.cumsum(sizes)[:-1].tolist(), axis=-1)


def _layer_norm(x, g, b):
    mu = jnp.mean(x, axis=-1, keepdims=True)
    var = jnp.mean(jnp.square(x - mu), axis=-1, keepdims=True)
    return (x - mu) * lax.rsqrt(var + LN_EPS) * g + b


def _masked_softmax(s, mask):
    s = jnp.where(mask, s.astype(jnp.float32), -jnp.inf)
    m = jnp.max(s, axis=-1, keepdims=True)
    m = jnp.where(jnp.isfinite(m), m, 0.0)
    e = jnp.exp(s - m)
    d = jnp.sum(e, axis=-1, keepdims=True)
    return e / jnp.where(d > 0, d, 1.0)


def _rope(x, pos):
    half = x.shape[-1] // 2
    inv = ROPE_THETA ** (-jnp.arange(half, dtype=jnp.float32) / half)
    ang = pos.astype(jnp.float32)[..., None] * inv
    cos = jnp.cos(ang)[..., None, :]
    sin = jnp.sin(ang)[..., None, :]
    x1, x2 = x[..., :half], x[..., half:]
    return jnp.concatenate([x1 * cos - x2 * sin, x2 * cos + x1 * sin], axis=-1)


def _causal_dwconv(x, w, b):
    k, ch = w.shape
    y = lax.conv_general_dilated(x, w[:, None, :], window_strides=(1,), padding=[(k - 1, 0)],
                                 dimension_numbers=('NWC', 'WIO', 'NWC'), feature_group_count=ch)
    return y + b


def _nsa_compress(k_raw, blk, pos_emb, w1, w2):
    B, _, G, _ = k_raw.shape
    n_cmp = blk.shape[0]
    blocks = k_raw[:, blk] + pos_emb[None, None, :, None, :]
    blocks = blocks.transpose(0, 1, 3, 2, 4).reshape(B, n_cmp, G, CMP_BLOCK * HEAD_DIM)
    return jax.nn.gelu(blocks @ w1) @ w2


def _nsa_compressed_attention(q, k_cmp, v_cmp, cmp_end):
    S = q.shape[1]
    s = jnp.einsum('bsgrd,bngd->bgrsn', q, k_cmp)
    mask = cmp_end[None, :] <= np.arange(S)[:, None]
    p = _masked_softmax(s, mask)
    o = jnp.einsum('bgrsn,bngd->bsgrd', p, v_cmp)
    return o, p


def _nsa_select_blocks(p_cmp, cmp_start):
    S = p_cmp.shape[3]
    n_slc = S // SLC_BLOCK
    n_sel = min(N_SELECT, n_slc)
    slc_start = np.arange(n_slc) * SLC_BLOCK
    overlap = ((cmp_start[:, None] < slc_start[None, :] + SLC_BLOCK)
               & (cmp_start[:, None] + CMP_BLOCK > slc_start[None, :])).astype(np.float32)
    imp = jnp.einsum('bgrsn,nj->bgsj', p_cmp, jnp.asarray(overlap))
    qblk = np.arange(S)[:, None] // SLC_BLOCK
    j = np.arange(n_slc)[None, :]
    forced = (j == 0) | (j == qblk) | (j == qblk - 1)
    future = j > qblk
    imp = jnp.where(forced, jnp.inf, jnp.where(future, -jnp.inf, imp))
    return lax.top_k(imp, n_sel)[1]


def _ssd_chunked(x, dt, a, bm, cm):
    B, S, G, E, P = x.shape
    N = bm.shape[-1]
    nc, l = S // SSM_CHUNK, SSM_CHUNK
    xd = (x * dt[..., None]).reshape(B, nc, l, G, E, P)
    ad = (dt * a).reshape(B, nc, l, G, E).transpose(0, 3, 4, 1, 2)
    bc = bm.reshape(B, nc, l, G, N)
    cc = cm.reshape(B, nc, l, G, N)
    a_cum = jnp.cumsum(ad, axis=-1)
    tril = np.tril(np.ones((l, l), dtype=bool))
    seg = jnp.where(tril, a_cum[..., :, None] - a_cum[..., None, :], -jnp.inf)
    L = jnp.exp(seg)
    y_diag = jnp.einsum('bclgn,bcsgn,bgecls,bcsgep->bclgep', cc, bc, L, xd)
    decay_states = jnp.exp(a_cum[..., -1:] - a_cum)
    states = jnp.einsum('bclgn,bgecl,bclgep->bcgepn', bc, decay_states, xd)
    chunk_decay = jnp.exp(a_cum[..., -1])

    def step(h, inp):
        st, dec = inp
        return dec[..., None, None] * h + st, h

    _, prev = lax.scan(step, jnp.zeros((B, G, E, P, N), jnp.float32),
                       (states.transpose(1, 0, 2, 3, 4, 5), chunk_decay.transpose(3, 0, 1, 2)))
    prev = prev.transpose(1, 0, 2, 3, 4, 5)
    y_off = jnp.einsum('bclgn,bcgepn,bgecl->bclgep', cc, prev, jnp.exp(a_cum))
    return (y_diag + y_off).reshape(B, S, G, E, P)


def _gated_rmsnorm(y, z, w):
    B, S, _ = y.shape
    yz = (y * jax.nn.silu(z)).reshape(B, S, SSM_GROUPS, -1)
    yz = yz * lax.rsqrt(jnp.mean(jnp.square(yz), axis=-1, keepdims=True) + LN_EPS)
    return yz.reshape(B, S, SSM_DIM) * w


def _moe_gates(t, router):
    probs = jax.nn.softmax(t.astype(jnp.float32) @ router, axis=-1)
    top_p, top_i = lax.top_k(probs, TOP_K)
    top_p = top_p / jnp.sum(top_p, axis=-1, keepdims=True)
    return jnp.sum(jax.nn.one_hot(top_i, N_EXPERTS, dtype=jnp.float32) * top_p[..., None], axis=1)


TM = 512
FF_PAD = _round_up(D_FF, 1024)
FOX_HEADS_PER_STEP = 2


def _bf16(w):
    return w.astype(jnp.bfloat16)


def _col_offsets(sizes):
    return np.concatenate([[0], np.cumsum(sizes)]).tolist()


def _nsa_ssd_heads(proj_main, proj_small, positions, cmp_pos, cmp_w1, cmp_w2, conv_w, conv_b, dt_bias, a_log,
                   d_skip, norm_w):
    B, S = positions.shape
    T = B * S
    G, R = NSA_KV_HEADS, NSA_REP
    q_rot, ksl_rot, kw_rot = rope_qkk(proj_main, positions.reshape(T, 1), (0, NSA_Q_DIM),
                                      (NSA_Q_DIM + 2 * NSA_KV_DIM, NSA_KV_DIM), (NSA_Q_DIM + 4 * NSA_KV_DIM, NSA_KV_DIM))
    f32 = lambda a: a.astype(jnp.float32)
    kc = f32(proj_main[:, NSA_Q_DIM:NSA_Q_DIM + NSA_KV_DIM]).reshape(B, S, G, HEAD_DIM)
    vc = f32(proj_main[:, NSA_Q_DIM + NSA_KV_DIM:NSA_Q_DIM + 2 * NSA_KV_DIM]).reshape(B, S, G, HEAD_DIM)
    z0 = NSA_Q_DIM + 6 * NSA_KV_DIM
    z = f32(proj_main[:, z0:z0 + SSM_DIM]).reshape(B, S, SSM_DIM)
    xbc = f32(proj_main[:, z0 + SSM_DIM:]).reshape(B, S, SSM_CONV_DIM)
    gates = proj_small[:, :3 * NSA_HEADS]
    dt = proj_small[:, 3 * NSA_HEADS:3 * NSA_HEADS + SSM_HEADS].reshape(B, S, SSM_HEADS)
    q = f32(q_rot).reshape(B, S, G, R, HEAD_DIM)
    n_cmp = (S - CMP_BLOCK) // CMP_STRIDE + 1
    blk = np.arange(n_cmp)[:, None] * CMP_STRIDE + np.arange(CMP_BLOCK)[None, :]
    k_cmp = _rope(_nsa_compress(kc, blk, cmp_pos[0], cmp_w1[0], cmp_w2[0]), positions[:, blk[:, -1]])
    v_cmp = _nsa_compress(vc, blk, cmp_pos[1], cmp_w1[1], cmp_w2[1])
    o_cmp, p_cmp = _nsa_compressed_attention(q, k_cmp, v_cmp, blk[:, -1])
    sel_idx = _nsa_select_blocks(p_cmp, blk[:, 0])
    member = jnp.sum(jax.nn.one_hot(sel_idx, S // SLC_BLOCK, dtype=jnp.bfloat16), axis=-2)
    vsl0 = NSA_Q_DIM + 3 * NSA_KV_DIM
    vw0 = NSA_Q_DIM + 5 * NSA_KV_DIM
    o_slc = flash_attention("sel", q_rot, 0, ksl_rot, 0, proj_main, vsl0, B, S, G, R, True, extra=member)
    o_win = flash_attention("win", q_rot, 0, kw_rot, 0, proj_main, vw0, B, S, G, R, True)
    g = jax.nn.sigmoid(gates).reshape(T, NSA_HEADS, 3)
    heads = lambda a: f32(a).reshape(T, NSA_HEADS, HEAD_DIM)
    o_nsa = (g[..., 0:1] * heads(o_cmp.reshape(T, NSA_Q_DIM)) + g[..., 1:2] * heads(o_slc)
             + g[..., 2:3] * heads(o_win)).reshape(B, S, NSA_Q_DIM)
    xbc = jax.nn.silu(_causal_dwconv(xbc, conv_w, conv_b))
    xs, bm, cm = _split(xbc, (SSM_DIM, SSM_GROUPS * SSM_STATE, SSM_GROUPS * SSM_STATE))
    E = SSM_HEADS // SSM_GROUPS
    dt = jax.nn.softplus(dt + dt_bias)
    a = -jnp.exp(a_log)
    y = _ssd_chunked(xs.reshape(B, S, SSM_GROUPS, E, SSM_HEADDIM), dt.reshape(B, S, SSM_GROUPS, E),
                     a.reshape(SSM_GROUPS, E), bm.reshape(B, S, SSM_GROUPS, SSM_STATE),
                     cm.reshape(B, S, SSM_GROUPS, SSM_STATE))
    y = y.reshape(B, S, SSM_HEADS, SSM_HEADDIM) + d_skip[:, None] * xs.reshape(B, S, SSM_HEADS, SSM_HEADDIM)
    o_ssm = _gated_rmsnorm(y.reshape(B, S, SSM_DIM), z, norm_w)
    return jnp.concatenate([o_nsa, o_ssm], axis=-1)


def _fox_conformer_heads(proj_main, proj_small, B, S, f_bias, conv_w, conv_b, conf_ln_g, conf_ln_b):
    logf = jax.nn.log_sigmoid(proj_small[:, :FOX_HEADS].reshape(B, S, FOX_HEADS) + f_bias)
    F = jnp.cumsum(logf, axis=1).transpose(0, 2, 1).reshape(B, FOX_HEADS, 1, S)
    o_fox = flash_attention("fox", proj_main, 0, proj_main, FOX_DIM, proj_main, 2 * FOX_DIM, B, S,
                            FOX_HEADS // FOX_HEADS_PER_STEP, FOX_HEADS_PER_STEP, False, extra=F,
                            q_scale=HEAD_DIM ** -0.5)
    glu = proj_main[:, 3 * FOX_DIM:].astype(jnp.float32).reshape(B, S, 2 * CONF_DIM)
    u_val, u_gate = jnp.split(glu, 2, axis=-1)
    u = _causal_dwconv(u_val * jax.nn.sigmoid(u_gate), conv_w, conv_b)
    u = jax.nn.silu(_layer_norm(u, conf_ln_g, conf_ln_b))
    return jnp.concatenate([o_fox.astype(jnp.float32).reshape(B, S, FOX_DIM), u], axis=-1)


def kernel(x, c, positions, w_ada, b_ada, e_ada_table, e_ln1_g, e_ln1_b, e_ln2_g, e_ln2_b, e_w_in, e_cmp_pos, e_cmp_w1, e_cmp_w2, e_conv_w, e_conv_b, e_dt_bias, e_a_log, e_d_skip, e_ssm_norm_w, e_w_out, e_ffn_w_gate, e_ffn_w_up, e_ffn_w_down, o_ada_table, o_ln1_g, o_ln1_b, o_ln2_g, o_ln2_b, o_w_in, o_fox_f_bias, o_conf_conv_w, o_conf_conv_b, o_conf_ln_g, o_conf_ln_b, o_w_out, o_moe_router, o_moe_w_gate, o_moe_w_up, o_moe_w_down):
    B, S, D = x.shape
    T = B * S
    mod = ada_projection(c, w_ada, b_ada).reshape(B, 6, D)

    def mods(table):
        m = mod + table[None]
        return [m[:, k, None, :] for k in range(6)]

    layer_mods = [mods(e_ada_table[i // 2]) if i % 2 == 0 else mods(o_ada_table[i // 2]) for i in range(DEPTH)]
    eo = _col_offsets(E_SPLIT_SIZES)
    oo = _col_offsets(O_SPLIT_SIZES)
    xf = x.reshape(T, D)
    h = modulate(xf, layer_mods[0][1], layer_mods[0][0], S)
    for i in range(DEPTH):
        j = i // 2
        sh1, sc1, g1, sh2, sc2, g2 = layer_mods[i]
        if i % 2 == 0:
            w = e_w_in[j]
            w_main = _bf16(jnp.concatenate([w[:, :eo[7]], w[:, eo[8]:eo[10]]], axis=1))
            w_small = jnp.concatenate([w[:, eo[7]:eo[8]], w[:, eo[10]:eo[11]]], axis=1)
            w_small = _bf16(jnp.pad(w_small, ((0, 0), (0, LANES - w_small.shape[1]))))
            proj_main = matmul(h, w_main, jnp.bfloat16, TM, 1024)
            proj_small = matmul(h, w_small, jnp.float32, TM, LANES)
            mix = _nsa_ssd_heads(proj_main, proj_small, positions, e_cmp_pos[j], e_cmp_w1[j], e_cmp_w2[j],
                                 e_conv_w[j], e_conv_b[j], e_dt_bias[j], e_a_log[j], e_d_skip[j], e_ssm_norm_w[j])
            y = matmul(_bf16(mix.reshape(T, MIX_DIM)), _bf16(e_w_out[j]), jnp.float32, TM, 1024)
            xf, h = residual_layer_norm(xf, y, g1, e_ln1_g[j], e_ln1_b[j], sc2, sh2, S)
            pad_ff = ((0, 0), (0, FF_PAD - D_FF))
            hid = swiglu_up(h, _bf16(jnp.pad(e_ffn_w_gate[j], pad_ff)), _bf16(jnp.pad(e_ffn_w_up[j], pad_ff)), TM, 1024)
            w_down = _bf16(jnp.pad(e_ffn_w_down[j], ((0, FF_PAD - D_FF), (0, 0))))
            y = matmul(hid, w_down, jnp.float32, TM, 1024, tk=FF_PAD // 4)
            ln_g, ln_b = e_ln2_g[j], e_ln2_b[j]
        else:
            w = o_w_in[j]
            w_main = _bf16(jnp.concatenate([w[:, :oo[3]], w[:, oo[4]:oo[5]]], axis=1))
            w_small = _bf16(jnp.pad(w[:, oo[3]:oo[4]], ((0, 0), (0, LANES - FOX_HEADS))))
            proj_main = matmul(h, w_main, jnp.bfloat16, TM, 1024)
            proj_small = matmul(h, w_small, jnp.float32, TM, LANES)
            mix = _fox_conformer_heads(proj_main, proj_small, B, S, o_fox_f_bias[j], o_conf_conv_w[j],
                                       o_conf_conv_b[j], o_conf_ln_g[j], o_conf_ln_b[j])
            y = matmul(_bf16(mix.reshape(T, MIX_DIM)), _bf16(o_w_out[j]), jnp.float32, TM, 1024)
            xf, h = residual_layer_norm(xf, y, g1, o_ln1_g[j], o_ln1_b[j], sc2, sh2, S)
            gates = _moe_gates(h, o_moe_router[j])
            y = jnp.zeros((T, D), jnp.float32)
            for e in range(N_EXPERTS):
                hid = swiglu_up(h, _bf16(o_moe_w_gate[j, e]), _bf16(o_moe_w_up[j, e]), TM, D_EXPERT)
                hid = (hid.astype(jnp.float32) * gates[:, e:e + 1]).astype(jnp.bfloat16)
                y = y + matmul(hid, _bf16(o_moe_w_down[j, e]), jnp.float32, TM, 1024)
            ln_g, ln_b = o_ln2_g[j], o_ln2_b[j]
        if i + 1 < DEPTH:
            nsh, nsc = layer_mods[i + 1][0], layer_mods[i + 1][1]
        else:
            nsh, nsc = sh2, sc2
        xf, h = residual_layer_norm(xf, y, g2, ln_g, ln_b, nsc, nsh, S)
    return xf.reshape(B, S, D)
```

```python
import functools

import jax
import jax.numpy as jnp
import numpy as np
from jax import lax
from jax.experimental import pallas as pl
from jax.experimental.pallas import tpu as pltpu

D_MODEL = 4096
DEPTH = 4
HEAD_DIM = 128
MIX_DIM = D_MODEL

NSA_HEADS = 16
NSA_KV_HEADS = 4
NSA_REP = NSA_HEADS // NSA_KV_HEADS
NSA_Q_DIM = NSA_HEADS * HEAD_DIM
NSA_KV_DIM = NSA_KV_HEADS * HEAD_DIM
CMP_BLOCK = 32
CMP_STRIDE = 16
SLC_BLOCK = 64
N_SELECT = 16
WINDOW = 512
ROPE_THETA = 10000.0

SSM_DIM = MIX_DIM - NSA_Q_DIM
SSM_HEADDIM = 64
SSM_HEADS = SSM_DIM // SSM_HEADDIM
SSM_GROUPS = 8
SSM_STATE = 128
SSM_CONV = 4
SSM_CHUNK = 128
SSM_CONV_DIM = SSM_DIM + 2 * SSM_GROUPS * SSM_STATE

FOX_HEADS = 16
FOX_DIM = FOX_HEADS * HEAD_DIM

CONF_DIM = MIX_DIM - FOX_DIM
CONF_KERNEL = 31

D_FF = 11008
N_EXPERTS = 8
TOP_K = 2
D_EXPERT = 1408

DEEPNORM_ALPHA = (2.0 * DEPTH) ** 0.25
LN_EPS = 1e-5

E_SPLIT_SIZES = (NSA_Q_DIM,) + (NSA_KV_DIM,) * 6 + (3 * NSA_HEADS, SSM_DIM, SSM_CONV_DIM, SSM_HEADS)
O_SPLIT_SIZES = (FOX_DIM, FOX_DIM, FOX_DIM, FOX_HEADS, 2 * CONF_DIM)

LANES = 128
VMEM_LIMIT_BYTES = 56 * 1024 * 1024


def _round_up(n, m):
    return (n + m - 1) // m * m


def _params(*sem):
    return pltpu.CompilerParams(dimension_semantics=sem, vmem_limit_bytes=VMEM_LIMIT_BYTES)


def _cast_cols_kernel(w_ref, o_ref, *, segments, fill):
    if fill:
        o_ref[...] = jnp.zeros_like(o_ref)
    for src, width, dst in segments:
        o_ref[:, dst:dst + width] = w_ref[:, src:src + width].astype(o_ref.dtype)


def cast_cols(w, out_cols, segments, tr=256):
    rows, cols = w.shape
    assert rows % tr == 0
    fill = sum(width for _, width, _ in segments) != out_cols
    return pl.pallas_call(
        functools.partial(_cast_cols_kernel, segments=tuple(segments), fill=fill),
        grid=(rows // tr,),
        in_specs=[pl.BlockSpec((tr, cols), lambda i: (i, 0))],
        out_specs=pl.BlockSpec((tr, out_cols), lambda i: (i, 0)),
        out_shape=jax.ShapeDtypeStruct((rows, out_cols), jnp.bfloat16),
        compiler_params=_params("parallel"),
        name="cast_cols",
    )(w)


def cast_bf16(w, tr=256):
    w2 = w.reshape(-1, w.shape[-1])
    return cast_cols(w2, w2.shape[1], [(0, w2.shape[1], 0)], tr)


def _matmul_kernel(x_ref, w_ref, o_ref):
    o_ref[...] = jnp.dot(x_ref[...], w_ref[...], preferred_element_type=jnp.float32).astype(o_ref.dtype)


def _matmul_acc_kernel(x_ref, w_ref, o_ref, acc_ref):
    k = pl.program_id(2)

    @pl.when(k == 0)
    def _():
        acc_ref[...] = jnp.zeros_like(acc_ref)

    acc_ref[...] += jnp.dot(x_ref[...], w_ref[...], preferred_element_type=jnp.float32)

    @pl.when(k == pl.num_programs(2) - 1)
    def _():
        o_ref[...] = acc_ref[...].astype(o_ref.dtype)


def matmul(x, w, out_dtype, tm, tn, tk=None, n=None, w_row0=0, w_col0=0):
    m, k = x.shape
    n = w.shape[1] if n is None else n
    tk = k if tk is None else tk
    assert m % tm == 0 and n % tn == 0 and k % tk == 0 and w_row0 % tk == 0 and w_col0 % tn == 0
    r0, c0 = w_row0 // tk, w_col0 // tn
    if tk == k:
        return pl.pallas_call(
            _matmul_kernel,
            grid=(n // tn, m // tm),
            in_specs=[pl.BlockSpec((tm, k), lambda j, i: (i, 0)),
                      pl.BlockSpec((k, tn), lambda j, i: (r0, c0 + j))],
            out_specs=pl.BlockSpec((tm, tn), lambda j, i: (i, j)),
            out_shape=jax.ShapeDtypeStruct((m, n), out_dtype),
            compiler_params=_params("parallel", "parallel"),
            name="matmul",
        )(x, w)
    return pl.pallas_call(
        _matmul_acc_kernel,
        grid=(n // tn, m // tm, k // tk),
        in_specs=[pl.BlockSpec((tm, tk), lambda j, i, kk: (i, kk)),
                  pl.BlockSpec((tk, tn), lambda j, i, kk: (r0 + kk, c0 + j))],
        out_specs=pl.BlockSpec((tm, tn), lambda j, i, kk: (i, j)),
        out_shape=jax.ShapeDtypeStruct((m, n), out_dtype),
        scratch_shapes=[pltpu.VMEM((tm, tn), jnp.float32)],
        compiler_params=_params("parallel", "parallel", "arbitrary"),
        name="matmul_acc",
    )(x, w)


def _swiglu_up_kernel(*refs, gated):
    if gated:
        x_ref, wg_ref, wu_ref, gate_ref, o_ref = refs
    else:
        x_ref, wg_ref, wu_ref, o_ref = refs
    x = x_ref[...]
    g = jnp.dot(x, wg_ref[...], preferred_element_type=jnp.float32)
    u = jnp.dot(x, wu_ref[...], preferred_element_type=jnp.float32)
    hid = g * jax.nn.sigmoid(g) * u
    if gated:
        hid = hid * gate_ref[:, :1]
    o_ref[...] = hid.astype(o_ref.dtype)


def swiglu_up(x, wg, wu, tm, tn, w_row0=0):
    m, k = x.shape
    n = wg.shape[1]
    assert m % tm == 0 and n % tn == 0 and w_row0 % k == 0
    r0 = w_row0 // k
    return pl.pallas_call(
        functools.partial(_swiglu_up_kernel, gated=False),
        grid=(m // tm, n // tn),
        in_specs=[pl.BlockSpec((tm, k), lambda i, j: (i, 0)),
                  pl.BlockSpec((k, tn), lambda i, j: (r0, j)),
                  pl.BlockSpec((k, tn), lambda i, j: (r0, j))],
        out_specs=pl.BlockSpec((tm, tn), lambda i, j: (i, j)),
        out_shape=jax.ShapeDtypeStruct((m, n), jnp.bfloat16),
        compiler_params=_params("parallel", "parallel"),
        name="swiglu_up",
    )(x, wg, wu)


def expert_swiglu_up(x, wg, wu, gates_wide, n_experts, expert0, tm):
    m, k = x.shape
    d_e = wg.shape[1]
    return pl.pallas_call(
        functools.partial(_swiglu_up_kernel, gated=True),
        grid=(n_experts, m // tm),
        in_specs=[pl.BlockSpec((tm, k), lambda e, i: (i, 0)),
                  pl.BlockSpec((k, d_e), lambda e, i: (expert0 + e, 0), pipeline_mode=pl.Buffered(1)),
                  pl.BlockSpec((k, d_e), lambda e, i: (expert0 + e, 0), pipeline_mode=pl.Buffered(1)),
                  pl.BlockSpec((tm, LANES), lambda e, i: (i, e))],
        out_specs=pl.BlockSpec((tm, d_e), lambda e, i: (i, e)),
        out_shape=jax.ShapeDtypeStruct((m, n_experts * d_e), jnp.bfloat16),
        compiler_params=_params("parallel", "parallel"),
        name="expert_swiglu_up",
    )(x, wg, wu, gates_wide)


def _router_kernel(x_ref, w_ref, o_ref, *, n_experts):
    logits = jnp.dot(x_ref[...], w_ref[...], preferred_element_type=jnp.float32)
    lane = lax.broadcasted_iota(jnp.int32, logits.shape, 1)
    logits = jnp.where(lane < n_experts, logits, -jnp.inf)
    e = jnp.exp(logits - jnp.max(logits, axis=1, keepdims=True))
    p = e / jnp.sum(e, axis=1, keepdims=True)
    big = jnp.int32(LANES)
    v1 = jnp.max(p, axis=1, keepdims=True)
    i1 = jnp.min(jnp.where(p == v1, lane, big), axis=1, keepdims=True)
    p2 = jnp.where(lane == i1, -1.0, p)
    v2 = jnp.max(p2, axis=1, keepdims=True)
    i2 = jnp.min(jnp.where(p2 == v2, lane, big), axis=1, keepdims=True)
    denom = v1 + v2
    for ex in range(n_experts):
        g = jnp.where(i1 == ex, v1 / denom, jnp.where(i2 == ex, v2 / denom, 0.0))
        o_ref[:, ex * LANES:(ex + 1) * LANES] = jnp.broadcast_to(g, (g.shape[0], LANES))


def moe_router(x, router_padded, n_experts, tm=512):
    m, k = x.shape
    return pl.pallas_call(
        functools.partial(_router_kernel, n_experts=n_experts),
        grid=(m // tm,),
        in_specs=[pl.BlockSpec((tm, k), lambda i: (i, 0)),
                  pl.BlockSpec((k, LANES), lambda i: (0, 0))],
        out_specs=pl.BlockSpec((tm, n_experts * LANES), lambda i: (i, 0)),
        out_shape=jax.ShapeDtypeStruct((m, n_experts * LANES), jnp.float32),
        compiler_params=_params("parallel"),
        name="moe_router",
    )(x, router_padded)


def _ada_kernel(c_ref, w_ref, b_ref, o_ref):
    c = c_ref[...]
    a = c * jax.nn.sigmoid(c)
    o_ref[...] = jnp.dot(a, w_ref[...], preferred_element_type=jnp.float32,
                         precision=lax.Precision.HIGHEST) + b_ref[...]


def ada_projection(c, w_ada, b_ada):
    b, d = c.shape
    n = w_ada.shape[1]
    rows = _round_up(b, 8)
    c_pad = jnp.pad(c, ((0, rows - b), (0, 0)))
    tn = 512
    out = pl.pallas_call(
        _ada_kernel,
        grid=(n // tn,),
        in_specs=[pl.BlockSpec((rows, d), lambda j: (0, 0)),
                  pl.BlockSpec((d, tn), lambda j: (0, j)),
                  pl.BlockSpec((1, tn), lambda j: (0, j))],
        out_specs=pl.BlockSpec((rows, tn), lambda j: (0, j)),
        out_shape=jax.ShapeDtypeStruct((rows, n), jnp.float32),
        compiler_params=_params("parallel"),
        name="ada_projection",
    )(c_pad, w_ada, b_ada.reshape(1, n))
    return out[:b]


def _modulate_kernel(x_ref, sc_ref, sh_ref, h_ref):
    h_ref[...] = (x_ref[...] * (1.0 + sc_ref[0]) + sh_ref[0]).astype(h_ref.dtype)


def modulate(x, sc, sh, seq, tr=256):
    t, d = x.shape
    per_b = seq // tr
    vec = pl.BlockSpec((1, 1, d), lambda i: (i // per_b, 0, 0))
    return pl.pallas_call(
        _modulate_kernel,
        grid=(t // tr,),
        in_specs=[pl.BlockSpec((tr, d), lambda i: (i, 0)), vec, vec],
        out_specs=pl.BlockSpec((tr, d), lambda i: (i, 0)),
        out_shape=jax.ShapeDtypeStruct((t, d), jnp.bfloat16),
        compiler_params=_params("parallel"),
        name="modulate",
    )(x, sc, sh)


def _res_ln_kernel(x_ref, y_ref, gate_ref, g_ref, b_ref, sc_ref, sh_ref, xn_ref, h_ref):
    v = DEEPNORM_ALPHA * x_ref[...] + (1.0 + gate_ref[0]) * y_ref[...].astype(jnp.float32)
    mu = jnp.mean(v, axis=-1, keepdims=True)
    vc = v - mu
    var = jnp.mean(vc * vc, axis=-1, keepdims=True)
    xn = vc * lax.rsqrt(var + LN_EPS) * g_ref[...] + b_ref[...]
    xn_ref[...] = xn
    h_ref[...] = (xn * (1.0 + sc_ref[0]) + sh_ref[0]).astype(h_ref.dtype)


def residual_layer_norm(x, y, gate, ln_g, ln_b, sc_next, sh_next, seq, tr=256):
    t, d = x.shape
    per_b = seq // tr
    row = pl.BlockSpec((tr, d), lambda i: (i, 0))
    vec_b = pl.BlockSpec((1, 1, d), lambda i: (i // per_b, 0, 0))
    vec = pl.BlockSpec((1, d), lambda i: (0, 0))
    return pl.pallas_call(
        _res_ln_kernel,
        grid=(t // tr,),
        in_specs=[row, row, vec_b, vec, vec, vec_b, vec_b],
        out_specs=[row, row],
        out_shape=[jax.ShapeDtypeStruct((t, d), jnp.float32), jax.ShapeDtypeStruct((t, d), jnp.bfloat16)],
        compiler_params=_params("parallel"),
        name="residual_layer_norm",
    )(x, y, gate, ln_g.reshape(1, d), ln_b.reshape(1, d), sc_next, sh_next)


def _rope_kernel(pos_ref, inv_ref, sign_ref, q_ref, k1_ref, k2_ref, qo_ref, k1o_ref, k2o_ref, *, q_scale):
    ang = pos_ref[...].astype(jnp.float32) * inv_ref[...]
    cos = jnp.cos(ang)
    sin = jnp.sin(ang) * sign_ref[...]

    def rot(x_ref, o_ref, scale):
        for h in range(x_ref.shape[1] // HEAD_DIM):
            x = x_ref[:, h * HEAD_DIM:(h + 1) * HEAD_DIM].astype(jnp.float32)
            y = x * cos + pltpu.roll(x, HEAD_DIM // 2, 1) * sin
            o_ref[:, h * HEAD_DIM:(h + 1) * HEAD_DIM] = (y * scale).astype(o_ref.dtype)

    rot(q_ref, qo_ref, q_scale)
    rot(k1_ref, k1o_ref, 1.0)
    rot(k2_ref, k2o_ref, 1.0)


def rope_qkk(proj, pos_col, q_cols, k1_cols, k2_cols, tr=512):
    t = proj.shape[0]
    half = HEAD_DIM // 2
    inv = ROPE_THETA ** (-jnp.arange(half, dtype=jnp.float32) / half)
    inv = jnp.concatenate([inv, inv]).reshape(1, HEAD_DIM)
    sign = jnp.concatenate([-jnp.ones((half,), jnp.float32), jnp.ones((half,), jnp.float32)]).reshape(1, HEAD_DIM)

    def cols(c):
        start, width = c
        assert start % width == 0
        return pl.BlockSpec((tr, width), lambda i: (i, start // width))

    def out(c):
        return pl.BlockSpec((tr, c[1]), lambda i: (i, 0))

    vec = pl.BlockSpec((1, HEAD_DIM), lambda i: (0, 0))
    return pl.pallas_call(
        functools.partial(_rope_kernel, q_scale=HEAD_DIM ** -0.5),
        grid=(t // tr,),
        in_specs=[pl.BlockSpec((tr, 1), lambda i: (i, 0)), vec, vec, cols(q_cols), cols(k1_cols), cols(k2_cols)],
        out_specs=[out(q_cols), out(k1_cols), out(k2_cols)],
        out_shape=[jax.ShapeDtypeStruct((t, c[1]), jnp.bfloat16) for c in (q_cols, k1_cols, k2_cols)],
        compiler_params=_params("parallel"),
        name="rope_qkk",
    )(pos_col, inv, sign, proj, proj, proj)


MASK_VALUE = -1e30
ATT_TILE = 512


def _flash_kernel(*refs, mode, heads, shared_kv, tile, n_win_tiles, q_scale):
    if mode == "fox":
        q_ref, k_ref, v_ref, f_ref, o_ref, m_sc, l_sc, acc_sc = refs
    elif mode == "sel":
        q_ref, k_ref, v_ref, sel_ref, o_ref, m_sc, l_sc, acc_sc = refs
    else:
        q_ref, k_ref, v_ref, o_ref, m_sc, l_sc, acc_sc = refs
    qi = pl.program_id(2)
    kj = pl.program_id(3)
    if mode == "win":
        kt = qi - (n_win_tiles - 1) + kj
        active = kt >= 0
        last = kj == n_win_tiles - 1
    else:
        kt = kj
        active = kj <= qi
        last = kj == qi

    @pl.when(kj == 0)
    def _():
        m_sc[...] = jnp.full_like(m_sc, MASK_VALUE)
        l_sc[...] = jnp.zeros_like(l_sc)
        acc_sc[...] = jnp.zeros_like(acc_sc)

    def step(masked):
        mask = None
        if masked:
            row = qi * tile + lax.broadcasted_iota(jnp.int32, (tile, tile), 0)
            col = kt * tile + lax.broadcasted_iota(jnp.int32, (tile, tile), 1)
            mask = col <= row
            if mode == "win":
                mask = mask & (row - col < WINDOW)
            if mode == "sel":
                n_blk = sel_ref.shape[-1]
                blk = lax.broadcasted_iota(jnp.int32, (n_blk, tile), 0)
                key_blk = (kt * tile + lax.broadcasted_iota(jnp.int32, (n_blk, tile), 1)) // SLC_BLOCK
                expand = jnp.where(blk == key_blk, 1.0, 0.0).astype(jnp.bfloat16)
                member = jnp.dot(sel_ref[0, 0], expand, preferred_element_type=jnp.float32)
                mask = mask & (member > 0.5)
        for r in range(heads):
            q = q_ref[:, r * HEAD_DIM:(r + 1) * HEAD_DIM]
            if q_scale != 1.0:
                q = (q.astype(jnp.float32) * q_scale).astype(q.dtype)
            kv = 0 if shared_kv else r
            k = k_ref[:, kv * HEAD_DIM:(kv + 1) * HEAD_DIM]
            v = v_ref[:, kv * HEAD_DIM:(kv + 1) * HEAD_DIM]
            s = lax.dot_general(q, k, (((1,), (1,)), ((), ())), preferred_element_type=jnp.float32)
            if mode == "fox":
                s = s - f_ref[0, r]
            if masked:
                s = jnp.where(mask, s, MASK_VALUE)
            m_prev = m_sc[r]
            m_new = jnp.maximum(m_prev, jnp.max(s, axis=1, keepdims=True))
            alpha = jnp.exp(m_prev - m_new)
            p = jnp.exp(s - m_new[:, :1])
            if masked:
                p = jnp.where(mask, p, 0.0)
            l_sc[r] = alpha * l_sc[r] + jnp.sum(p, axis=1, keepdims=True)
            acc_sc[r] = alpha * acc_sc[r] + jnp.dot(p.astype(v.dtype), v, preferred_element_type=jnp.float32)
            m_sc[r] = m_new

    if mode == "fox":
        @pl.when(kj < qi)
        def _():
            step(False)

        @pl.when(kj == qi)
        def _():
            step(True)
    else:
        @pl.when(active)
        def _():
            step(True)

    @pl.when(last)
    def _():
        for r in range(heads):
            inv_l = 1.0 / l_sc[r]
            o_ref[:, r * HEAD_DIM:(r + 1) * HEAD_DIM] = (acc_sc[r] * inv_l).astype(o_ref.dtype)


def flash_attention(mode, q_arr, q_col0, k_arr, k_col0, v_arr, v_col0, batch, seq, n_groups, heads,
                    shared_kv, extra=None, q_scale=1.0, tile=ATT_TILE):
    t = batch * seq
    nq = seq // tile
    qw = heads * HEAD_DIM
    kvw = HEAD_DIM if shared_kv else qw
    assert q_col0 % qw == 0 and k_col0 % kvw == 0 and v_col0 % kvw == 0 and seq % tile == 0
    n_win_tiles = WINDOW // tile + 1
    nk = n_win_tiles if mode == "win" else nq

    def k_tile(qi, kj):
        if mode == "win":
            return jnp.maximum(qi - (n_win_tiles - 1) + kj, 0)
        return jnp.minimum(kj, qi)

    in_specs = [
        pl.BlockSpec((tile, qw), lambda b, g, qi, kj: (b * nq + qi, q_col0 // qw + g)),
        pl.BlockSpec((tile, kvw), lambda b, g, qi, kj: (b * nq + k_tile(qi, kj), k_col0 // kvw + g)),
        pl.BlockSpec((tile, kvw), lambda b, g, qi, kj: (b * nq + k_tile(qi, kj), v_col0 // kvw + g)),
    ]
    args = [q_arr, k_arr, v_arr]
    if mode == "sel":
        in_specs.append(pl.BlockSpec((1, 1, tile, extra.shape[-1]), lambda b, g, qi, kj: (b, g, qi, 0)))
        args.append(extra)
    elif mode == "fox":
        in_specs.append(pl.BlockSpec((1, heads, 1, tile), lambda b, g, qi, kj: (b, g, 0, k_tile(qi, kj))))
        args.append(extra)
    stat = pltpu.VMEM((heads, tile, HEAD_DIM), jnp.float32)
    return pl.pallas_call(
        functools.partial(_flash_kernel, mode=mode, heads=heads, shared_kv=shared_kv, tile=tile,
                          n_win_tiles=n_win_tiles, q_scale=q_scale),
        grid=(batch, n_groups, nq, nk),
        in_specs=in_specs,
        out_specs=pl.BlockSpec((tile, qw), lambda b, g, qi, kj: (b * nq + qi, g)),
        out_shape=jax.ShapeDtypeStruct((t, n_groups * qw), jnp.bfloat16),
        scratch_shapes=[stat, stat, stat],
        compiler_params=_params("parallel", "parallel", "parallel", "arbitrary"),
        name="flash_" + mode,
    )(*args)


def _split(a, sizes):
    return jnp.split(a, np.cumsum(sizes)[:-1].tolist(), axis=-1)


def _layer_norm(x, g, b):
    mu = jnp.mean(x, axis=-1, keepdims=True)
    var = jnp.mean(jnp.square(x - mu), axis=-1, keepdims=True)
    return (x - mu) * lax.rsqrt(var + LN_EPS) * g + b


def _masked_softmax(s, mask):
    s = jnp.where(mask, s.astype(jnp.float32), -jnp.inf)
    m = jnp.max(s, axis=-1, keepdims=True)
    m = jnp.where(jnp.isfinite(m), m, 0.0)
    e = jnp.exp(s - m)
    d = jnp.sum(e, axis=-1, keepdims=True)
    return e / jnp.where(d > 0, d, 1.0)


def _rope(x, pos):
    half = x.shape[-1] // 2
    inv = ROPE_THETA ** (-jnp.arange(half, dtype=jnp.float32) / half)
    ang = pos.astype(jnp.float32)[..., None] * inv
    cos = jnp.cos(ang)[..., None, :]
    sin = jnp.sin(ang)[..., None, :]
    x1, x2 = x[..., :half], x[..., half:]
    return jnp.concatenate([x1 * cos - x2 * sin, x2 * cos + x1 * sin], axis=-1)


def _causal_dwconv(x, w, b):
    k, ch = w.shape
    y = lax.conv_general_dilated(x, w[:, None, :], window_strides=(1,), padding=[(k - 1, 0)],
                                 dimension_numbers=('NWC', 'WIO', 'NWC'), feature_group_count=ch)
    return y + b


def _nsa_compress(k_raw, blk, pos_emb, w1, w2):
    B, _, G, _ = k_raw.shape
    n_cmp = blk.shape[0]
    blocks = k_raw[:, blk] + pos_emb[None, None, :, None, :]
    blocks = blocks.transpose(0, 1, 3, 2, 4).reshape(B, n_cmp, G, CMP_BLOCK * HEAD_DIM)
    return jax.nn.gelu(blocks @ w1) @ w2


def _nsa_compressed_attention(q, k_cmp, v_cmp, cmp_end):
    S = q.shape[1]
    s = jnp.einsum('bsgrd,bngd->bgrsn', q, k_cmp)
    mask = cmp_end[None, :] <= np.arange(S)[:, None]
    p = _masked_softmax(s, mask)
    o = jnp.einsum('bgrsn,bngd->bsgrd', p, v_cmp)
    return o, p


def _nsa_select_blocks(p_cmp, cmp_start):
    S = p_cmp.shape[3]
    n_slc = S // SLC_BLOCK
    n_sel = min(N_SELECT, n_slc)
    slc_start = np.arange(n_slc) * SLC_BLOCK
    overlap = ((cmp_start[:, None] < slc_start[None, :] + SLC_BLOCK)
               & (cmp_start[:, None] + CMP_BLOCK > slc_start[None, :])).astype(np.float32)
    imp = jnp.einsum('bgrsn,nj->bgsj', p_cmp, jnp.asarray(overlap))
    qblk = np.arange(S)[:, None] // SLC_BLOCK
    j = np.arange(n_slc)[None, :]
    forced = (j == 0) | (j == qblk) | (j == qblk - 1)
    future = j > qblk
    imp = jnp.where(forced, jnp.inf, jnp.where(future, -jnp.inf, imp))
    return lax.top_k(imp, n_sel)[1]


def _ssd_chunked(x, dt, a, bm, cm):
    B, S, G, E, P = x.shape
    N = bm.shape[-1]
    nc, l = S // SSM_CHUNK, SSM_CHUNK
    xd = (x * dt[..., None]).reshape(B, nc, l, G, E, P)
    ad = (dt * a).reshape(B, nc, l, G, E).transpose(0, 3, 4, 1, 2)
    bc = bm.reshape(B, nc, l, G, N)
    cc = cm.reshape(B, nc, l, G, N)
    a_cum = jnp.cumsum(ad, axis=-1)
    tril = np.tril(np.ones((l, l), dtype=bool))
    seg = jnp.where(tril, a_cum[..., :, None] - a_cum[..., None, :], -jnp.inf)
    L = jnp.exp(seg)
    y_diag = jnp.einsum('bclgn,bcsgn,bgecls,bcsgep->bclgep', cc, bc, L, xd)
    decay_states = jnp.exp(a_cum[..., -1:] - a_cum)
    states = jnp.einsum('bclgn,bgecl,bclgep->bcgepn', bc, decay_states, xd)
    chunk_decay = jnp.exp(a_cum[..., -1])

    def step(h, inp):
        st, dec = inp
        return dec[..., None, None] * h + st, h

    _, prev = lax.scan(step, jnp.zeros((B, G, E, P, N), jnp.float32),
                       (states.transpose(1, 0, 2, 3, 4, 5), chunk_decay.transpose(3, 0, 1, 2)))
    prev = prev.transpose(1, 0, 2, 3, 4, 5)
    y_off = jnp.einsum('bclgn,bcgepn,bgecl->bclgep', cc, prev, jnp.exp(a_cum))
    return (y_diag + y_off).reshape(B, S, G, E, P)


def _gated_rmsnorm(y, z, w):
    B, S, _ = y.shape
    yz = (y * jax.nn.silu(z)).reshape(B, S, SSM_GROUPS, -1)
    yz = yz * lax.rsqrt(jnp.mean(jnp.square(yz), axis=-1, keepdims=True) + LN_EPS)
    return yz.reshape(B, S, SSM_DIM) * w


TM = 512
FFN_TN = 256
FOX_HEADS_PER_STEP = 2
E_MAIN = NSA_Q_DIM + 6 * NSA_KV_DIM + SSM_DIM + SSM_CONV_DIM
O_MAIN = 3 * FOX_DIM + 2 * CONF_DIM
assert TOP_K == 2


def _bf16(w):
    return w.astype(jnp.bfloat16)


def _col_offsets(sizes):
    return np.concatenate([[0], np.cumsum(sizes)]).tolist()


def _nsa_ssd_heads(proj_main, proj_small, positions, cmp_pos, cmp_w1, cmp_w2, conv_w, conv_b, dt_bias, a_log,
                   d_skip, norm_w):
    B, S = positions.shape
    T = B * S
    G, R = NSA_KV_HEADS, NSA_REP
    q_rot, ksl_rot, kw_rot = rope_qkk(proj_main, positions.reshape(T, 1), (0, NSA_Q_DIM),
                                      (NSA_Q_DIM + 2 * NSA_KV_DIM, NSA_KV_DIM), (NSA_Q_DIM + 4 * NSA_KV_DIM, NSA_KV_DIM))
    f32 = lambda a: a.astype(jnp.float32)
    kc = f32(proj_main[:, NSA_Q_DIM:NSA_Q_DIM + NSA_KV_DIM]).reshape(B, S, G, HEAD_DIM)
    vc = f32(proj_main[:, NSA_Q_DIM + NSA_KV_DIM:NSA_Q_DIM + 2 * NSA_KV_DIM]).reshape(B, S, G, HEAD_DIM)
    z0 = NSA_Q_DIM + 6 * NSA_KV_DIM
    z = f32(proj_main[:, z0:z0 + SSM_DIM]).reshape(B, S, SSM_DIM)
    xbc = f32(proj_main[:, z0 + SSM_DIM:]).reshape(B, S, SSM_CONV_DIM)
    gates = proj_small[:, :3 * NSA_HEADS]
    dt = proj_small[:, 3 * NSA_HEADS:3 * NSA_HEADS + SSM_HEADS].reshape(B, S, SSM_HEADS)
    q = f32(q_rot).reshape(B, S, G, R, HEAD_DIM)
    n_cmp = (S - CMP_BLOCK) // CMP_STRIDE + 1
    blk = np.arange(n_cmp)[:, None] * CMP_STRIDE + np.arange(CMP_BLOCK)[None, :]
    k_cmp = _rope(_nsa_compress(kc, blk, cmp_pos[0], cmp_w1[0], cmp_w2[0]), positions[:, blk[:, -1]])
    v_cmp = _nsa_compress(vc, blk, cmp_pos[1], cmp_w1[1], cmp_w2[1])
    o_cmp, p_cmp = _nsa_compressed_attention(q, k_cmp, v_cmp, blk[:, -1])
    sel_idx = _nsa_select_blocks(p_cmp, blk[:, 0])
    member = jnp.sum(jax.nn.one_hot(sel_idx, S // SLC_BLOCK, dtype=jnp.bfloat16), axis=-2)
    vsl0 = NSA_Q_DIM + 3 * NSA_KV_DIM
    vw0 = NSA_Q_DIM + 5 * NSA_KV_DIM
    o_slc = flash_attention("sel", q_rot, 0, ksl_rot, 0, proj_main, vsl0, B, S, G, R, True, extra=member)
    o_win = flash_attention("win", q_rot, 0, kw_rot, 0, proj_main, vw0, B, S, G, R, True)
    g = jax.nn.sigmoid(gates).reshape(T, NSA_HEADS, 3)
    heads = lambda a: f32(a).reshape(T, NSA_HEADS, HEAD_DIM)
    o_nsa = (g[..., 0:1] * heads(o_cmp.reshape(T, NSA_Q_DIM)) + g[..., 1:2] * heads(o_slc)
             + g[..., 2:3] * heads(o_win)).reshape(B, S, NSA_Q_DIM)
    xbc = jax.nn.silu(_causal_dwconv(xbc, conv_w, conv_b))
    xs, bm, cm = _split(xbc, (SSM_DIM, SSM_GROUPS * SSM_STATE, SSM_GROUPS * SSM_STATE))
    E = SSM_HEADS // SSM_GROUPS
    dt = jax.nn.softplus(dt + dt_bias)
    a = -jnp.exp(a_log)
    y = _ssd_chunked(xs.reshape(B, S, SSM_GROUPS, E, SSM_HEADDIM), dt.reshape(B, S, SSM_GROUPS, E),
                     a.reshape(SSM_GROUPS, E), bm.reshape(B, S, SSM_GROUPS, SSM_STATE),
                     cm.reshape(B, S, SSM_GROUPS, SSM_STATE))
    y = y.reshape(B, S, SSM_HEADS, SSM_HEADDIM) + d_skip[:, None] * xs.reshape(B, S, SSM_HEADS, SSM_HEADDIM)
    o_ssm = _gated_rmsnorm(y.reshape(B, S, SSM_DIM), z, norm_w)
    return jnp.concatenate([o_nsa, o_ssm], axis=-1)


def _fox_conformer_heads(proj_main, proj_small, B, S, f_bias, conv_w, conv_b, conf_ln_g, conf_ln_b):
    logf = jax.nn.log_sigmoid(proj_small[:, :FOX_HEADS].reshape(B, S, FOX_HEADS) + f_bias)
    F = jnp.cumsum(logf, axis=1).transpose(0, 2, 1).reshape(B, FOX_HEADS, 1, S)
    o_fox = flash_attention("fox", proj_main, 0, proj_main, FOX_DIM, proj_main, 2 * FOX_DIM, B, S,
                            FOX_HEADS // FOX_HEADS_PER_STEP, FOX_HEADS_PER_STEP, False, extra=F,
                            q_scale=HEAD_DIM ** -0.5)
    glu = proj_main[:, 3 * FOX_DIM:].astype(jnp.float32).reshape(B, S, 2 * CONF_DIM)
    u_val, u_gate = jnp.split(glu, 2, axis=-1)
    u = _causal_dwconv(u_val * jax.nn.sigmoid(u_gate), conv_w, conv_b)
    u = jax.nn.silu(_layer_norm(u, conf_ln_g, conf_ln_b))
    return jnp.concatenate([o_fox.astype(jnp.float32).reshape(B, S, FOX_DIM), u], axis=-1)


def kernel(x, c, positions, w_ada, b_ada, e_ada_table, e_ln1_g, e_ln1_b, e_ln2_g, e_ln2_b, e_w_in, e_cmp_pos, e_cmp_w1, e_cmp_w2, e_conv_w, e_conv_b, e_dt_bias, e_a_log, e_d_skip, e_ssm_norm_w, e_w_out, e_ffn_w_gate, e_ffn_w_up, e_ffn_w_down, o_ada_table, o_ln1_g, o_ln1_b, o_ln2_g, o_ln2_b, o_w_in, o_fox_f_bias, o_conf_conv_w, o_conf_conv_b, o_conf_ln_g, o_conf_ln_b, o_w_out, o_moe_router, o_moe_w_gate, o_moe_w_up, o_moe_w_down):
    B, S, D = x.shape
    T = B * S
    mod = ada_projection(c, w_ada, b_ada).reshape(B, 6, D)

    def mods(table):
        m = mod + table[None]
        return [m[:, k, None, :] for k in range(6)]

    layer_mods = [mods(e_ada_table[i // 2]) if i % 2 == 0 else mods(o_ada_table[i // 2]) for i in range(DEPTH)]
    eo = _col_offsets(E_SPLIT_SIZES)
    oo = _col_offsets(O_SPLIT_SIZES)
    e_small = eo[8] - eo[7] + eo[11] - eo[10]
    e_in = cast_cols(e_w_in.reshape(-1, eo[11]), E_MAIN + LANES,
                     [(0, eo[7], 0), (eo[8], eo[10] - eo[8], eo[7]), (eo[7], eo[8] - eo[7], E_MAIN),
                      (eo[10], eo[11] - eo[10], E_MAIN + eo[8] - eo[7])])
    assert e_small <= LANES
    o_in = cast_cols(o_w_in.reshape(-1, oo[5]), O_MAIN + LANES,
                     [(0, oo[3], 0), (oo[4], oo[5] - oo[4], oo[3]), (oo[3], oo[4] - oo[3], O_MAIN)])
    e_out, o_out = cast_bf16(e_w_out), cast_bf16(o_w_out)
    ffn_g, ffn_u, ffn_d = cast_bf16(e_ffn_w_gate), cast_bf16(e_ffn_w_up), cast_bf16(e_ffn_w_down)
    moe_g, moe_u = cast_bf16(o_moe_w_gate, tr=1024), cast_bf16(o_moe_w_up, tr=1024)
    moe_d = cast_bf16(o_moe_w_down)
    xf = x.reshape(T, D)
    h = modulate(xf, layer_mods[0][1], layer_mods[0][0], S)
    for i in range(DEPTH):
        j = i // 2
        sh1, sc1, g1, sh2, sc2, g2 = layer_mods[i]
        if i % 2 == 0:
            proj_main = matmul(h, e_in, jnp.bfloat16, TM, 1024, n=E_MAIN, w_row0=j * D)
            proj_small = matmul(h, e_in, jnp.float32, TM, LANES, n=LANES, w_row0=j * D, w_col0=E_MAIN)
            mix = _nsa_ssd_heads(proj_main, proj_small, positions, e_cmp_pos[j], e_cmp_w1[j], e_cmp_w2[j],
                                 e_conv_w[j], e_conv_b[j], e_dt_bias[j], e_a_log[j], e_d_skip[j], e_ssm_norm_w[j])
            y = matmul(_bf16(mix.reshape(T, MIX_DIM)), e_out, jnp.float32, TM, 1024, w_row0=j * MIX_DIM)
            xf, h = residual_layer_norm(xf, y, g1, e_ln1_g[j], e_ln1_b[j], sc2, sh2, S)
            hid = swiglu_up(h, ffn_g, ffn_u, 2 * TM, FFN_TN, w_row0=j * D)
            y = matmul(hid, ffn_d, jnp.float32, TM, 1024, tk=D_FF // 2, w_row0=j * D_FF)
            ln_g, ln_b = e_ln2_g[j], e_ln2_b[j]
        else:
            proj_main = matmul(h, o_in, jnp.bfloat16, TM, 1024, n=O_MAIN, w_row0=j * D)
            proj_small = matmul(h, o_in, jnp.float32, TM, LANES, n=LANES, w_row0=j * D, w_col0=O_MAIN)
            mix = _fox_conformer_heads(proj_main, proj_small, B, S, o_fox_f_bias[j], o_conf_conv_w[j],
                                       o_conf_conv_b[j], o_conf_ln_g[j], o_conf_ln_b[j])
            y = matmul(_bf16(mix.reshape(T, MIX_DIM)), o_out, jnp.float32, TM, 1024, w_row0=j * MIX_DIM)
            xf, h = residual_layer_norm(xf, y, g1, o_ln1_g[j], o_ln1_b[j], sc2, sh2, S)
            router = _bf16(jnp.pad(o_moe_router[j], ((0, 0), (0, LANES - N_EXPERTS))))
            gates_wide = moe_router(h, router, N_EXPERTS)
            hid = expert_swiglu_up(h, moe_g, moe_u, gates_wide, N_EXPERTS, j * N_EXPERTS, TM)
            y = matmul(hid, moe_d, jnp.float32, TM, 1024, tk=2 * D_EXPERT, w_row0=j * N_EXPERTS * D_EXPERT)
            ln_g, ln_b = o_ln2_g[j], o_ln2_b[j]
        if i + 1 < DEPTH:
            nsh, nsc = layer_mods[i + 1][0], layer_mods[i + 1][1]
        else:
            nsh, nsc = sh2, sc2
        xf, h = residual_layer_norm(xf, y, g2, ln_g, ln_b, nsc, nsh, S)
    return xf.reshape(B, S, D)
```

```python
import functools

import jax
import jax.numpy as jnp
import numpy as np
from jax import lax
from jax.experimental import pallas as pl
from jax.experimental.pallas import tpu as pltpu

D_MODEL = 4096
DEPTH = 4
HEAD_DIM = 128
MIX_DIM = D_MODEL

NSA_HEADS = 16
NSA_KV_HEADS = 4
NSA_REP = NSA_HEADS // NSA_KV_HEADS
NSA_Q_DIM = NSA_HEADS * HEAD_DIM
NSA_KV_DIM = NSA_KV_HEADS * HEAD_DIM
CMP_BLOCK = 32
CMP_STRIDE = 16
SLC_BLOCK = 64
N_SELECT = 16
WINDOW = 512
ROPE_THETA = 10000.0

SSM_DIM = MIX_DIM - NSA_Q_DIM
SSM_HEADDIM = 64
SSM_HEADS = SSM_DIM // SSM_HEADDIM
SSM_GROUPS = 8
SSM_STATE = 128
SSM_CONV = 4
SSM_CHUNK = 128
SSM_CONV_DIM = SSM_DIM + 2 * SSM_GROUPS * SSM_STATE

FOX_HEADS = 16
FOX_DIM = FOX_HEADS * HEAD_DIM

CONF_DIM = MIX_DIM - FOX_DIM
CONF_KERNEL = 31

D_FF = 11008
N_EXPERTS = 8
TOP_K = 2
D_EXPERT = 1408

DEEPNORM_ALPHA = (2.0 * DEPTH) ** 0.25
LN_EPS = 1e-5

E_SPLIT_SIZES = (NSA_Q_DIM,) + (NSA_KV_DIM,) * 6 + (3 * NSA_HEADS, SSM_DIM, SSM_CONV_DIM, SSM_HEADS)
O_SPLIT_SIZES = (FOX_DIM, FOX_DIM, FOX_DIM, FOX_HEADS, 2 * CONF_DIM)

LANES = 128
VMEM_LIMIT_BYTES = 56 * 1024 * 1024


def _round_up(n, m):
    return (n + m - 1) // m * m


def _params(*sem):
    return pltpu.CompilerParams(dimension_semantics=sem, vmem_limit_bytes=VMEM_LIMIT_BYTES)


def _cast_cols_kernel(w_ref, o_ref, *, segments, fill):
    if fill:
        o_ref[...] = jnp.zeros_like(o_ref)
    for src, width, dst in segments:
        o_ref[:, dst:dst + width] = w_ref[:, src:src + width].astype(o_ref.dtype)


def cast_cols(w, out_cols, segments, tr=256):
    rows, cols = w.shape
    assert rows % tr == 0
    fill = sum(width for _, width, _ in segments) != out_cols
    return pl.pallas_call(
        functools.partial(_cast_cols_kernel, segments=tuple(segments), fill=fill),
        grid=(rows // tr,),
        in_specs=[pl.BlockSpec((tr, cols), lambda i: (i, 0))],
        out_specs=pl.BlockSpec((tr, out_cols), lambda i: (i, 0)),
        out_shape=jax.ShapeDtypeStruct((rows, out_cols), jnp.bfloat16),
        compiler_params=_params("parallel"),
        name="cast_cols",
    )(w)


def cast_bf16(w, tr=256):
    w2 = w.reshape(-1, w.shape[-1])
    return cast_cols(w2, w2.shape[1], [(0, w2.shape[1], 0)], tr)


def _matmul_kernel(x_ref, w_ref, o_ref):
    o_ref[...] = jnp.dot(x_ref[...], w_ref[...], preferred_element_type=jnp.float32).astype(o_ref.dtype)


def _matmul_acc_kernel(x_ref, w_ref, o_ref, acc_ref):
    k = pl.program_id(2)

    @pl.when(k == 0)
    def _():
        acc_ref[...] = jnp.zeros_like(acc_ref)

    acc_ref[...] += jnp.dot(x_ref[...], w_ref[...], preferred_element_type=jnp.float32)

    @pl.when(k == pl.num_programs(2) - 1)
    def _():
        o_ref[...] = acc_ref[...].astype(o_ref.dtype)


def matmul(x, w, out_dtype, tm, tn, tk=None, n=None, w_row0=0, w_col0=0):
    m, k = x.shape
    n = w.shape[1] if n is None else n
    tk = k if tk is None else tk
    assert m % tm == 0 and n % tn == 0 and k % tk == 0 and w_row0 % tk == 0 and w_col0 % tn == 0
    r0, c0 = w_row0 // tk, w_col0 // tn
    if tk == k:
        return pl.pallas_call(
            _matmul_kernel,
            grid=(n // tn, m // tm),
            in_specs=[pl.BlockSpec((tm, k), lambda j, i: (i, 0)),
                      pl.BlockSpec((k, tn), lambda j, i: (r0, c0 + j))],
            out_specs=pl.BlockSpec((tm, tn), lambda j, i: (i, j)),
            out_shape=jax.ShapeDtypeStruct((m, n), out_dtype),
            compiler_params=_params("parallel", "parallel"),
            name="matmul",
        )(x, w)
    return pl.pallas_call(
        _matmul_acc_kernel,
        grid=(n // tn, m // tm, k // tk),
        in_specs=[pl.BlockSpec((tm, tk), lambda j, i, kk: (i, kk)),
                  pl.BlockSpec((tk, tn), lambda j, i, kk: (r0 + kk, c0 + j))],
        out_specs=pl.BlockSpec((tm, tn), lambda j, i, kk: (i, j)),
        out_shape=jax.ShapeDtypeStruct((m, n), out_dtype),
        scratch_shapes=[pltpu.VMEM((tm, tn), jnp.float32)],
        compiler_params=_params("parallel", "parallel", "arbitrary"),
        name="matmul_acc",
    )(x, w)


def _swiglu_up_kernel(*refs, gated):
    if gated:
        x_ref, wg_ref, wu_ref, gate_ref, o_ref = refs
    else:
        x_ref, wg_ref, wu_ref, o_ref = refs
    x = x_ref[...]
    g = jnp.dot(x, wg_ref[...], preferred_element_type=jnp.float32)
    u = jnp.dot(x, wu_ref[...], preferred_element_type=jnp.float32)
    hid = g * jax.nn.sigmoid(g) * u
    if gated:
        hid = hid * gate_ref[:, :1]
    o_ref[...] = hid.astype(o_ref.dtype)


def swiglu_up(x, wg, wu, tm, tn, w_row0=0):
    m, k = x.shape
    n = wg.shape[1]
    assert m % tm == 0 and n % tn == 0 and w_row0 % k == 0
    r0 = w_row0 // k
    return pl.pallas_call(
        functools.partial(_swiglu_up_kernel, gated=False),
        grid=(m // tm, n // tn),
        in_specs=[pl.BlockSpec((tm, k), lambda i, j: (i, 0)),
                  pl.BlockSpec((k, tn), lambda i, j: (r0, j)),
                  pl.BlockSpec((k, tn), lambda i, j: (r0, j))],
        out_specs=pl.BlockSpec((tm, tn), lambda i, j: (i, j)),
        out_shape=jax.ShapeDtypeStruct((m, n), jnp.bfloat16),
        compiler_params=_params("parallel", "parallel"),
        name="swiglu_up",
    )(x, wg, wu)


def expert_swiglu_up(x, wg, wu, gates_wide, n_experts, expert0, tm):
    m, k = x.shape
    d_e = wg.shape[1]
    return pl.pallas_call(
        functools.partial(_swiglu_up_kernel, gated=True),
        grid=(n_experts, m // tm),
        in_specs=[pl.BlockSpec((tm, k), lambda e, i: (i, 0)),
                  pl.BlockSpec((k, d_e), lambda e, i: (expert0 + e, 0), pipeline_mode=pl.Buffered(1)),
                  pl.BlockSpec((k, d_e), lambda e, i: (expert0 + e, 0), pipeline_mode=pl.Buffered(1)),
                  pl.BlockSpec((tm, LANES), lambda e, i: (i, e))],
        out_specs=pl.BlockSpec((tm, d_e), lambda e, i: (i, e)),
        out_shape=jax.ShapeDtypeStruct((m, n_experts * d_e), jnp.bfloat16),
        compiler_params=_params("parallel", "parallel"),
        name="expert_swiglu_up",
    )(x, wg, wu, gates_wide)


def _router_kernel(x_ref, w_ref, o_ref, *, n_experts):
    logits = jnp.dot(x_ref[...], w_ref[...], preferred_element_type=jnp.float32)
    lane = lax.broadcasted_iota(jnp.int32, logits.shape, 1)
    logits = jnp.where(lane < n_experts, logits, -jnp.inf)
    e = jnp.exp(logits - jnp.max(logits, axis=1, keepdims=True))
    p = e / jnp.sum(e, axis=1, keepdims=True)
    big = jnp.int32(LANES)
    v1 = jnp.max(p, axis=1, keepdims=True)
    i1 = jnp.min(jnp.where(p == v1, lane, big), axis=1, keepdims=True)
    p2 = jnp.where(lane == i1, -1.0, p)
    v2 = jnp.max(p2, axis=1, keepdims=True)
    i2 = jnp.min(jnp.where(p2 == v2, lane, big), axis=1, keepdims=True)
    denom = v1 + v2
    for ex in range(n_experts):
        g = jnp.where(i1 == ex, v1 / denom, jnp.where(i2 == ex, v2 / denom, 0.0))
        o_ref[:, ex * LANES:(ex + 1) * LANES] = jnp.broadcast_to(g, (g.shape[0], LANES))


def moe_router(x, router_padded, n_experts, tm=512):
    m, k = x.shape
    return pl.pallas_call(
        functools.partial(_router_kernel, n_experts=n_experts),
        grid=(m // tm,),
        in_specs=[pl.BlockSpec((tm, k), lambda i: (i, 0)),
                  pl.BlockSpec((k, LANES), lambda i: (0, 0))],
        out_specs=pl.BlockSpec((tm, n_experts * LANES), lambda i: (i, 0)),
        out_shape=jax.ShapeDtypeStruct((m, n_experts * LANES), jnp.float32),
        compiler_params=_params("parallel"),
        name="moe_router",
    )(x, router_padded)


def _ada_kernel(c_ref, w_ref, b_ref, o_ref):
    c = c_ref[...]
    a = c * jax.nn.sigmoid(c)
    o_ref[...] = jnp.dot(a, w_ref[...], preferred_element_type=jnp.float32,
                         precision=lax.Precision.HIGHEST) + b_ref[...]


def ada_projection(c, w_ada, b_ada):
    b, d = c.shape
    n = w_ada.shape[1]
    rows = _round_up(b, 8)
    c_pad = jnp.pad(c, ((0, rows - b), (0, 0)))
    tn = 512
    out = pl.pallas_call(
        _ada_kernel,
        grid=(n // tn,),
        in_specs=[pl.BlockSpec((rows, d), lambda j: (0, 0)),
                  pl.BlockSpec((d, tn), lambda j: (0, j)),
                  pl.BlockSpec((1, tn), lambda j: (0, j))],
        out_specs=pl.BlockSpec((rows, tn), lambda j: (0, j)),
        out_shape=jax.ShapeDtypeStruct((rows, n), jnp.float32),
        compiler_params=_params("parallel"),
        name="ada_projection",
    )(c_pad, w_ada, b_ada.reshape(1, n))
    return out[:b]


def _modulate_kernel(x_ref, sc_ref, sh_ref, h_ref):
    h_ref[...] = (x_ref[...] * (1.0 + sc_ref[0]) + sh_ref[0]).astype(h_ref.dtype)


def modulate(x, sc, sh, seq, tr=256):
    t, d = x.shape
    per_b = seq // tr
    vec = pl.BlockSpec((1, 1, d), lambda i: (i // per_b, 0, 0))
    return pl.pallas_call(
        _modulate_kernel,
        grid=(t // tr,),
        in_specs=[pl.BlockSpec((tr, d), lambda i: (i, 0)), vec, vec],
        out_specs=pl.BlockSpec((tr, d), lambda i: (i, 0)),
        out_shape=jax.ShapeDtypeStruct((t, d), jnp.bfloat16),
        compiler_params=_params("parallel"),
        name="modulate",
    )(x, sc, sh)


def _res_ln_kernel(x_ref, y_ref, gate_ref, g_ref, b_ref, sc_ref, sh_ref, xn_ref, h_ref):
    v = DEEPNORM_ALPHA * x_ref[...] + (1.0 + gate_ref[0]) * y_ref[...].astype(jnp.float32)
    mu = jnp.mean(v, axis=-1, keepdims=True)
    vc = v - mu
    var = jnp.mean(vc * vc, axis=-1, keepdims=True)
    xn = vc * lax.rsqrt(var + LN_EPS) * g_ref[...] + b_ref[...]
    xn_ref[...] = xn
    h_ref[...] = (xn * (1.0 + sc_ref[0]) + sh_ref[0]).astype(h_ref.dtype)


def residual_layer_norm(x, y, gate, ln_g, ln_b, sc_next, sh_next, seq, tr=256):
    t, d = x.shape
    per_b = seq // tr
    row = pl.BlockSpec((tr, d), lambda i: (i, 0))
    vec_b = pl.BlockSpec((1, 1, d), lambda i: (i // per_b, 0, 0))
    vec = pl.BlockSpec((1, d), lambda i: (0, 0))
    return pl.pallas_call(
        _res_ln_kernel,
        grid=(t // tr,),
        in_specs=[row, row, vec_b, vec, vec, vec_b, vec_b],
        out_specs=[row, row],
        out_shape=[jax.ShapeDtypeStruct((t, d), jnp.float32), jax.ShapeDtypeStruct((t, d), jnp.bfloat16)],
        compiler_params=_params("parallel"),
        name="residual_layer_norm",
    )(x, y, gate, ln_g.reshape(1, d), ln_b.reshape(1, d), sc_next, sh_next)


def _rope_kernel(pos_ref, inv_ref, sign_ref, q_ref, k1_ref, k2_ref, qo_ref, k1o_ref, k2o_ref, *, q_scale):
    ang = pos_ref[...].astype(jnp.float32) * inv_ref[...]
    cos = jnp.cos(ang)
    sin = jnp.sin(ang) * sign_ref[...]

    def rot(x_ref, o_ref, scale):
        for h in range(x_ref.shape[1] // HEAD_DIM):
            x = x_ref[:, h * HEAD_DIM:(h + 1) * HEAD_DIM].astype(jnp.float32)
            y = x * cos + pltpu.roll(x, HEAD_DIM // 2, 1) * sin
            o_ref[:, h * HEAD_DIM:(h + 1) * HEAD_DIM] = (y * scale).astype(o_ref.dtype)

    rot(q_ref, qo_ref, q_scale)
    rot(k1_ref, k1o_ref, 1.0)
    rot(k2_ref, k2o_ref, 1.0)


def rope_qkk(proj, pos_col, q_cols, k1_cols, k2_cols, tr=512):
    t = proj.shape[0]
    half = HEAD_DIM // 2
    inv = ROPE_THETA ** (-jnp.arange(half, dtype=jnp.float32) / half)
    inv = jnp.concatenate([inv, inv]).reshape(1, HEAD_DIM)
    sign = jnp.concatenate([-jnp.ones((half,), jnp.float32), jnp.ones((half,), jnp.float32)]).reshape(1, HEAD_DIM)

    def cols(c):
        start, width = c
        assert start % width == 0
        return pl.BlockSpec((tr, width), lambda i: (i, start // width))

    def out(c):
        return pl.BlockSpec((tr, c[1]), lambda i: (i, 0))

    vec = pl.BlockSpec((1, HEAD_DIM), lambda i: (0, 0))
    return pl.pallas_call(
        functools.partial(_rope_kernel, q_scale=HEAD_DIM ** -0.5),
        grid=(t // tr,),
        in_specs=[pl.BlockSpec((tr, 1), lambda i: (i, 0)), vec, vec, cols(q_cols), cols(k1_cols), cols(k2_cols)],
        out_specs=[out(q_cols), out(k1_cols), out(k2_cols)],
        out_shape=[jax.ShapeDtypeStruct((t, c[1]), jnp.bfloat16) for c in (q_cols, k1_cols, k2_cols)],
        compiler_params=_params("parallel"),
        name="rope_qkk",
    )(pos_col, inv, sign, proj, proj, proj)


MASK_VALUE = -1e30
ATT_TILE = 512


def _flash_kernel(*refs, mode, heads, shared_kv, tile, n_win_tiles, q_scale):
    if mode == "fox":
        q_ref, k_ref, v_ref, f_ref, o_ref, m_sc, l_sc, acc_sc = refs
    elif mode == "sel":
        q_ref, k_ref, v_ref, sel_ref, o_ref, m_sc, l_sc, acc_sc = refs
    else:
        q_ref, k_ref, v_ref, o_ref, m_sc, l_sc, acc_sc = refs
    qi = pl.program_id(2)
    kj = pl.program_id(3)
    if mode == "win":
        kt = qi - (n_win_tiles - 1) + kj
        active = kt >= 0
        last = kj == n_win_tiles - 1
    else:
        kt = kj
        active = kj <= qi
        last = kj == qi

    @pl.when(kj == 0)
    def _():
        m_sc[...] = jnp.full_like(m_sc, MASK_VALUE)
        l_sc[...] = jnp.zeros_like(l_sc)
        acc_sc[...] = jnp.zeros_like(acc_sc)

    def step(masked):
        mask = None
        if masked:
            row = qi * tile + lax.broadcasted_iota(jnp.int32, (tile, tile), 0)
            col = kt * tile + lax.broadcasted_iota(jnp.int32, (tile, tile), 1)
            mask = col <= row
            if mode == "win":
                mask = mask & (row - col < WINDOW)
            if mode == "sel":
                n_blk = sel_ref.shape[-1]
                blk = lax.broadcasted_iota(jnp.int32, (n_blk, tile), 0)
                key_blk = (kt * tile + lax.broadcasted_iota(jnp.int32, (n_blk, tile), 1)) // SLC_BLOCK
                expand = jnp.where(blk == key_blk, 1.0, 0.0).astype(jnp.bfloat16)
                member = jnp.dot(sel_ref[0, 0], expand, preferred_element_type=jnp.float32)
                mask = mask & (member > 0.5)
        for r in range(heads):
            q = q_ref[:, r * HEAD_DIM:(r + 1) * HEAD_DIM]
            if q_scale != 1.0:
                q = (q.astype(jnp.float32) * q_scale).astype(q.dtype)
            kv = 0 if shared_kv else r
            k = k_ref[:, kv * HEAD_DIM:(kv + 1) * HEAD_DIM]
            v = v_ref[:, kv * HEAD_DIM:(kv + 1) * HEAD_DIM]
            s = lax.dot_general(q, k, (((1,), (1,)), ((), ())), preferred_element_type=jnp.float32)
            if mode == "fox":
                s = s - f_ref[0, r]
            if masked:
                s = jnp.where(mask, s, MASK_VALUE)
            m_prev = m_sc[r]
            m_new = jnp.maximum(m_prev, jnp.max(s, axis=1, keepdims=True))
            alpha = jnp.exp(m_prev - m_new)
            p = jnp.exp(s - m_new[:, :1])
            if masked:
                p = jnp.where(mask, p, 0.0)
            l_sc[r] = alpha * l_sc[r] + jnp.sum(p, axis=1, keepdims=True)
            acc_sc[r] = alpha * acc_sc[r] + jnp.dot(p.astype(v.dtype), v, preferred_element_type=jnp.float32)
            m_sc[r] = m_new

    if mode == "fox":
        @pl.when(kj < qi)
        def _():
            step(False)

        @pl.when(kj == qi)
        def _():
            step(True)
    else:
        @pl.when(active)
        def _():
            step(True)

    @pl.when(last)
    def _():
        for r in range(heads):
            inv_l = 1.0 / l_sc[r]
            o_ref[:, r * HEAD_DIM:(r + 1) * HEAD_DIM] = (acc_sc[r] * inv_l).astype(o_ref.dtype)


def flash_attention(mode, q_arr, q_col0, k_arr, k_col0, v_arr, v_col0, batch, seq, n_groups, heads,
                    shared_kv, extra=None, q_scale=1.0, tile=ATT_TILE):
    t = batch * seq
    nq = seq // tile
    qw = heads * HEAD_DIM
    kvw = HEAD_DIM if shared_kv else qw
    assert q_col0 % qw == 0 and k_col0 % kvw == 0 and v_col0 % kvw == 0 and seq % tile == 0
    n_win_tiles = WINDOW // tile + 1
    nk = n_win_tiles if mode == "win" else nq

    def k_tile(qi, kj):
        if mode == "win":
            return jnp.maximum(qi - (n_win_tiles - 1) + kj, 0)
        return jnp.minimum(kj, qi)

    in_specs = [
        pl.BlockSpec((tile, qw), lambda b, g, qi, kj: (b * nq + qi, q_col0 // qw + g)),
        pl.BlockSpec((tile, kvw), lambda b, g, qi, kj: (b * nq + k_tile(qi, kj), k_col0 // kvw + g)),
        pl.BlockSpec((tile, kvw), lambda b, g, qi, kj: (b * nq + k_tile(qi, kj), v_col0 // kvw + g)),
    ]
    args = [q_arr, k_arr, v_arr]
    if mode == "sel":
        in_specs.append(pl.BlockSpec((1, 1, tile, extra.shape[-1]), lambda b, g, qi, kj: (b, g, qi, 0)))
        args.append(extra)
    elif mode == "fox":
        in_specs.append(pl.BlockSpec((1, heads, 1, tile), lambda b, g, qi, kj: (b, g, 0, k_tile(qi, kj))))
        args.append(extra)
    stat = pltpu.VMEM((heads, tile, HEAD_DIM), jnp.float32)
    return pl.pallas_call(
        functools.partial(_flash_kernel, mode=mode, heads=heads, shared_kv=shared_kv, tile=tile,
                          n_win_tiles=n_win_tiles, q_scale=q_scale),
        grid=(batch, n_groups, nq, nk),
        in_specs=in_specs,
        out_specs=pl.BlockSpec((tile, qw), lambda b, g, qi, kj: (b * nq + qi, g)),
        out_shape=jax.ShapeDtypeStruct((t, n_groups * qw), jnp.bfloat16),
        scratch_shapes=[stat, stat, stat],
        compiler_params=_params("parallel", "parallel", "parallel", "arbitrary"),
        name="flash_" + mode,
    )(*args)


SSD_COL_TILE = 1024
HALO_ROWS = 8


def _cumsum_rows(x):
    n = x.shape[0]
    row = lax.broadcasted_iota(jnp.int32, x.shape, 0)
    k = 1
    while k < n:
        x = x + jnp.where(row >= k, pltpu.roll(x, k, 0), 0.0)
        k *= 2
    return x


def _ssd_kernel(xs0_ref, xs1_ref, bm_ref, cm_ref, z0_ref, z1_ref, small_ref, conv_w_ref, conv_b_ref,
                dt_bias_ref, a_log_ref, d_skip_ref, norm_w_ref, o_ref, buf_sc, act_sc, state_sc, *, dt_lane0):
    L, P, N = SSM_CHUNK, SSM_HEADDIM, SSM_STATE
    heads_per_group = SSM_HEADS // SSM_GROUPS
    c = pl.program_id(1)

    @pl.when(c == 0)
    def _():
        buf_sc[0:HALO_ROWS, :] = jnp.zeros((HALO_ROWS, SSM_CONV_DIM), jnp.float32)
        state_sc[...] = jnp.zeros_like(state_sc)

    srcs = (xs0_ref, xs1_ref, bm_ref, cm_ref)
    cw = 512
    for cb in range(SSM_CONV_DIM // cw):
        src = srcs[cb * cw // SSD_COL_TILE]
        off = cb * cw % SSD_COL_TILE
        cols = slice(cb * cw, (cb + 1) * cw)
        buf_sc[HALO_ROWS:HALO_ROWS + L, cols] = src[:, off:off + cw].astype(jnp.float32)
        acc = jnp.broadcast_to(conv_b_ref[:, cols], (L, cw))
        for k in range(SSM_CONV):
            start = HALO_ROWS - (SSM_CONV - 1) + k
            acc = acc + conv_w_ref[k:k + 1, cols] * buf_sc[start:start + L, cols]
        act_sc[:, cols] = acc * jax.nn.sigmoid(acc)
        buf_sc[0:HALO_ROWS, cols] = buf_sc[L:L + HALO_ROWS, cols]

    pre = small_ref[...] + dt_bias_ref[...]
    dt = jnp.maximum(pre, 0.0) + jnp.log(1.0 + jnp.exp(-jnp.abs(pre)))
    a_cum = _cumsum_rows(dt * (-jnp.exp(a_log_ref[...])))
    a_cum_t = a_cum.T
    a_last = a_cum[L - 1:L, :]
    tril = lax.broadcasted_iota(jnp.int32, (L, L), 0) >= lax.broadcasted_iota(jnp.int32, (L, L), 1)

    for g in range(SSM_GROUPS):
        bc = act_sc[:, SSM_DIM + g * N:SSM_DIM + (g + 1) * N]
        cc = act_sc[:, SSM_DIM + SSM_GROUPS * N + g * N:SSM_DIM + SSM_GROUPS * N + (g + 1) * N].astype(jnp.bfloat16)
        cb_mat = lax.dot_general(cc, bc.astype(jnp.bfloat16), (((1,), (1,)), ((), ())),
                                 preferred_element_type=jnp.float32)
        bc_t = bc.T.astype(jnp.bfloat16)
        yz = []
        ss = jnp.zeros((L, 1), jnp.float32)
        for e in range(heads_per_group):
            h = g * heads_per_group + e
            lane = dt_lane0 + h
            dt_col = dt[:, lane:lane + 1]
            ac_col = a_cum[:, lane:lane + 1]
            ac_row = a_cum_t[lane:lane + 1, :]
            last = a_last[:, lane:lane + 1]
            decay = jnp.where(tril, jnp.exp(jnp.minimum(ac_col - ac_row, 0.0)), 0.0)
            xs = act_sc[:, h * P:(h + 1) * P]
            xd = xs * dt_col
            y = jnp.dot((cb_mat * decay).astype(jnp.bfloat16), xd.astype(jnp.bfloat16),
                        preferred_element_type=jnp.float32)
            st = state_sc[h]
            y = y + jnp.dot(cc, st.astype(jnp.bfloat16), preferred_element_type=jnp.float32) * jnp.exp(ac_col)
            xdd = (xd * jnp.exp(last - ac_col)).astype(jnp.bfloat16)
            state_sc[h] = jnp.exp(last) * st + jnp.dot(bc_t, xdd, preferred_element_type=jnp.float32)
            y = y + d_skip_ref[:, h:h + 1] * xs
            z_ref = z0_ref if h * P < SSD_COL_TILE else z1_ref
            zc = h * P % SSD_COL_TILE
            z = z_ref[:, zc:zc + P].astype(jnp.float32)
            y = y * (z * jax.nn.sigmoid(z))
            ss = ss + jnp.sum(y * y, axis=1, keepdims=True)
            yz.append(y)
        scale = lax.rsqrt(ss / (heads_per_group * P) + LN_EPS)
        for e in range(heads_per_group):
            h = g * heads_per_group + e
            o_ref[:, h * P:(h + 1) * P] = (yz[e] * scale * norm_w_ref[:, h * P:(h + 1) * P]).astype(o_ref.dtype)


def ssd_mixer(proj_main, proj_small, batch, seq, xbc_col0, z_col0, dt_lane0, conv_w, conv_b, dt_bias, a_log,
              d_skip, norm_w):
    t = batch * seq
    nc = seq // SSM_CHUNK
    ct = SSD_COL_TILE
    assert xbc_col0 % ct == 0 and z_col0 % ct == 0 and SSM_DIM == 2 * ct and SSM_GROUPS * SSM_STATE == ct

    def view(col0):
        return pl.BlockSpec((SSM_CHUNK, ct), lambda b, c: (b * nc + c, col0 // ct))

    def lane_row(v):
        return jnp.pad(v.astype(jnp.float32), (dt_lane0, LANES - dt_lane0 - v.shape[0])).reshape(1, LANES)

    full = lambda a: pl.BlockSpec(a.shape, lambda b, c: (0, 0))
    consts = [conv_w, conv_b.reshape(1, -1), lane_row(dt_bias), lane_row(a_log),
              jnp.pad(d_skip, (0, LANES - SSM_HEADS)).reshape(1, LANES), norm_w.reshape(1, -1)]
    return pl.pallas_call(
        functools.partial(_ssd_kernel, dt_lane0=dt_lane0),
        grid=(batch, nc),
        in_specs=[view(xbc_col0), view(xbc_col0 + ct), view(xbc_col0 + 2 * ct), view(xbc_col0 + 3 * ct),
                  view(z_col0), view(z_col0 + ct),
                  pl.BlockSpec((SSM_CHUNK, LANES), lambda b, c: (b * nc + c, 0))] + [full(a) for a in consts],
        out_specs=pl.BlockSpec((SSM_CHUNK, SSM_DIM), lambda b, c: (b * nc + c, 0)),
        out_shape=jax.ShapeDtypeStruct((t, SSM_DIM), jnp.bfloat16),
        scratch_shapes=[pltpu.VMEM((SSM_CHUNK + HALO_ROWS, SSM_CONV_DIM), jnp.float32),
                        pltpu.VMEM((SSM_CHUNK, SSM_CONV_DIM), jnp.float32),
                        pltpu.VMEM((SSM_HEADS, SSM_STATE, SSM_HEADDIM), jnp.float32)],
        compiler_params=_params("parallel", "arbitrary"),
        name="ssd_mixer",
    )(proj_main, proj_main, proj_main, proj_main, proj_main, proj_main, proj_small, *consts)


BIG = 3.0e38


def _rope_rows(x, pos_col, inv_row, sign_row):
    ang = pos_col.astype(jnp.float32) * inv_row
    return x * jnp.cos(ang) + pltpu.roll(x, HEAD_DIM // 2, 1) * (jnp.sin(ang) * sign_row)


def _gelu_tanh(x):
    return 0.5 * x * (1.0 + jnp.tanh(0.7978845608028654 * (x + 0.044715 * x * x * x)))


def _compress_kernel(kc_ref, vc_ref, pos_ref, emb_ref, w1_ref, w2_ref, inv_ref, sign_ref, ko_ref, vo_ref, x_sc):
    n = ko_ref.shape[2]
    half = CMP_BLOCK // 2
    assert half == CMP_STRIDE
    for which, src, dst in ((0, kc_ref, ko_ref), (1, vc_ref, vo_ref)):
        x_sc[...] = src[...].astype(jnp.float32)
        lo = jnp.zeros((n, HEAD_DIM), jnp.float32)
        hi = jnp.zeros((n, HEAD_DIM), jnp.float32)
        for p in range(half):
            xp = x_sc[pl.ds(p, n, stride=CMP_STRIDE), :]
            for part, tok in ((0, p), (1, half + p)):
                xb = (xp + emb_ref[which, tok:tok + 1, :]).astype(jnp.bfloat16)
                term = jnp.dot(xb, w1_ref[which, tok * HEAD_DIM:(tok + 1) * HEAD_DIM, :],
                               preferred_element_type=jnp.float32)
                if part == 0:
                    lo = lo + term
                else:
                    hi = hi + term
        pre = lo + pltpu.roll(hi, n - 1, 0)
        out = jnp.dot(_gelu_tanh(pre).astype(jnp.bfloat16), w2_ref[which], preferred_element_type=jnp.float32)
        if which == 0:
            out = _rope_rows(out, pos_ref[0], inv_ref[...], sign_ref[...])
        dst[0, 0] = out.astype(dst.dtype)


def nsa_compress(proj_main, kc_col0, vc_col0, pos_cmp, cmp_pos, cmp_w1, cmp_w2, batch, seq):
    n = seq // CMP_STRIDE
    half = HEAD_DIM // 2
    inv = ROPE_THETA ** (-jnp.arange(half, dtype=jnp.float32) / half)
    inv = jnp.concatenate([inv, inv]).reshape(1, HEAD_DIM)
    sign = jnp.concatenate([-jnp.ones((half,), jnp.float32), jnp.ones((half,), jnp.float32)]).reshape(1, HEAD_DIM)
    w1 = cmp_w1.astype(jnp.bfloat16)
    w2 = cmp_w2.astype(jnp.bfloat16)
    full = lambda a: pl.BlockSpec(a.shape, lambda b, g: (0,) * a.ndim)
    out = jax.ShapeDtypeStruct((batch, NSA_KV_HEADS, n, HEAD_DIM), jnp.bfloat16)
    out_spec = pl.BlockSpec((1, 1, n, HEAD_DIM), lambda b, g: (b, g, 0, 0))
    return pl.pallas_call(
        _compress_kernel,
        grid=(batch, NSA_KV_HEADS),
        in_specs=[pl.BlockSpec((seq, HEAD_DIM), lambda b, g: (b, kc_col0 // HEAD_DIM + g)),
                  pl.BlockSpec((seq, HEAD_DIM), lambda b, g: (b, vc_col0 // HEAD_DIM + g)),
                  pl.BlockSpec((1, n, 1), lambda b, g: (b, 0, 0)),
                  full(cmp_pos), full(w1), full(w2), full(inv), full(sign)],
        out_specs=[out_spec, out_spec],
        out_shape=[out, out],
        scratch_shapes=[pltpu.VMEM((seq, HEAD_DIM), jnp.float32)],
        compiler_params=_params("parallel", "parallel"),
        name="nsa_compress",
    )(proj_main, proj_main, pos_cmp, cmp_pos, w1, w2, inv, sign)


def _softmax_rows_or_zero(s, mask, axis):
    s = jnp.where(mask, s, MASK_VALUE)
    e = jnp.where(mask, jnp.exp(s - jnp.max(s, axis=axis, keepdims=True)), 0.0)
    d = jnp.sum(e, axis=axis, keepdims=True)
    return e / jnp.where(d > 0.0, d, 1.0)


def _cmp_select_kernel(q_ref, k_ref, v_ref, o_ref, sel_ref, *, tq, n_slc):
    qi = pl.program_id(2)
    k = k_ref[0, 0]
    v = v_ref[0, 0]
    n = k.shape[0]
    last_tok = CMP_BLOCK - 1
    t_row = qi * tq + lax.broadcasted_iota(jnp.int32, (tq, n), 0)
    mask = CMP_STRIDE * lax.broadcasted_iota(jnp.int32, (tq, n), 1) + last_tok <= t_row
    t_lane = qi * tq + lax.broadcasted_iota(jnp.int32, (n, tq), 1)
    mask_t = CMP_STRIDE * lax.broadcasted_iota(jnp.int32, (n, tq), 0) + last_tok <= t_lane
    p_sum_t = jnp.zeros((n, tq), jnp.float32)
    for r in range(NSA_REP):
        q = q_ref[:, r * HEAD_DIM:(r + 1) * HEAD_DIM]
        s = lax.dot_general(q, k, (((1,), (1,)), ((), ())), preferred_element_type=jnp.float32)
        p = _softmax_rows_or_zero(s, mask, 1)
        o_ref[:, r * HEAD_DIM:(r + 1) * HEAD_DIM] = jnp.dot(
            p.astype(v.dtype), v, preferred_element_type=jnp.float32).astype(o_ref.dtype)
        s_t = lax.dot_general(k, q, (((1,), (1,)), ((), ())), preferred_element_type=jnp.float32)
        p_sum_t = p_sum_t + _softmax_rows_or_zero(s_t, mask_t, 0)
    blk_n = lax.broadcasted_iota(jnp.int32, (n_slc, n), 0) * SLC_BLOCK
    start_n = lax.broadcasted_iota(jnp.int32, (n_slc, n), 1) * CMP_STRIDE
    overlap_t = jnp.where((start_n < blk_n + SLC_BLOCK) & (start_n + CMP_BLOCK > blk_n), 1.0, 0.0).astype(jnp.bfloat16)
    p_hi = p_sum_t.astype(jnp.bfloat16)
    p_lo = (p_sum_t - p_hi.astype(jnp.float32)).astype(jnp.bfloat16)
    imp = (jnp.dot(overlap_t, p_hi, preferred_element_type=jnp.float32)
           + jnp.dot(overlap_t, p_lo, preferred_element_type=jnp.float32))
    blk = lax.broadcasted_iota(jnp.int32, (n_slc, tq), 0)
    qblk = (qi * tq + lax.broadcasted_iota(jnp.int32, (n_slc, tq), 1)) // SLC_BLOCK
    forced = (blk == 0) | (blk == qblk) | (blk == qblk - 1)
    imp = jnp.where(forced, BIG, jnp.where(blk > qblk, -BIG, imp))
    rank = jnp.zeros((n_slc, tq), jnp.float32)
    for j in range(n_slc):
        other = imp[j:j + 1, :]
        beats = (other > imp) | ((other == imp) & (blk > j))
        rank = rank + jnp.where(beats, 1.0, 0.0)
    member = jnp.where(rank < min(N_SELECT, n_slc), 1.0, 0.0)
    sel_ref[0, 0] = member.T.astype(sel_ref.dtype)


def nsa_compressed_attention(q_rot, k_cmp, v_cmp, batch, seq, tq=256):
    t = batch * seq
    nq = seq // tq
    n_slc = seq // SLC_BLOCK
    n = k_cmp.shape[2]
    qw = NSA_REP * HEAD_DIM
    kv_spec = pl.BlockSpec((1, 1, n, HEAD_DIM), lambda b, g, qi: (b, g, 0, 0))
    return pl.pallas_call(
        functools.partial(_cmp_select_kernel, tq=tq, n_slc=n_slc),
        grid=(batch, NSA_KV_HEADS, nq),
        in_specs=[pl.BlockSpec((tq, qw), lambda b, g, qi: (b * nq + qi, g)), kv_spec, kv_spec],
        out_specs=[pl.BlockSpec((tq, qw), lambda b, g, qi: (b * nq + qi, g)),
                   pl.BlockSpec((1, 1, tq, n_slc), lambda b, g, qi: (b, g, qi, 0))],
        out_shape=[jax.ShapeDtypeStruct((t, NSA_Q_DIM), jnp.bfloat16),
                   jax.ShapeDtypeStruct((batch, NSA_KV_HEADS, seq, n_slc), jnp.bfloat16)],
        compiler_params=_params("parallel", "parallel", "parallel"),
        name="nsa_cmp_select",
    )(q_rot, k_cmp, v_cmp)


def _nsa_combine_kernel(cmp_ref, slc_ref, win_ref, gate_ref, ssm_ref, o_ref):
    g = jax.nn.sigmoid(gate_ref[...])
    for h in range(NSA_HEADS):
        c = slice(h * HEAD_DIM, (h + 1) * HEAD_DIM)
        o = (g[:, 3 * h:3 * h + 1] * cmp_ref[:, c].astype(jnp.float32)
             + g[:, 3 * h + 1:3 * h + 2] * slc_ref[:, c].astype(jnp.float32)
             + g[:, 3 * h + 2:3 * h + 3] * win_ref[:, c].astype(jnp.float32))
        o_ref[:, c] = o.astype(o_ref.dtype)
    o_ref[:, NSA_Q_DIM:] = ssm_ref[...]


def nsa_combine(o_cmp, o_slc, o_win, proj_small, o_ssm, tr=512):
    t = o_cmp.shape[0]
    row = lambda w: pl.BlockSpec((tr, w), lambda i: (i, 0))
    return pl.pallas_call(
        _nsa_combine_kernel,
        grid=(t // tr,),
        in_specs=[row(NSA_Q_DIM), row(NSA_Q_DIM), row(NSA_Q_DIM), row(LANES), row(SSM_DIM)],
        out_specs=row(MIX_DIM),
        out_shape=jax.ShapeDtypeStruct((t, MIX_DIM), jnp.bfloat16),
        compiler_params=_params("parallel"),
        name="nsa_combine",
    )(o_cmp, o_slc, o_win, proj_small, o_ssm)


CONF_HALO = 32


def _conformer_kernel(val_ref, gate_ref, fox_ref, w_ref, b_ref, lng_ref, lnb_ref, o_ref, buf_sc, act_sc, *, ts):
    @pl.when(pl.program_id(1) == 0)
    def _():
        buf_sc[0:CONF_HALO, :] = jnp.zeros((CONF_HALO, CONF_DIM), jnp.float32)

    cw = LANES
    total = jnp.zeros((ts, 1), jnp.float32)
    for cb in range(CONF_DIM // cw):
        cols = slice(cb * cw, (cb + 1) * cw)
        gate = gate_ref[:, cols].astype(jnp.float32)
        buf_sc[CONF_HALO:CONF_HALO + ts, cols] = val_ref[:, cols].astype(jnp.float32) * jax.nn.sigmoid(gate)
        acc = jnp.broadcast_to(b_ref[:, cols], (ts, cw))
        for k in range(CONF_KERNEL):
            start = CONF_HALO - (CONF_KERNEL - 1) + k
            acc = acc + w_ref[k:k + 1, cols] * buf_sc[start:start + ts, cols]
        act_sc[:, cols] = acc
        total = total + jnp.sum(acc, axis=1, keepdims=True)
        buf_sc[0:CONF_HALO, cols] = buf_sc[ts:ts + CONF_HALO, cols]
    u = act_sc[...]
    mu = total / CONF_DIM
    uc = u - mu
    var = jnp.mean(uc * uc, axis=1, keepdims=True)
    y = uc * lax.rsqrt(var + LN_EPS) * lng_ref[...] + lnb_ref[...]
    o_ref[:, FOX_DIM:] = (y * jax.nn.sigmoid(y)).astype(o_ref.dtype)
    o_ref[:, :FOX_DIM] = fox_ref[...]


def conformer_mixer(proj_main, glu_col0, o_fox, batch, seq, conv_w, conv_b, ln_g, ln_b, ts=256):
    t = batch * seq
    ns = seq // ts
    assert glu_col0 % CONF_DIM == 0
    full = lambda a: pl.BlockSpec(a.shape, lambda b, s: (0, 0))
    consts = [conv_w, conv_b.reshape(1, -1), ln_g.reshape(1, -1), ln_b.reshape(1, -1)]
    return pl.pallas_call(
        functools.partial(_conformer_kernel, ts=ts),
        grid=(batch, ns),
        in_specs=[pl.BlockSpec((ts, CONF_DIM), lambda b, s: (b * ns + s, glu_col0 // CONF_DIM)),
                  pl.BlockSpec((ts, CONF_DIM), lambda b, s: (b * ns + s, glu_col0 // CONF_DIM + 1)),
                  pl.BlockSpec((ts, FOX_DIM), lambda b, s: (b * ns + s, 0))] + [full(a) for a in consts],
        out_specs=pl.BlockSpec((ts, MIX_DIM), lambda b, s: (b * ns + s, 0)),
        out_shape=jax.ShapeDtypeStruct((t, MIX_DIM), jnp.bfloat16),
        scratch_shapes=[pltpu.VMEM((ts + CONF_HALO, CONF_DIM), jnp.float32),
                        pltpu.VMEM((ts, CONF_DIM), jnp.float32)],
        compiler_params=_params("parallel", "arbitrary"),
        name="conformer_mixer",
    )(proj_main, proj_main, o_fox, *consts)


def _fox_decay_kernel(f_ref, bias_ref, o_ref):
    pre = f_ref[...] + bias_ref[...]
    logf = jnp.minimum(pre, 0.0) - jnp.log(1.0 + jnp.exp(-jnp.abs(pre)))
    o_ref[0] = _cumsum_rows(logf).T[:FOX_HEADS, :]


def fox_decay(proj_small, f_bias, batch, seq):
    bias = jnp.pad(f_bias.astype(jnp.float32), (0, LANES - FOX_HEADS)).reshape(1, LANES)
    out = pl.pallas_call(
        _fox_decay_kernel,
        grid=(batch,),
        in_specs=[pl.BlockSpec((seq, LANES), lambda b: (b, 0)), pl.BlockSpec((1, LANES), lambda b: (0, 0))],
        out_specs=pl.BlockSpec((1, FOX_HEADS, seq), lambda b: (b, 0, 0)),
        out_shape=jax.ShapeDtypeStruct((batch, FOX_HEADS, seq), jnp.float32),
        compiler_params=_params("parallel"),
        name="fox_decay",
    )(proj_small, bias)
    return out.reshape(batch, FOX_HEADS, 1, seq)


TM = 512
FFN_TN = 256
FOX_HEADS_PER_STEP = 2
E_MAIN = NSA_Q_DIM + 6 * NSA_KV_DIM + SSM_DIM + SSM_CONV_DIM
O_MAIN = 3 * FOX_DIM + 2 * CONF_DIM
assert TOP_K == 2


def _bf16(w):
    return w.astype(jnp.bfloat16)


def _col_offsets(sizes):
    return np.concatenate([[0], np.cumsum(sizes)]).tolist()


def _nsa_ssd_heads(proj_main, proj_small, positions, cmp_pos, cmp_w1, cmp_w2, conv_w, conv_b, dt_bias, a_log,
                   d_skip, norm_w):
    B, S = positions.shape
    T = B * S
    G, R = NSA_KV_HEADS, NSA_REP
    kv0 = [NSA_Q_DIM + i * NSA_KV_DIM for i in range(6)]
    z0 = NSA_Q_DIM + 6 * NSA_KV_DIM
    q_rot, ksl_rot, kw_rot = rope_qkk(proj_main, positions.reshape(T, 1), (0, NSA_Q_DIM),
                                      (kv0[2], NSA_KV_DIM), (kv0[4], NSA_KV_DIM))
    n = S // CMP_STRIDE
    pos_cmp = positions[:, CMP_BLOCK - 1::CMP_STRIDE]
    pos_cmp = jnp.pad(pos_cmp, ((0, 0), (0, n - pos_cmp.shape[1]))).reshape(B, n, 1)
    k_cmp, v_cmp = nsa_compress(proj_main, kv0[0], kv0[1], pos_cmp, cmp_pos, cmp_w1, cmp_w2, B, S)
    o_cmp, member = nsa_compressed_attention(q_rot, k_cmp, v_cmp, B, S)
    o_slc = flash_attention("sel", q_rot, 0, ksl_rot, 0, proj_main, kv0[3], B, S, G, R, True, extra=member)
    o_win = flash_attention("win", q_rot, 0, kw_rot, 0, proj_main, kv0[5], B, S, G, R, True)
    o_ssm = ssd_mixer(proj_main, proj_small, B, S, z0 + SSM_DIM, z0, 3 * NSA_HEADS, conv_w, conv_b, dt_bias,
                      a_log, d_skip, norm_w)
    return nsa_combine(o_cmp, o_slc, o_win, proj_small, o_ssm)


def _fox_conformer_heads(proj_main, proj_small, B, S, f_bias, conv_w, conv_b, conf_ln_g, conf_ln_b):
    F = fox_decay(proj_small, f_bias, B, S)
    o_fox = flash_attention("fox", proj_main, 0, proj_main, FOX_DIM, proj_main, 2 * FOX_DIM, B, S,
                            FOX_HEADS // FOX_HEADS_PER_STEP, FOX_HEADS_PER_STEP, False, extra=F,
                            q_scale=HEAD_DIM ** -0.5)
    return conformer_mixer(proj_main, 3 * FOX_DIM, o_fox, B, S, conv_w, conv_b, conf_ln_g, conf_ln_b)


def kernel(x, c, positions, w_ada, b_ada, e_ada_table, e_ln1_g, e_ln1_b, e_ln2_g, e_ln2_b, e_w_in, e_cmp_pos, e_cmp_w1, e_cmp_w2, e_conv_w, e_conv_b, e_dt_bias, e_a_log, e_d_skip, e_ssm_norm_w, e_w_out, e_ffn_w_gate, e_ffn_w_up, e_ffn_w_down, o_ada_table, o_ln1_g, o_ln1_b, o_ln2_g, o_ln2_b, o_w_in, o_fox_f_bias, o_conf_conv_w, o_conf_conv_b, o_conf_ln_g, o_conf_ln_b, o_w_out, o_moe_router, o_moe_w_gate, o_moe_w_up, o_moe_w_down):
    B, S, D = x.shape
    T = B * S
    mod = ada_projection(c, w_ada, b_ada).reshape(B, 6, D)

    def mods(table):
        m = mod + table[None]
        return [m[:, k, None, :] for k in range(6)]

    layer_mods = [mods(e_ada_table[i // 2]) if i % 2 == 0 else mods(o_ada_table[i // 2]) for i in range(DEPTH)]
    eo = _col_offsets(E_SPLIT_SIZES)
    oo = _col_offsets(O_SPLIT_SIZES)
    e_small = eo[8] - eo[7] + eo[11] - eo[10]
    e_in = cast_cols(e_w_in.reshape(-1, eo[11]), E_MAIN + LANES,
                     [(0, eo[7], 0), (eo[8], eo[10] - eo[8], eo[7]), (eo[7], eo[8] - eo[7], E_MAIN),
                      (eo[10], eo[11] - eo[10], E_MAIN + eo[8] - eo[7])])
    assert e_small <= LANES
    o_in = cast_cols(o_w_in.reshape(-1, oo[5]), O_MAIN + LANES,
                     [(0, oo[3], 0), (oo[4], oo[5] - oo[4], oo[3]), (oo[3], oo[4] - oo[3], O_MAIN)])
    e_out, o_out = cast_bf16(e_w_out), cast_bf16(o_w_out)
    ffn_g, ffn_u, ffn_d = cast_bf16(e_ffn_w_gate), cast_bf16(e_ffn_w_up), cast_bf16(e_ffn_w_down)
    moe_g, moe_u = cast_bf16(o_moe_w_gate, tr=1024), cast_bf16(o_moe_w_up, tr=1024)
    moe_d = cast_bf16(o_moe_w_down)
    xf = x.reshape(T, D)
    h = modulate(xf, layer_mods[0][1], layer_mods[0][0], S)
    for i in range(DEPTH):
        j = i // 2
        sh1, sc1, g1, sh2, sc2, g2 = layer_mods[i]
        if i % 2 == 0:
            proj_main = matmul(h, e_in, jnp.bfloat16, TM, 1024, n=E_MAIN, w_row0=j * D)
            proj_small = matmul(h, e_in, jnp.float32, TM, LANES, n=LANES, w_row0=j * D, w_col0=E_MAIN)
            mix = _nsa_ssd_heads(proj_main, proj_small, positions, e_cmp_pos[j], e_cmp_w1[j], e_cmp_w2[j],
                                 e_conv_w[j], e_conv_b[j], e_dt_bias[j], e_a_log[j], e_d_skip[j], e_ssm_norm_w[j])
            y = matmul(mix, e_out, jnp.float32, TM, 1024, w_row0=j * MIX_DIM)
            xf, h = residual_layer_norm(xf, y, g1, e_ln1_g[j], e_ln1_b[j], sc2, sh2, S)
            hid = swiglu_up(h, ffn_g, ffn_u, 2 * TM, FFN_TN, w_row0=j * D)
            y = matmul(hid, ffn_d, jnp.float32, TM, 1024, tk=D_FF // 2, w_row0=j * D_FF)
            ln_g, ln_b = e_ln2_g[j], e_ln2_b[j]
        else:
            proj_main = matmul(h, o_in, jnp.bfloat16, TM, 1024, n=O_MAIN, w_row0=j * D)
            proj_small = matmul(h, o_in, jnp.float32, TM, LANES, n=LANES, w_row0=j * D, w_col0=O_MAIN)
            mix = _fox_conformer_heads(proj_main, proj_small, B, S, o_fox_f_bias[j], o_conf_conv_w[j],
                                       o_conf_conv_b[j], o_conf_ln_g[j], o_conf_ln_b[j])
            y = matmul(mix, o_out, jnp.float32, TM, 1024, w_row0=j * MIX_DIM)
            xf, h = residual_layer_norm(xf, y, g1, o_ln1_g[j], o_ln1_b[j], sc2, sh2, S)
            router = _bf16(jnp.pad(o_moe_router[j], ((0, 0), (0, LANES - N_EXPERTS))))
            gates_wide = moe_router(h, router, N_EXPERTS)
            hid = expert_swiglu_up(h, moe_g, moe_u, gates_wide, N_EXPERTS, j * N_EXPERTS, TM)
            y = matmul(hid, moe_d, jnp.float32, TM, 1024, tk=2 * D_EXPERT, w_row0=j * N_EXPERTS * D_EXPERT)
            ln_g, ln_b = o_ln2_g[j], o_ln2_b[j]
        if i + 1 < DEPTH:
            nsh, nsc = layer_mods[i + 1][0], layer_mods[i + 1][1]
        else:
            nsh, nsc = sh2, sc2
        xf, h = residual_layer_norm(xf, y, g2, ln_g, ln_b, nsc, nsh, S)
    return xf.reshape(B, S, D)
```

```python
import functools

import jax
import jax.numpy as jnp
import numpy as np
from jax import lax
from jax.experimental import pallas as pl
from jax.experimental.pallas import tpu as pltpu

D_MODEL = 4096
DEPTH = 4
HEAD_DIM = 128
MIX_DIM = D_MODEL

NSA_HEADS = 16
NSA_KV_HEADS = 4
NSA_REP = NSA_HEADS // NSA_KV_HEADS
NSA_Q_DIM = NSA_HEADS * HEAD_DIM
NSA_KV_DIM = NSA_KV_HEADS * HEAD_DIM
CMP_BLOCK = 32
CMP_STRIDE = 16
SLC_BLOCK = 64
N_SELECT = 16
WINDOW = 512
ROPE_THETA = 10000.0

SSM_DIM = MIX_DIM - NSA_Q_DIM
SSM_HEADDIM = 64
SSM_HEADS = SSM_DIM // SSM_HEADDIM
SSM_GROUPS = 8
SSM_STATE = 128
SSM_CONV = 4
SSM_CHUNK = 128
SSM_CONV_DIM = SSM_DIM + 2 * SSM_GROUPS * SSM_STATE

FOX_HEADS = 16
FOX_DIM = FOX_HEADS * HEAD_DIM

CONF_DIM = MIX_DIM - FOX_DIM
CONF_KERNEL = 31

D_FF = 11008
N_EXPERTS = 8
TOP_K = 2
D_EXPERT = 1408

DEEPNORM_ALPHA = (2.0 * DEPTH) ** 0.25
LN_EPS = 1e-5

E_SPLIT_SIZES = (NSA_Q_DIM,) + (NSA_KV_DIM,) * 6 + (3 * NSA_HEADS, SSM_DIM, SSM_CONV_DIM, SSM_HEADS)
O_SPLIT_SIZES = (FOX_DIM, FOX_DIM, FOX_DIM, FOX_HEADS, 2 * CONF_DIM)

LANES = 128
VMEM_LIMIT_BYTES = 56 * 1024 * 1024


def _round_up(n, m):
    return (n + m - 1) // m * m


def _params(*sem):
    return pltpu.CompilerParams(dimension_semantics=sem, vmem_limit_bytes=VMEM_LIMIT_BYTES)


def _cast_cols_kernel(w_ref, o_ref, *, segments, fill):
    if fill:
        o_ref[...] = jnp.zeros_like(o_ref)
    for src, width, dst in segments:
        o_ref[:, dst:dst + width] = w_ref[0, :, src:src + width].astype(o_ref.dtype)


def cast_cols(w, out_cols, segments, tr=256):
    layers, rows, cols = w.shape
    assert rows % tr == 0
    per_layer = rows // tr
    fill = sum(width for _, width, _ in segments) != out_cols
    return pl.pallas_call(
        functools.partial(_cast_cols_kernel, segments=tuple(segments), fill=fill),
        grid=(layers, per_layer),
        in_specs=[pl.BlockSpec((1, tr, cols), lambda l, i: (l, i, 0))],
        out_specs=pl.BlockSpec((tr, out_cols), lambda l, i: (l * per_layer + i, 0)),
        out_shape=jax.ShapeDtypeStruct((layers * rows, out_cols), jnp.bfloat16),
        compiler_params=_params("parallel", "parallel"),
        name="cast_cols",
    )(w)


def cast_bf16(w, tr=256):
    w3 = w.reshape((-1,) + w.shape[-2:])
    return cast_cols(w3, w3.shape[2], [(0, w3.shape[2], 0)], tr)


def _matmul_kernel(x_ref, w_ref, o_ref):
    o_ref[...] = jnp.dot(x_ref[...], w_ref[...], preferred_element_type=jnp.float32).astype(o_ref.dtype)


def _matmul_acc_kernel(x_ref, w_ref, o_ref, acc_ref):
    k = pl.program_id(2)

    @pl.when(k == 0)
    def _():
        acc_ref[...] = jnp.zeros_like(acc_ref)

    acc_ref[...] += jnp.dot(x_ref[...], w_ref[...], preferred_element_type=jnp.float32)

    @pl.when(k == pl.num_programs(2) - 1)
    def _():
        o_ref[...] = acc_ref[...].astype(o_ref.dtype)


def matmul(x, w, out_dtype, tm, tn, tk=None, n=None, w_row0=0, w_col0=0):
    m, k = x.shape
    n = w.shape[1] if n is None else n
    tk = k if tk is None else tk
    assert m % tm == 0 and n % tn == 0 and k % tk == 0 and w_row0 % tk == 0 and w_col0 % tn == 0
    r0, c0 = w_row0 // tk, w_col0 // tn
    if tk == k:
        return pl.pallas_call(
            _matmul_kernel,
            grid=(n // tn, m // tm),
            in_specs=[pl.BlockSpec((tm, k), lambda j, i: (i, 0)),
                      pl.BlockSpec((k, tn), lambda j, i: (r0, c0 + j))],
            out_specs=pl.BlockSpec((tm, tn), lambda j, i: (i, j)),
            out_shape=jax.ShapeDtypeStruct((m, n), out_dtype),
            compiler_params=_params("parallel", "parallel"),
            name="matmul",
        )(x, w)
    return pl.pallas_call(
        _matmul_acc_kernel,
        grid=(n // tn, m // tm, k // tk),
        in_specs=[pl.BlockSpec((tm, tk), lambda j, i, kk: (i, kk)),
                  pl.BlockSpec((tk, tn), lambda j, i, kk: (r0 + kk, c0 + j))],
        out_specs=pl.BlockSpec((tm, tn), lambda j, i, kk: (i, j)),
        out_shape=jax.ShapeDtypeStruct((m, n), out_dtype),
        scratch_shapes=[pltpu.VMEM((tm, tn), jnp.float32)],
        compiler_params=_params("parallel", "parallel", "arbitrary"),
        name="matmul_acc",
    )(x, w)


def _swiglu_up_kernel(*refs, gated):
    if gated:
        x_ref, wg_ref, wu_ref, gate_ref, o_ref = refs
    else:
        x_ref, wg_ref, wu_ref, o_ref = refs
    x = x_ref[...]
    g = jnp.dot(x, wg_ref[...], preferred_element_type=jnp.float32)
    u = jnp.dot(x, wu_ref[...], preferred_element_type=jnp.float32)
    hid = g * jax.nn.sigmoid(g) * u
    if gated:
        hid = hid * gate_ref[:, :1]
    o_ref[...] = hid.astype(o_ref.dtype)


def swiglu_up(x, wg, wu, tm, tn, w_row0=0):
    m, k = x.shape
    n = wg.shape[1]
    assert m % tm == 0 and n % tn == 0 and w_row0 % k == 0
    r0 = w_row0 // k
    return pl.pallas_call(
        functools.partial(_swiglu_up_kernel, gated=False),
        grid=(m // tm, n // tn),
        in_specs=[pl.BlockSpec((tm, k), lambda i, j: (i, 0)),
                  pl.BlockSpec((k, tn), lambda i, j: (r0, j)),
                  pl.BlockSpec((k, tn), lambda i, j: (r0, j))],
        out_specs=pl.BlockSpec((tm, tn), lambda i, j: (i, j)),
        out_shape=jax.ShapeDtypeStruct((m, n), jnp.bfloat16),
        compiler_params=_params("parallel", "parallel"),
        name="swiglu_up",
    )(x, wg, wu)


def expert_swiglu_up(x, wg, wu, gates_wide, n_experts, expert0, tm):
    m, k = x.shape
    d_e = wg.shape[1]
    return pl.pallas_call(
        functools.partial(_swiglu_up_kernel, gated=True),
        grid=(n_experts, m // tm),
        in_specs=[pl.BlockSpec((tm, k), lambda e, i: (i, 0)),
                  pl.BlockSpec((k, d_e), lambda e, i: (expert0 + e, 0), pipeline_mode=pl.Buffered(1)),
                  pl.BlockSpec((k, d_e), lambda e, i: (expert0 + e, 0), pipeline_mode=pl.Buffered(1)),
                  pl.BlockSpec((tm, LANES), lambda e, i: (i, e))],
        out_specs=pl.BlockSpec((tm, d_e), lambda e, i: (i, e)),
        out_shape=jax.ShapeDtypeStruct((m, n_experts * d_e), jnp.bfloat16),
        compiler_params=_params("parallel", "parallel"),
        name="expert_swiglu_up",
    )(x, wg, wu, gates_wide)


def _router_kernel(x_ref, w_ref, o_ref, *, n_experts):
    logits = jnp.dot(x_ref[...], w_ref[...], preferred_element_type=jnp.float32)
    lane = lax.broadcasted_iota(jnp.int32, logits.shape, 1)
    logits = jnp.where(lane < n_experts, logits, -jnp.inf)
    e = jnp.exp(logits - jnp.max(logits, axis=1, keepdims=True))
    p = e / jnp.sum(e, axis=1, keepdims=True)
    big = jnp.int32(LANES)
    v1 = jnp.max(p, axis=1, keepdims=True)
    i1 = jnp.min(jnp.where(p == v1, lane, big), axis=1, keepdims=True)
    p2 = jnp.where(lane == i1, -1.0, p)
    v2 = jnp.max(p2, axis=1, keepdims=True)
    i2 = jnp.min(jnp.where(p2 == v2, lane, big), axis=1, keepdims=True)
    denom = v1 + v2
    for ex in range(n_experts):
        g = jnp.where(i1 == ex, v1 / denom, jnp.where(i2 == ex, v2 / denom, 0.0))
        o_ref[:, ex * LANES:(ex + 1) * LANES] = jnp.broadcast_to(g, (g.shape[0], LANES))


def moe_router(x, router_padded, n_experts, tm=512):
    m, k = x.shape
    return pl.pallas_call(
        functools.partial(_router_kernel, n_experts=n_experts),
        grid=(m // tm,),
        in_specs=[pl.BlockSpec((tm, k), lambda i: (i, 0)),
                  pl.BlockSpec((k, LANES), lambda i: (0, 0))],
        out_specs=pl.BlockSpec((tm, n_experts * LANES), lambda i: (i, 0)),
        out_shape=jax.ShapeDtypeStruct((m, n_experts * LANES), jnp.float32),
        compiler_params=_params("parallel"),
        name="moe_router",
    )(x, router_padded)


def _ada_kernel(c_ref, w_ref, b_ref, o_ref):
    c = c_ref[...]
    a = c * jax.nn.sigmoid(c)
    o_ref[...] = jnp.dot(a, w_ref[...], preferred_element_type=jnp.float32,
                         precision=lax.Precision.HIGHEST) + b_ref[...]


def ada_projection(c, w_ada, b_ada):
    b, d = c.shape
    n = w_ada.shape[1]
    rows = _round_up(b, 8)
    c_pad = jnp.pad(c, ((0, rows - b), (0, 0)))
    tn = 512
    out = pl.pallas_call(
        _ada_kernel,
        grid=(n // tn,),
        in_specs=[pl.BlockSpec((rows, d), lambda j: (0, 0)),
                  pl.BlockSpec((d, tn), lambda j: (0, j)),
                  pl.BlockSpec((1, tn), lambda j: (0, j))],
        out_specs=pl.BlockSpec((rows, tn), lambda j: (0, j)),
        out_shape=jax.ShapeDtypeStruct((rows, n), jnp.float32),
        compiler_params=_params("parallel"),
        name="ada_projection",
    )(c_pad, w_ada, b_ada.reshape(1, n))
    return out[:b]


def _modulate_kernel(x_ref, sc_ref, sh_ref, h_ref):
    h_ref[...] = (x_ref[...] * (1.0 + sc_ref[0]) + sh_ref[0]).astype(h_ref.dtype)


def modulate(x, sc, sh, seq, tr=256):
    t, d = x.shape
    per_b = seq // tr
    vec = pl.BlockSpec((1, 1, d), lambda i: (i // per_b, 0, 0))
    return pl.pallas_call(
        _modulate_kernel,
        grid=(t // tr,),
        in_specs=[pl.BlockSpec((tr, d), lambda i: (i, 0)), vec, vec],
        out_specs=pl.BlockSpec((tr, d), lambda i: (i, 0)),
        out_shape=jax.ShapeDtypeStruct((t, d), jnp.bfloat16),
        compiler_params=_params("parallel"),
        name="modulate",
    )(x, sc, sh)


def _res_ln_kernel(x_ref, y_ref, gate_ref, g_ref, b_ref, sc_ref, sh_ref, xn_ref, h_ref):
    v = DEEPNORM_ALPHA * x_ref[...] + (1.0 + gate_ref[0]) * y_ref[...].astype(jnp.float32)
    mu = jnp.mean(v, axis=-1, keepdims=True)
    vc = v - mu
    var = jnp.mean(vc * vc, axis=-1, keepdims=True)
    xn = vc * lax.rsqrt(var + LN_EPS) * g_ref[...] + b_ref[...]
    xn_ref[...] = xn
    h_ref[...] = (xn * (1.0 + sc_ref[0]) + sh_ref[0]).astype(h_ref.dtype)


def residual_layer_norm(x, y, gate, ln_g, ln_b, sc_next, sh_next, seq, tr=256):
    t, d = x.shape
    per_b = seq // tr
    row = pl.BlockSpec((tr, d), lambda i: (i, 0))
    vec_b = pl.BlockSpec((1, 1, d), lambda i: (i // per_b, 0, 0))
    vec = pl.BlockSpec((1, d), lambda i: (0, 0))
    return pl.pallas_call(
        _res_ln_kernel,
        grid=(t // tr,),
        in_specs=[row, row, vec_b, vec, vec, vec_b, vec_b],
        out_specs=[row, row],
        out_shape=[jax.ShapeDtypeStruct((t, d), jnp.float32), jax.ShapeDtypeStruct((t, d), jnp.bfloat16)],
        compiler_params=_params("parallel"),
        name="residual_layer_norm",
    )(x, y, gate, ln_g.reshape(1, d), ln_b.reshape(1, d), sc_next, sh_next)


def _rope_kernel(pos_ref, inv_ref, sign_ref, q_ref, k1_ref, k2_ref, qo_ref, k1o_ref, k2o_ref, *, q_scale):
    ang = pos_ref[...].astype(jnp.float32) * inv_ref[...]
    cos = jnp.cos(ang)
    sin = jnp.sin(ang) * sign_ref[...]

    def rot(x_ref, o_ref, scale):
        for h in range(x_ref.shape[1] // HEAD_DIM):
            x = x_ref[:, h * HEAD_DIM:(h + 1) * HEAD_DIM].astype(jnp.float32)
            y = x * cos + pltpu.roll(x, HEAD_DIM // 2, 1) * sin
            o_ref[:, h * HEAD_DIM:(h + 1) * HEAD_DIM] = (y * scale).astype(o_ref.dtype)

    rot(q_ref, qo_ref, q_scale)
    rot(k1_ref, k1o_ref, 1.0)
    rot(k2_ref, k2o_ref, 1.0)


def rope_qkk(proj, pos_col, q_cols, k1_cols, k2_cols, tr=512):
    t = proj.shape[0]
    half = HEAD_DIM // 2
    inv = ROPE_THETA ** (-jnp.arange(half, dtype=jnp.float32) / half)
    inv = jnp.concatenate([inv, inv]).reshape(1, HEAD_DIM)
    sign = jnp.concatenate([-jnp.ones((half,), jnp.float32), jnp.ones((half,), jnp.float32)]).reshape(1, HEAD_DIM)

    def cols(c):
        start, width = c
        assert start % width == 0
        return pl.BlockSpec((tr, width), lambda i: (i, start // width))

    def out(c):
        return pl.BlockSpec((tr, c[1]), lambda i: (i, 0))

    vec = pl.BlockSpec((1, HEAD_DIM), lambda i: (0, 0))
    return pl.pallas_call(
        functools.partial(_rope_kernel, q_scale=HEAD_DIM ** -0.5),
        grid=(t // tr,),
        in_specs=[pl.BlockSpec((tr, 1), lambda i: (i, 0)), vec, vec, cols(q_cols), cols(k1_cols), cols(k2_cols)],
        out_specs=[out(q_cols), out(k1_cols), out(k2_cols)],
        out_shape=[jax.ShapeDtypeStruct((t, c[1]), jnp.bfloat16) for c in (q_cols, k1_cols, k2_cols)],
        compiler_params=_params("parallel"),
        name="rope_qkk",
    )(pos_col, inv, sign, proj, proj, proj)


MASK_VALUE = -1e30
ATT_TILE = 512


def _flash_kernel(qi_ref, kt_ref, first_ref, last_ref, *refs, mode, heads, shared_kv, tile, q_scale):
    if mode == "fox":
        q_ref, k_ref, v_ref, f_ref, o_ref, m_sc, l_sc, acc_sc = refs
    elif mode == "sel":
        q_ref, k_ref, v_ref, sel_ref, o_ref, m_sc, l_sc, acc_sc = refs
    else:
        q_ref, k_ref, v_ref, o_ref, m_sc, l_sc, acc_sc = refs
    step_id = pl.program_id(2)
    qi = qi_ref[step_id]
    kt = kt_ref[step_id]

    @pl.when(first_ref[step_id] == 1)
    def _():
        m_sc[...] = jnp.full_like(m_sc, MASK_VALUE)
        l_sc[...] = jnp.zeros_like(l_sc)
        acc_sc[...] = jnp.zeros_like(acc_sc)

    def step(masked):
        mask = None
        if masked:
            row = qi * tile + lax.broadcasted_iota(jnp.int32, (tile, tile), 0)
            col = kt * tile + lax.broadcasted_iota(jnp.int32, (tile, tile), 1)
            mask = col <= row
            if mode == "win":
                mask = mask & (row - col < WINDOW)
            if mode == "sel":
                n_blk = sel_ref.shape[-1]
                blk = lax.broadcasted_iota(jnp.int32, (n_blk, tile), 0)
                key_blk = (kt * tile + lax.broadcasted_iota(jnp.int32, (n_blk, tile), 1)) // SLC_BLOCK
                expand = jnp.where(blk == key_blk, 1.0, 0.0).astype(jnp.bfloat16)
                member = jnp.dot(sel_ref[0, 0], expand, preferred_element_type=jnp.float32)
                mask = mask & (member > 0.5)
        for r in range(heads):
            q = q_ref[:, r * HEAD_DIM:(r + 1) * HEAD_DIM]
            if q_scale != 1.0:
                q = (q.astype(jnp.float32) * q_scale).astype(q.dtype)
            kv = 0 if shared_kv else r
            k = k_ref[:, kv * HEAD_DIM:(kv + 1) * HEAD_DIM]
            v = v_ref[:, kv * HEAD_DIM:(kv + 1) * HEAD_DIM]
            s = lax.dot_general(q, k, (((1,), (1,)), ((), ())), preferred_element_type=jnp.float32)
            if mode == "fox":
                s = s - f_ref[0, r]
            if masked:
                s = jnp.where(mask, s, MASK_VALUE)
            m_prev = m_sc[r]
            m_new = jnp.maximum(m_prev, jnp.max(s, axis=1, keepdims=True))
            alpha = jnp.exp(m_prev - m_new)
            p = jnp.exp(s - m_new[:, :1])
            if masked:
                p = jnp.where(mask, p, 0.0)
            l_sc[r] = alpha * l_sc[r] + jnp.sum(p, axis=1, keepdims=True)
            acc_sc[r] = alpha * acc_sc[r] + jnp.dot(p.astype(v.dtype), v, preferred_element_type=jnp.float32)
            m_sc[r] = m_new

    if mode == "fox":
        @pl.when(kt < qi)
        def _():
            step(False)

        @pl.when(kt == qi)
        def _():
            step(True)
    else:
        step(True)

    @pl.when(last_ref[step_id] == 1)
    def _():
        for r in range(heads):
            inv_l = 1.0 / l_sc[r]
            o_ref[:, r * HEAD_DIM:(r + 1) * HEAD_DIM] = (acc_sc[r] * inv_l).astype(o_ref.dtype)


def flash_attention(mode, q_arr, q_col0, k_arr, k_col0, v_arr, v_col0, batch, seq, n_groups, heads,
                    shared_kv, extra=None, q_scale=1.0, tile=ATT_TILE):
    t = batch * seq
    nq = seq // tile
    qw = heads * HEAD_DIM
    kvw = HEAD_DIM if shared_kv else qw
    assert q_col0 % qw == 0 and k_col0 % kvw == 0 and v_col0 % kvw == 0 and seq % tile == 0
    reach = WINDOW // tile if mode == "win" else nq
    pairs = [(qi, kt) for qi in range(nq) for kt in range(max(0, qi - reach), qi + 1)]
    tables = [jnp.asarray(col, jnp.int32) for col in (
        [qi for qi, _ in pairs], [kt for _, kt in pairs],
        [int(kt == max(0, qi - reach)) for qi, kt in pairs], [int(kt == qi) for qi, kt in pairs])]

    in_specs = [
        pl.BlockSpec((tile, qw), lambda b, g, s, qi, kt, fi, la: (b * nq + qi[s], q_col0 // qw + g)),
        pl.BlockSpec((tile, kvw), lambda b, g, s, qi, kt, fi, la: (b * nq + kt[s], k_col0 // kvw + g)),
        pl.BlockSpec((tile, kvw), lambda b, g, s, qi, kt, fi, la: (b * nq + kt[s], v_col0 // kvw + g)),
    ]
    args = [q_arr, k_arr, v_arr]
    if mode == "sel":
        in_specs.append(pl.BlockSpec((1, 1, tile, extra.shape[-1]),
                                     lambda b, g, s, qi, kt, fi, la: (b, g, qi[s], 0)))
        args.append(extra)
    elif mode == "fox":
        in_specs.append(pl.BlockSpec((1, heads, 1, tile), lambda b, g, s, qi, kt, fi, la: (b, g, 0, kt[s])))
        args.append(extra)
    stat = pltpu.VMEM((heads, tile, HEAD_DIM), jnp.float32)
    return pl.pallas_call(
        functools.partial(_flash_kernel, mode=mode, heads=heads, shared_kv=shared_kv, tile=tile, q_scale=q_scale),
        grid_spec=pltpu.PrefetchScalarGridSpec(
            num_scalar_prefetch=len(tables),
            grid=(batch, n_groups, len(pairs)),
            in_specs=in_specs,
            out_specs=pl.BlockSpec((tile, qw), lambda b, g, s, qi, kt, fi, la: (b * nq + qi[s], g)),
            scratch_shapes=[stat, stat, stat]),
        out_shape=jax.ShapeDtypeStruct((t, n_groups * qw), jnp.bfloat16),
        compiler_params=_params("parallel", "parallel", "arbitrary"),
        name="flash_" + mode,
    )(*tables, *args)


SSD_COL_TILE = 1024
HALO_ROWS = 8


def _cumsum_rows(x):
    n = x.shape[0]
    row = lax.broadcasted_iota(jnp.int32, x.shape, 0)
    k = 1
    while k < n:
        x = x + jnp.where(row >= k, pltpu.roll(x, k, 0), 0.0)
        k *= 2
    return x


def _ssd_kernel(xs0_ref, xs1_ref, bm_ref, cm_ref, z0_ref, z1_ref, small_ref, conv_w_ref, conv_b_ref,
                dt_bias_ref, a_log_ref, d_skip_ref, norm_w_ref, o_ref, buf_sc, act_sc, state_sc, *, dt_lane0):
    L, P, N = SSM_CHUNK, SSM_HEADDIM, SSM_STATE
    heads_per_group = SSM_HEADS // SSM_GROUPS
    c = pl.program_id(1)

    @pl.when(c == 0)
    def _():
        buf_sc[0:HALO_ROWS, :] = jnp.zeros((HALO_ROWS, SSM_CONV_DIM), jnp.float32)
        state_sc[...] = jnp.zeros_like(state_sc)

    srcs = (xs0_ref, xs1_ref, bm_ref, cm_ref)
    cw = 512
    for cb in range(SSM_CONV_DIM // cw):
        src = srcs[cb * cw // SSD_COL_TILE]
        off = cb * cw % SSD_COL_TILE
        cols = slice(cb * cw, (cb + 1) * cw)
        buf_sc[HALO_ROWS:HALO_ROWS + L, cols] = src[:, off:off + cw].astype(jnp.float32)
        acc = jnp.broadcast_to(conv_b_ref[:, cols], (L, cw))
        for k in range(SSM_CONV):
            start = HALO_ROWS - (SSM_CONV - 1) + k
            acc = acc + conv_w_ref[k:k + 1, cols] * buf_sc[start:start + L, cols]
        act_sc[:, cols] = acc * jax.nn.sigmoid(acc)
        buf_sc[0:HALO_ROWS, cols] = buf_sc[L:L + HALO_ROWS, cols]

    pre = small_ref[...] + dt_bias_ref[...]
    dt = jnp.maximum(pre, 0.0) + jnp.log(1.0 + jnp.exp(-jnp.abs(pre)))
    a_cum = _cumsum_rows(dt * (-jnp.exp(a_log_ref[...])))
    a_cum_t = a_cum.T
    a_last = a_cum[L - 1:L, :]
    tril = lax.broadcasted_iota(jnp.int32, (L, L), 0) >= lax.broadcasted_iota(jnp.int32, (L, L), 1)

    for g in range(SSM_GROUPS):
        bc = act_sc[:, SSM_DIM + g * N:SSM_DIM + (g + 1) * N]
        cc = act_sc[:, SSM_DIM + SSM_GROUPS * N + g * N:SSM_DIM + SSM_GROUPS * N + (g + 1) * N].astype(jnp.bfloat16)
        cb_mat = lax.dot_general(cc, bc.astype(jnp.bfloat16), (((1,), (1,)), ((), ())),
                                 preferred_element_type=jnp.float32)
        bc_t = bc.T.astype(jnp.bfloat16)
        yz = []
        ss = jnp.zeros((L, 1), jnp.float32)
        for e in range(heads_per_group):
            h = g * heads_per_group + e
            lane = dt_lane0 + h
            dt_col = dt[:, lane:lane + 1]
            ac_col = a_cum[:, lane:lane + 1]
            ac_row = a_cum_t[lane:lane + 1, :]
            last = a_last[:, lane:lane + 1]
            decay = jnp.where(tril, jnp.exp(jnp.minimum(ac_col - ac_row, 0.0)), 0.0)
            xs = act_sc[:, h * P:(h + 1) * P]
            xd = xs * dt_col
            y = jnp.dot((cb_mat * decay).astype(jnp.bfloat16), xd.astype(jnp.bfloat16),
                        preferred_element_type=jnp.float32)
            st = state_sc[h]
            y = y + jnp.dot(cc, st.astype(jnp.bfloat16), preferred_element_type=jnp.float32) * jnp.exp(ac_col)
            xdd = (xd * jnp.exp(last - ac_col)).astype(jnp.bfloat16)
            state_sc[h] = jnp.exp(last) * st + jnp.dot(bc_t, xdd, preferred_element_type=jnp.float32)
            y = y + d_skip_ref[:, h:h + 1] * xs
            z_ref = z0_ref if h * P < SSD_COL_TILE else z1_ref
            zc = h * P % SSD_COL_TILE
            z = z_ref[:, zc:zc + P].astype(jnp.float32)
            y = y * (z * jax.nn.sigmoid(z))
            ss = ss + jnp.sum(y * y, axis=1, keepdims=True)
            yz.append(y)
        scale = lax.rsqrt(ss / (heads_per_group * P) + LN_EPS)
        for e in range(heads_per_group):
            h = g * heads_per_group + e
            o_ref[:, h * P:(h + 1) * P] = (yz[e] * scale * norm_w_ref[:, h * P:(h + 1) * P]).astype(o_ref.dtype)


def ssd_mixer(proj_main, proj_small, batch, seq, xbc_col0, z_col0, dt_lane0, conv_w, conv_b, dt_bias, a_log,
              d_skip, norm_w):
    t = batch * seq
    nc = seq // SSM_CHUNK
    ct = SSD_COL_TILE
    assert xbc_col0 % ct == 0 and z_col0 % ct == 0 and SSM_DIM == 2 * ct and SSM_GROUPS * SSM_STATE == ct

    def view(col0):
        return pl.BlockSpec((SSM_CHUNK, ct), lambda b, c: (b * nc + c, col0 // ct))

    def lane_row(v):
        return jnp.pad(v.astype(jnp.float32), (dt_lane0, LANES - dt_lane0 - v.shape[0])).reshape(1, LANES)

    full = lambda a: pl.BlockSpec(a.shape, lambda b, c: (0, 0))
    consts = [conv_w, conv_b.reshape(1, -1), lane_row(dt_bias), lane_row(a_log),
              jnp.pad(d_skip, (0, LANES - SSM_HEADS)).reshape(1, LANES), norm_w.reshape(1, -1)]
    return pl.pallas_call(
        functools.partial(_ssd_kernel, dt_lane0=dt_lane0),
        grid=(batch, nc),
        in_specs=[view(xbc_col0), view(xbc_col0 + ct), view(xbc_col0 + 2 * ct), view(xbc_col0 + 3 * ct),
                  view(z_col0), view(z_col0 + ct),
                  pl.BlockSpec((SSM_CHUNK, LANES), lambda b, c: (b * nc + c, 0))] + [full(a) for a in consts],
        out_specs=pl.BlockSpec((SSM_CHUNK, SSM_DIM), lambda b, c: (b * nc + c, 0)),
        out_shape=jax.ShapeDtypeStruct((t, SSM_DIM), jnp.bfloat16),
        scratch_shapes=[pltpu.VMEM((SSM_CHUNK + HALO_ROWS, SSM_CONV_DIM), jnp.float32),
                        pltpu.VMEM((SSM_CHUNK, SSM_CONV_DIM), jnp.float32),
                        pltpu.VMEM((SSM_HEADS, SSM_STATE, SSM_HEADDIM), jnp.float32)],
        compiler_params=_params("parallel", "arbitrary"),
        name="ssd_mixer",
    )(proj_main, proj_main, proj_main, proj_main, proj_main, proj_main, proj_small, *consts)


BIG = 3.0e38


def _rope_rows(x, pos_col, inv_row, sign_row):
    ang = pos_col.astype(jnp.float32) * inv_row
    return x * jnp.cos(ang) + pltpu.roll(x, HEAD_DIM // 2, 1) * (jnp.sin(ang) * sign_row)


def _gelu_tanh(x):
    return 0.5 * x * (1.0 + jnp.tanh(0.7978845608028654 * (x + 0.044715 * x * x * x)))


def _compress_kernel(kc_ref, vc_ref, pos_ref, emb_ref, w1_ref, w2_ref, inv_ref, sign_ref, ko_ref, vo_ref, x_sc):
    n = ko_ref.shape[2]
    half = CMP_BLOCK // 2
    assert half == CMP_STRIDE
    for which, src, dst in ((0, kc_ref, ko_ref), (1, vc_ref, vo_ref)):
        x_sc[...] = src[...].astype(jnp.float32)
        lo = jnp.zeros((n, HEAD_DIM), jnp.float32)
        hi = jnp.zeros((n, HEAD_DIM), jnp.float32)
        for p in range(half):
            xp = x_sc[pl.ds(p, n, stride=CMP_STRIDE), :]
            for part, tok in ((0, p), (1, half + p)):
                xb = (xp + emb_ref[which, tok:tok + 1, :]).astype(jnp.bfloat16)
                term = jnp.dot(xb, w1_ref[which, tok * HEAD_DIM:(tok + 1) * HEAD_DIM, :],
                               preferred_element_type=jnp.float32)
                if part == 0:
                    lo = lo + term
                else:
                    hi = hi + term
        pre = lo + pltpu.roll(hi, n - 1, 0)
        out = jnp.dot(_gelu_tanh(pre).astype(jnp.bfloat16), w2_ref[which], preferred_element_type=jnp.float32)
        if which == 0:
            out = _rope_rows(out, pos_ref[0], inv_ref[...], sign_ref[...])
        dst[0, 0] = out.astype(dst.dtype)


def nsa_compress(proj_main, kc_col0, vc_col0, pos_cmp, cmp_pos, cmp_w1, cmp_w2, batch, seq):
    n = seq // CMP_STRIDE
    half = HEAD_DIM // 2
    inv = ROPE_THETA ** (-jnp.arange(half, dtype=jnp.float32) / half)
    inv = jnp.concatenate([inv, inv]).reshape(1, HEAD_DIM)
    sign = jnp.concatenate([-jnp.ones((half,), jnp.float32), jnp.ones((half,), jnp.float32)]).reshape(1, HEAD_DIM)
    w1 = cmp_w1.astype(jnp.bfloat16)
    w2 = cmp_w2.astype(jnp.bfloat16)
    full = lambda a: pl.BlockSpec(a.shape, lambda b, g: (0,) * a.ndim)
    out = jax.ShapeDtypeStruct((batch, NSA_KV_HEADS, n, HEAD_DIM), jnp.bfloat16)
    out_spec = pl.BlockSpec((1, 1, n, HEAD_DIM), lambda b, g: (b, g, 0, 0))
    return pl.pallas_call(
        _compress_kernel,
        grid=(batch, NSA_KV_HEADS),
        in_specs=[pl.BlockSpec((seq, HEAD_DIM), lambda b, g: (b, kc_col0 // HEAD_DIM + g)),
                  pl.BlockSpec((seq, HEAD_DIM), lambda b, g: (b, vc_col0 // HEAD_DIM + g)),
                  pl.BlockSpec((1, n, 1), lambda b, g: (b, 0, 0)),
                  full(cmp_pos), full(w1), full(w2), full(inv), full(sign)],
        out_specs=[out_spec, out_spec],
        out_shape=[out, out],
        scratch_shapes=[pltpu.VMEM((seq, HEAD_DIM), jnp.float32)],
        compiler_params=_params("parallel", "parallel"),
        name="nsa_compress",
    )(proj_main, proj_main, pos_cmp, cmp_pos, w1, w2, inv, sign)


def _softmax_rows_or_zero(s, mask, axis):
    s = jnp.where(mask, s, MASK_VALUE)
    e = jnp.where(mask, jnp.exp(s - jnp.max(s, axis=axis, keepdims=True)), 0.0)
    d = jnp.sum(e, axis=axis, keepdims=True)
    return e / jnp.where(d > 0.0, d, 1.0)


def _cmp_select_kernel(q_ref, k_ref, v_ref, o_ref, sel_ref, *, tq, n_slc):
    qi = pl.program_id(2)
    k = k_ref[0, 0]
    v = v_ref[0, 0]
    n = k.shape[0]
    last_tok = CMP_BLOCK - 1
    t_row = qi * tq + lax.broadcasted_iota(jnp.int32, (tq, n), 0)
    mask = CMP_STRIDE * lax.broadcasted_iota(jnp.int32, (tq, n), 1) + last_tok <= t_row
    t_lane = qi * tq + lax.broadcasted_iota(jnp.int32, (n, tq), 1)
    mask_t = CMP_STRIDE * lax.broadcasted_iota(jnp.int32, (n, tq), 0) + last_tok <= t_lane
    p_sum_t = jnp.zeros((n, tq), jnp.float32)
    for r in range(NSA_REP):
        q = q_ref[:, r * HEAD_DIM:(r + 1) * HEAD_DIM]
        s = lax.dot_general(q, k, (((1,), (1,)), ((), ())), preferred_element_type=jnp.float32)
        p = _softmax_rows_or_zero(s, mask, 1)
        o_ref[:, r * HEAD_DIM:(r + 1) * HEAD_DIM] = jnp.dot(
            p.astype(v.dtype), v, preferred_element_type=jnp.float32).astype(o_ref.dtype)
        s_t = lax.dot_general(k, q, (((1,), (1,)), ((), ())), preferred_element_type=jnp.float32)
        p_sum_t = p_sum_t + _softmax_rows_or_zero(s_t, mask_t, 0)
    blk_n = lax.broadcasted_iota(jnp.int32, (n_slc, n), 0) * SLC_BLOCK
    start_n = lax.broadcasted_iota(jnp.int32, (n_slc, n), 1) * CMP_STRIDE
    overlap_t = jnp.where((start_n < blk_n + SLC_BLOCK) & (start_n + CMP_BLOCK > blk_n), 1.0, 0.0).astype(jnp.bfloat16)
    p_hi = p_sum_t.astype(jnp.bfloat16)
    p_lo = (p_sum_t - p_hi.astype(jnp.float32)).astype(jnp.bfloat16)
    imp = (jnp.dot(overlap_t, p_hi, preferred_element_type=jnp.float32)
           + jnp.dot(overlap_t, p_lo, preferred_element_type=jnp.float32))
    blk = lax.broadcasted_iota(jnp.int32, (n_slc, tq), 0)
    qblk = (qi * tq + lax.broadcasted_iota(jnp.int32, (n_slc, tq), 1)) // SLC_BLOCK
    forced = (blk == 0) | (blk == qblk) | (blk == qblk - 1)
    imp = jnp.where(forced, BIG, jnp.where(blk > qblk, -BIG, imp))
    rank = jnp.zeros((n_slc, tq), jnp.float32)
    for j in range(n_slc):
        other = imp[j:j + 1, :]
        beats = (other > imp) | ((other == imp) & (blk > j))
        rank = rank + jnp.where(beats, 1.0, 0.0)
    member = jnp.where(rank < min(N_SELECT, n_slc), 1.0, 0.0)
    sel_ref[0, 0] = member.T.astype(sel_ref.dtype)


def nsa_compressed_attention(q_rot, k_cmp, v_cmp, batch, seq, tq=256):
    t = batch * seq
    nq = seq // tq
    n_slc = seq // SLC_BLOCK
    n = k_cmp.shape[2]
    qw = NSA_REP * HEAD_DIM
    kv_spec = pl.BlockSpec((1, 1, n, HEAD_DIM), lambda b, g, qi: (b, g, 0, 0))
    return pl.pallas_call(
        functools.partial(_cmp_select_kernel, tq=tq, n_slc=n_slc),
        grid=(batch, NSA_KV_HEADS, nq),
        in_specs=[pl.BlockSpec((tq, qw), lambda b, g, qi: (b * nq + qi, g)), kv_spec, kv_spec],
        out_specs=[pl.BlockSpec((tq, qw), lambda b, g, qi: (b * nq + qi, g)),
                   pl.BlockSpec((1, 1, tq, n_slc), lambda b, g, qi: (b, g, qi, 0))],
        out_shape=[jax.ShapeDtypeStruct((t, NSA_Q_DIM), jnp.bfloat16),
                   jax.ShapeDtypeStruct((batch, NSA_KV_HEADS, seq, n_slc), jnp.bfloat16)],
        compiler_params=_params("parallel", "parallel", "parallel"),
        name="nsa_cmp_select",
    )(q_rot, k_cmp, v_cmp)


def _nsa_combine_kernel(cmp_ref, slc_ref, win_ref, gate_ref, ssm_ref, o_ref):
    g = jax.nn.sigmoid(gate_ref[...])
    for h in range(NSA_HEADS):
        c = slice(h * HEAD_DIM, (h + 1) * HEAD_DIM)
        o = (g[:, 3 * h:3 * h + 1] * cmp_ref[:, c].astype(jnp.float32)
             + g[:, 3 * h + 1:3 * h + 2] * slc_ref[:, c].astype(jnp.float32)
             + g[:, 3 * h + 2:3 * h + 3] * win_ref[:, c].astype(jnp.float32))
        o_ref[:, c] = o.astype(o_ref.dtype)
    o_ref[:, NSA_Q_DIM:] = ssm_ref[...]


def nsa_combine(o_cmp, o_slc, o_win, proj_small, o_ssm, tr=512):
    t = o_cmp.shape[0]
    row = lambda w: pl.BlockSpec((tr, w), lambda i: (i, 0))
    return pl.pallas_call(
        _nsa_combine_kernel,
        grid=(t // tr,),
        in_specs=[row(NSA_Q_DIM), row(NSA_Q_DIM), row(NSA_Q_DIM), row(LANES), row(SSM_DIM)],
        out_specs=row(MIX_DIM),
        out_shape=jax.ShapeDtypeStruct((t, MIX_DIM), jnp.bfloat16),
        compiler_params=_params("parallel"),
        name="nsa_combine",
    )(o_cmp, o_slc, o_win, proj_small, o_ssm)


CONF_HALO = 32


def _conformer_kernel(val_ref, gate_ref, fox_ref, w_ref, b_ref, lng_ref, lnb_ref, o_ref, buf_sc, act_sc, *, ts):
    @pl.when(pl.program_id(1) == 0)
    def _():
        buf_sc[0:CONF_HALO, :] = jnp.zeros((CONF_HALO, CONF_DIM), jnp.float32)

    cw = LANES
    total = jnp.zeros((ts, 1), jnp.float32)
    for cb in range(CONF_DIM // cw):
        cols = slice(cb * cw, (cb + 1) * cw)
        gate = gate_ref[:, cols].astype(jnp.float32)
        buf_sc[CONF_HALO:CONF_HALO + ts, cols] = val_ref[:, cols].astype(jnp.float32) * jax.nn.sigmoid(gate)
        acc = jnp.broadcast_to(b_ref[:, cols], (ts, cw))
        for k in range(CONF_KERNEL):
            start = CONF_HALO - (CONF_KERNEL - 1) + k
            acc = acc + w_ref[k:k + 1, cols] * buf_sc[start:start + ts, cols]
        act_sc[:, cols] = acc
        total = total + jnp.sum(acc, axis=1, keepdims=True)
        buf_sc[0:CONF_HALO, cols] = buf_sc[ts:ts + CONF_HALO, cols]
    u = act_sc[...]
    mu = total / CONF_DIM
    uc = u - mu
    var = jnp.mean(uc * uc, axis=1, keepdims=True)
    y = uc * lax.rsqrt(var + LN_EPS) * lng_ref[...] + lnb_ref[...]
    o_ref[:, FOX_DIM:] = (y * jax.nn.sigmoid(y)).astype(o_ref.dtype)
    o_ref[:, :FOX_DIM] = fox_ref[...]


def conformer_mixer(proj_main, glu_col0, o_fox, batch, seq, conv_w, conv_b, ln_g, ln_b, ts=256):
    t = batch * seq
    ns = seq // ts
    assert glu_col0 % CONF_DIM == 0
    full = lambda a: pl.BlockSpec(a.shape, lambda b, s: (0, 0))
    consts = [conv_w, conv_b.reshape(1, -1), ln_g.reshape(1, -1), ln_b.reshape(1, -1)]
    return pl.pallas_call(
        functools.partial(_conformer_kernel, ts=ts),
        grid=(batch, ns),
        in_specs=[pl.BlockSpec((ts, CONF_DIM), lambda b, s: (b * ns + s, glu_col0 // CONF_DIM)),
                  pl.BlockSpec((ts, CONF_DIM), lambda b, s: (b * ns + s, glu_col0 // CONF_DIM + 1)),
                  pl.BlockSpec((ts, FOX_DIM), lambda b, s: (b * ns + s, 0))] + [full(a) for a in consts],
        out_specs=pl.BlockSpec((ts, MIX_DIM), lambda b, s: (b * ns + s, 0)),
        out_shape=jax.ShapeDtypeStruct((t, MIX_DIM), jnp.bfloat16),
        scratch_shapes=[pltpu.VMEM((ts + CONF_HALO, CONF_DIM), jnp.float32),
                        pltpu.VMEM((ts, CONF_DIM), jnp.float32)],
        compiler_params=_params("parallel", "arbitrary"),
        name="conformer_mixer",
    )(proj_main, proj_main, o_fox, *consts)


def _fox_decay_kernel(f_ref, bias_ref, o_ref):
    pre = f_ref[...] + bias_ref[...]
    logf = jnp.minimum(pre, 0.0) - jnp.log(1.0 + jnp.exp(-jnp.abs(pre)))
    o_ref[0] = _cumsum_rows(logf).T[:FOX_HEADS, :]


def fox_decay(proj_small, f_bias, batch, seq):
    bias = jnp.pad(f_bias.astype(jnp.float32), (0, LANES - FOX_HEADS)).reshape(1, LANES)
    out = pl.pallas_call(
        _fox_decay_kernel,
        grid=(batch,),
        in_specs=[pl.BlockSpec((seq, LANES), lambda b: (b, 0)), pl.BlockSpec((1, LANES), lambda b: (0, 0))],
        out_specs=pl.BlockSpec((1, FOX_HEADS, seq), lambda b: (b, 0, 0)),
        out_shape=jax.ShapeDtypeStruct((batch, FOX_HEADS, seq), jnp.float32),
        compiler_params=_params("parallel"),
        name="fox_decay",
    )(proj_small, bias)
    return out.reshape(batch, FOX_HEADS, 1, seq)


TM = 512
FFN_TN = 256
FOX_HEADS_PER_STEP = 4
E_MAIN = NSA_Q_DIM + 6 * NSA_KV_DIM + SSM_DIM + SSM_CONV_DIM
O_MAIN = 3 * FOX_DIM + 2 * CONF_DIM
assert TOP_K == 2


def _bf16(w):
    return w.astype(jnp.bfloat16)


def _col_offsets(sizes):
    return np.concatenate([[0], np.cumsum(sizes)]).tolist()


def _nsa_ssd_heads(proj_main, proj_small, positions, cmp_pos, cmp_w1, cmp_w2, conv_w, conv_b, dt_bias, a_log,
                   d_skip, norm_w):
    B, S = positions.shape
    T = B * S
    G, R = NSA_KV_HEADS, NSA_REP
    kv0 = [NSA_Q_DIM + i * NSA_KV_DIM for i in range(6)]
    z0 = NSA_Q_DIM + 6 * NSA_KV_DIM
    q_rot, ksl_rot, kw_rot = rope_qkk(proj_main, positions.reshape(T, 1), (0, NSA_Q_DIM),
                                      (kv0[2], NSA_KV_DIM), (kv0[4], NSA_KV_DIM))
    n = S // CMP_STRIDE
    pos_cmp = positions[:, CMP_BLOCK - 1::CMP_STRIDE]
    pos_cmp = jnp.pad(pos_cmp, ((0, 0), (0, n - pos_cmp.shape[1]))).reshape(B, n, 1)
    k_cmp, v_cmp = nsa_compress(proj_main, kv0[0], kv0[1], pos_cmp, cmp_pos, cmp_w1, cmp_w2, B, S)
    o_cmp, member = nsa_compressed_attention(q_rot, k_cmp, v_cmp, B, S)
    o_slc = flash_attention("sel", q_rot, 0, ksl_rot, 0, proj_main, kv0[3], B, S, G, R, True, extra=member)
    o_win = flash_attention("win", q_rot, 0, kw_rot, 0, proj_main, kv0[5], B, S, G, R, True)
    o_ssm = ssd_mixer(proj_main, proj_small, B, S, z0 + SSM_DIM, z0, 3 * NSA_HEADS, conv_w, conv_b, dt_bias,
                      a_log, d_skip, norm_w)
    return nsa_combine(o_cmp, o_slc, o_win, proj_small, o_ssm)


def _fox_conformer_heads(proj_main, proj_small, B, S, f_bias, conv_w, conv_b, conf_ln_g, conf_ln_b):
    F = fox_decay(proj_small, f_bias, B, S)
    o_fox = flash_attention("fox", proj_main, 0, proj_main, FOX_DIM, proj_main, 2 * FOX_DIM, B, S,
                            FOX_HEADS // FOX_HEADS_PER_STEP, FOX_HEADS_PER_STEP, False, extra=F,
                            q_scale=HEAD_DIM ** -0.5)
    return conformer_mixer(proj_main, 3 * FOX_DIM, o_fox, B, S, conv_w, conv_b, conf_ln_g, conf_ln_b)


def kernel(x, c, positions, w_ada, b_ada, e_ada_table, e_ln1_g, e_ln1_b, e_ln2_g, e_ln2_b, e_w_in, e_cmp_pos, e_cmp_w1, e_cmp_w2, e_conv_w, e_conv_b, e_dt_bias, e_a_log, e_d_skip, e_ssm_norm_w, e_w_out, e_ffn_w_gate, e_ffn_w_up, e_ffn_w_down, o_ada_table, o_ln1_g, o_ln1_b, o_ln2_g, o_ln2_b, o_w_in, o_fox_f_bias, o_conf_conv_w, o_conf_conv_b, o_conf_ln_g, o_conf_ln_b, o_w_out, o_moe_router, o_moe_w_gate, o_moe_w_up, o_moe_w_down):
    B, S, D = x.shape
    T = B * S
    mod = ada_projection(c, w_ada, b_ada).reshape(B, 6, D)

    def mods(table):
        m = mod + table[None]
        return [m[:, k, None, :] for k in range(6)]

    layer_mods = [mods(e_ada_table[i // 2]) if i % 2 == 0 else mods(o_ada_table[i // 2]) for i in range(DEPTH)]
    eo = _col_offsets(E_SPLIT_SIZES)
    oo = _col_offsets(O_SPLIT_SIZES)
    e_small = eo[8] - eo[7] + eo[11] - eo[10]
    e_in = cast_cols(e_w_in, E_MAIN + LANES,
                     [(0, eo[7], 0), (eo[8], eo[10] - eo[8], eo[7]), (eo[7], eo[8] - eo[7], E_MAIN),
                      (eo[10], eo[11] - eo[10], E_MAIN + eo[8] - eo[7])])
    assert e_small <= LANES
    o_in = cast_cols(o_w_in, O_MAIN + LANES,
                     [(0, oo[3], 0), (oo[4], oo[5] - oo[4], oo[3]), (oo[3], oo[4] - oo[3], O_MAIN)])
    e_out, o_out = cast_bf16(e_w_out), cast_bf16(o_w_out)
    ffn_g, ffn_u, ffn_d = cast_bf16(e_ffn_w_gate), cast_bf16(e_ffn_w_up), cast_bf16(e_ffn_w_down)
    moe_g, moe_u = cast_bf16(o_moe_w_gate, tr=1024), cast_bf16(o_moe_w_up, tr=1024)
    moe_d = cast_bf16(o_moe_w_down, tr=D_EXPERT // 2)
    xf = x.reshape(T, D)
    h = modulate(xf, layer_mods[0][1], layer_mods[0][0], S)
    for i in range(DEPTH):
        j = i // 2
        sh1, sc1, g1, sh2, sc2, g2 = layer_mods[i]
        if i % 2 == 0:
            proj_main = matmul(h, e_in, jnp.bfloat16, TM, 1024, n=E_MAIN, w_row0=j * D)
            proj_small = matmul(h, e_in, jnp.float32, TM, LANES, n=LANES, w_row0=j * D, w_col0=E_MAIN)
            mix = _nsa_ssd_heads(proj_main, proj_small, positions, e_cmp_pos[j], e_cmp_w1[j], e_cmp_w2[j],
                                 e_conv_w[j], e_conv_b[j], e_dt_bias[j], e_a_log[j], e_d_skip[j], e_ssm_norm_w[j])
            y = matmul(mix, e_out, jnp.bfloat16, TM, 1024, w_row0=j * MIX_DIM)
            xf, h = residual_layer_norm(xf, y, g1, e_ln1_g[j], e_ln1_b[j], sc2, sh2, S)
            hid = swiglu_up(h, ffn_g, ffn_u, 2 * TM, FFN_TN, w_row0=j * D)
            y = matmul(hid, ffn_d, jnp.bfloat16, TM, 1024, tk=D_FF // 2, w_row0=j * D_FF)
            ln_g, ln_b = e_ln2_g[j], e_ln2_b[j]
        else:
            proj_main = matmul(h, o_in, jnp.bfloat16, TM, 1024, n=O_MAIN, w_row0=j * D)
            proj_small = matmul(h, o_in, jnp.float32, TM, LANES, n=LANES, w_row0=j * D, w_col0=O_MAIN)
            mix = _fox_conformer_heads(proj_main, proj_small, B, S, o_fox_f_bias[j], o_conf_conv_w[j],
                                       o_conf_conv_b[j], o_conf_ln_g[j], o_conf_ln_b[j])
            y = matmul(mix, o_out, jnp.bfloat16, TM, 1024, w_row0=j * MIX_DIM)
            xf, h = residual_layer_norm(xf, y, g1, o_ln1_g[j], o_ln1_b[j], sc2, sh2, S)
            router = _bf16(jnp.pad(o_moe_router[j], ((0, 0), (0, LANES - N_EXPERTS))))
            gates_wide = moe_router(h, router, N_EXPERTS)
            hid = expert_swiglu_up(h, moe_g, moe_u, gates_wide, N_EXPERTS, j * N_EXPERTS, TM)
            y = matmul(hid, moe_d, jnp.bfloat16, TM, 1024, tk=2 * D_EXPERT, w_row0=j * N_EXPERTS * D_EXPERT)
            ln_g, ln_b = o_ln2_g[j], o_ln2_b[j]
        if i + 1 < DEPTH:
            nsh, nsc = layer_mods[i + 1][0], layer_mods[i + 1][1]
        else:
            nsh, nsc = sh2, sc2
        xf, h = residual_layer_norm(xf, y, g2, ln_g, ln_b, nsc, nsh, S)
    return xf.reshape(B, S, D)
```

```python
import functools

import jax
import jax.numpy as jnp
import numpy as np
from jax import lax
from jax.experimental import pallas as pl
from jax.experimental.pallas import tpu as pltpu

D_MODEL = 4096
DEPTH = 4
HEAD_DIM = 128
MIX_DIM = D_MODEL

NSA_HEADS = 16
NSA_KV_HEADS = 4
NSA_REP = NSA_HEADS // NSA_KV_HEADS
NSA_Q_DIM = NSA_HEADS * HEAD_DIM
NSA_KV_DIM = NSA_KV_HEADS * HEAD_DIM
CMP_BLOCK = 32
CMP_STRIDE = 16
SLC_BLOCK = 64
N_SELECT = 16
WINDOW = 512
ROPE_THETA = 10000.0

SSM_DIM = MIX_DIM - NSA_Q_DIM
SSM_HEADDIM = 64
SSM_HEADS = SSM_DIM // SSM_HEADDIM
SSM_GROUPS = 8
SSM_STATE = 128
SSM_CONV = 4
SSM_CHUNK = 128
SSM_CONV_DIM = SSM_DIM + 2 * SSM_GROUPS * SSM_STATE

FOX_HEADS = 16
FOX_DIM = FOX_HEADS * HEAD_DIM

CONF_DIM = MIX_DIM - FOX_DIM
CONF_KERNEL = 31

D_FF = 11008
N_EXPERTS = 8
TOP_K = 2
D_EXPERT = 1408

DEEPNORM_ALPHA = (2.0 * DEPTH) ** 0.25
LN_EPS = 1e-5

E_SPLIT_SIZES = (NSA_Q_DIM,) + (NSA_KV_DIM,) * 6 + (3 * NSA_HEADS, SSM_DIM, SSM_CONV_DIM, SSM_HEADS)
O_SPLIT_SIZES = (FOX_DIM, FOX_DIM, FOX_DIM, FOX_HEADS, 2 * CONF_DIM)

LANES = 128
VMEM_LIMIT_BYTES = 56 * 1024 * 1024


def _round_up(n, m):
    return (n + m - 1) // m * m


def _params(*sem):
    return pltpu.CompilerParams(dimension_semantics=sem, vmem_limit_bytes=VMEM_LIMIT_BYTES)


def _cast_cols_kernel(w_ref, o_ref, *, segments, fill):
    if fill:
        o_ref[...] = jnp.zeros_like(o_ref)
    for src, width, dst in segments:
        o_ref[:, dst:dst + width] = w_ref[0, :, src:src + width].astype(o_ref.dtype)


def cast_cols(w, out_cols, segments, tr=256):
    layers, rows, cols = w.shape
    assert rows % tr == 0
    per_layer = rows // tr
    fill = sum(width for _, width, _ in segments) != out_cols
    return pl.pallas_call(
        functools.partial(_cast_cols_kernel, segments=tuple(segments), fill=fill),
        grid=(layers, per_layer),
        in_specs=[pl.BlockSpec((1, tr, cols), lambda l, i: (l, i, 0))],
        out_specs=pl.BlockSpec((tr, out_cols), lambda l, i: (l * per_layer + i, 0)),
        out_shape=jax.ShapeDtypeStruct((layers * rows, out_cols), jnp.bfloat16),
        compiler_params=_params("parallel", "parallel"),
        name="cast_cols",
    )(w)


def cast_bf16(w, tr=256):
    w3 = w.reshape((-1,) + w.shape[-2:])
    return cast_cols(w3, w3.shape[2], [(0, w3.shape[2], 0)], tr)


def _matmul_kernel(x_ref, w_ref, o_ref):
    o_ref[...] = jnp.dot(x_ref[...], w_ref[...], preferred_element_type=jnp.float32).astype(o_ref.dtype)


def _matmul_acc_kernel(x_ref, w_ref, o_ref, acc_ref):
    k = pl.program_id(2)

    @pl.when(k == 0)
    def _():
        acc_ref[...] = jnp.zeros_like(acc_ref)

    acc_ref[...] += jnp.dot(x_ref[...], w_ref[...], preferred_element_type=jnp.float32)

    @pl.when(k == pl.num_programs(2) - 1)
    def _():
        o_ref[...] = acc_ref[...].astype(o_ref.dtype)


def matmul(x, w, out_dtype, tm, tn, tk=None, n=None, w_row0=0, w_col0=0):
    m, k = x.shape
    n = w.shape[1] if n is None else n
    tk = k if tk is None else tk
    assert m % tm == 0 and n % tn == 0 and k % tk == 0 and w_row0 % tk == 0 and w_col0 % tn == 0
    r0, c0 = w_row0 // tk, w_col0 // tn
    if tk == k:
        return pl.pallas_call(
            _matmul_kernel,
            grid=(n // tn, m // tm),
            in_specs=[pl.BlockSpec((tm, k), lambda j, i: (i, 0)),
                      pl.BlockSpec((k, tn), lambda j, i: (r0, c0 + j))],
            out_specs=pl.BlockSpec((tm, tn), lambda j, i: (i, j)),
            out_shape=jax.ShapeDtypeStruct((m, n), out_dtype),
            compiler_params=_params("parallel", "parallel"),
            name="matmul",
        )(x, w)
    return pl.pallas_call(
        _matmul_acc_kernel,
        grid=(n // tn, m // tm, k // tk),
        in_specs=[pl.BlockSpec((tm, tk), lambda j, i, kk: (i, kk)),
                  pl.BlockSpec((tk, tn), lambda j, i, kk: (r0 + kk, c0 + j))],
        out_specs=pl.BlockSpec((tm, tn), lambda j, i, kk: (i, j)),
        out_shape=jax.ShapeDtypeStruct((m, n), out_dtype),
        scratch_shapes=[pltpu.VMEM((tm, tn), jnp.float32)],
        compiler_params=_params("parallel", "parallel", "arbitrary"),
        name="matmul_acc",
    )(x, w)


def _swiglu_up_kernel(*refs, gated):
    if gated:
        x_ref, wg_ref, wu_ref, gate_ref, o_ref = refs
    else:
        x_ref, wg_ref, wu_ref, o_ref = refs
    x = x_ref[...]
    g = jnp.dot(x, wg_ref[...], preferred_element_type=jnp.float32)
    u = jnp.dot(x, wu_ref[...], preferred_element_type=jnp.float32)
    hid = g * jax.nn.sigmoid(g) * u
    if gated:
        hid = hid * gate_ref[:, :1]
    o_ref[...] = hid.astype(o_ref.dtype)


def swiglu_up(x, wg, wu, tm, tn, w_row0=0):
    m, k = x.shape
    n = wg.shape[1]
    assert m % tm == 0 and n % tn == 0 and w_row0 % k == 0
    r0 = w_row0 // k
    return pl.pallas_call(
        functools.partial(_swiglu_up_kernel, gated=False),
        grid=(m // tm, n // tn),
        in_specs=[pl.BlockSpec((tm, k), lambda i, j: (i, 0)),
                  pl.BlockSpec((k, tn), lambda i, j: (r0, j)),
                  pl.BlockSpec((k, tn), lambda i, j: (r0, j))],
        out_specs=pl.BlockSpec((tm, tn), lambda i, j: (i, j)),
        out_shape=jax.ShapeDtypeStruct((m, n), jnp.bfloat16),
        compiler_params=_params("parallel", "parallel"),
        name="swiglu_up",
    )(x, wg, wu)


def expert_swiglu_up(x, wg, wu, gates_wide, n_experts, expert0, tm):
    m, k = x.shape
    d_e = wg.shape[1]
    return pl.pallas_call(
        functools.partial(_swiglu_up_kernel, gated=True),
        grid=(n_experts, m // tm),
        in_specs=[pl.BlockSpec((tm, k), lambda e, i: (i, 0)),
                  pl.BlockSpec((k, d_e), lambda e, i: (expert0 + e, 0), pipeline_mode=pl.Buffered(1)),
                  pl.BlockSpec((k, d_e), lambda e, i: (expert0 + e, 0), pipeline_mode=pl.Buffered(1)),
                  pl.BlockSpec((tm, LANES), lambda e, i: (i, e))],
        out_specs=pl.BlockSpec((tm, d_e), lambda e, i: (i, e)),
        out_shape=jax.ShapeDtypeStruct((m, n_experts * d_e), jnp.bfloat16),
        compiler_params=_params("parallel", "parallel"),
        name="expert_swiglu_up",
    )(x, wg, wu, gates_wide)


def _router_kernel(x_ref, w_ref, o_ref, *, n_experts):
    logits = jnp.dot(x_ref[...], w_ref[...], preferred_element_type=jnp.float32)
    lane = lax.broadcasted_iota(jnp.int32, logits.shape, 1)
    logits = jnp.where(lane < n_experts, logits, -jnp.inf)
    e = jnp.exp(logits - jnp.max(logits, axis=1, keepdims=True))
    p = e / jnp.sum(e, axis=1, keepdims=True)
    big = jnp.int32(LANES)
    v1 = jnp.max(p, axis=1, keepdims=True)
    i1 = jnp.min(jnp.where(p == v1, lane, big), axis=1, keepdims=True)
    p2 = jnp.where(lane == i1, -1.0, p)
    v2 = jnp.max(p2, axis=1, keepdims=True)
    i2 = jnp.min(jnp.where(p2 == v2, lane, big), axis=1, keepdims=True)
    denom = v1 + v2
    for ex in range(n_experts):
        g = jnp.where(i1 == ex, v1 / denom, jnp.where(i2 == ex, v2 / denom, 0.0))
        o_ref[:, ex * LANES:(ex + 1) * LANES] = jnp.broadcast_to(g, (g.shape[0], LANES))


def moe_router(x, router_padded, n_experts, tm=512):
    m, k = x.shape
    return pl.pallas_call(
        functools.partial(_router_kernel, n_experts=n_experts),
        grid=(m // tm,),
        in_specs=[pl.BlockSpec((tm, k), lambda i: (i, 0)),
                  pl.BlockSpec((k, LANES), lambda i: (0, 0))],
        out_specs=pl.BlockSpec((tm, n_experts * LANES), lambda i: (i, 0)),
        out_shape=jax.ShapeDtypeStruct((m, n_experts * LANES), jnp.float32),
        compiler_params=_params("parallel"),
        name="moe_router",
    )(x, router_padded)


MOE_TILE = 256


def _route_kernel(x_ref, w_ref, o_ref, *, n_experts):
    logits = jnp.dot(x_ref[...].astype(jnp.bfloat16), w_ref[...], preferred_element_type=jnp.float32)
    lane = lax.broadcasted_iota(jnp.int32, logits.shape, 1)
    logits = jnp.where(lane < n_experts, logits, -jnp.inf)
    e = jnp.exp(logits - jnp.max(logits, axis=1, keepdims=True))
    p = e / jnp.sum(e, axis=1, keepdims=True)
    big = jnp.int32(LANES)
    v1 = jnp.max(p, axis=1, keepdims=True)
    i1 = jnp.min(jnp.where(p == v1, lane, big), axis=1, keepdims=True)
    p2 = jnp.where(lane == i1, -1.0, p)
    v2 = jnp.max(p2, axis=1, keepdims=True)
    i2 = jnp.min(jnp.where(p2 == v2, lane, big), axis=1, keepdims=True)
    denom = v1 + v2
    o_ref[...] = jnp.where(lane == 0, i1.astype(jnp.float32),
                           jnp.where(lane == 1, i2.astype(jnp.float32),
                                     jnp.where(lane == 2, v1 / denom, jnp.where(lane == 3, v2 / denom, 0.0))))


def moe_route(x, router_padded, n_experts, tm=512):
    m, k = x.shape
    return pl.pallas_call(
        functools.partial(_route_kernel, n_experts=n_experts),
        grid=(m // tm,),
        in_specs=[pl.BlockSpec((tm, k), lambda i: (i, 0)),
                  pl.BlockSpec((k, LANES), lambda i: (0, 0))],
        out_specs=pl.BlockSpec((tm, LANES), lambda i: (i, 0)),
        out_shape=jax.ShapeDtypeStruct((m, LANES), jnp.float32),
        compiler_params=_params("parallel"),
        name="moe_route",
    )(x, router_padded)


def _row_copy(src_hbm, row, buf, slot, r, sem):
    return pltpu.make_async_copy(src_hbm.at[pl.ds(row, 1), :], buf.at[slot, pl.ds(r, 1), :], sem.at[slot])


def _moe_up_kernel(tile_expert_ref, src_ref, n_used_ref, x_hbm, wg_ref, wu_ref, o_ref, xbuf, sem):
    tile = xbuf.shape[1]
    i = pl.program_id(0)
    n_used = n_used_ref[0]
    slot = i % 2

    def start_gather(t_idx, s):
        def body(r, carry):
            _row_copy(x_hbm, src_ref[t_idx * tile + r], xbuf, s, r, sem).start()
            return carry
        lax.fori_loop(0, tile, body, 0)

    @pl.when(i == 0)
    def _():
        start_gather(0, 0)

    @pl.when(i + 1 < n_used)
    def _():
        start_gather(i + 1, 1 - slot)

    @pl.when(i < n_used)
    def _():
        def wait_row(r, carry):
            _row_copy(x_hbm, 0, xbuf, slot, r, sem).wait()
            return carry
        lax.fori_loop(0, tile, wait_row, 0)
        x = xbuf[slot].astype(jnp.bfloat16)
        g = jnp.dot(x, wg_ref[...], preferred_element_type=jnp.float32)
        u = jnp.dot(x, wu_ref[...], preferred_element_type=jnp.float32)
        o_ref[...] = (g * jax.nn.sigmoid(g) * u).astype(o_ref.dtype)

    @pl.when(i >= n_used)
    def _():
        o_ref[...] = jnp.zeros_like(o_ref)


def moe_up_sorted(x, wg, wu, tile_expert, src_row, n_used, expert0):
    k = x.shape[1]
    d_e = wg.shape[1]
    n_tiles = tile_expert.shape[0]
    w_spec = pl.BlockSpec((k, d_e), lambda i, te, src, nu: (expert0 + te[i], 0), pipeline_mode=pl.Buffered(1))
    return pl.pallas_call(
        _moe_up_kernel,
        grid_spec=pltpu.PrefetchScalarGridSpec(
            num_scalar_prefetch=3,
            grid=(n_tiles,),
            in_specs=[pl.BlockSpec(memory_space=pl.ANY), w_spec, w_spec],
            out_specs=pl.BlockSpec((MOE_TILE, d_e), lambda i, te, src, nu: (i, 0)),
            scratch_shapes=[pltpu.VMEM((2, MOE_TILE, k), jnp.float32), pltpu.SemaphoreType.DMA((2,))]),
        out_shape=jax.ShapeDtypeStruct((n_tiles * MOE_TILE, d_e), jnp.bfloat16),
        compiler_params=_params("arbitrary"),
        name="moe_up_sorted",
    )(tile_expert, src_row, n_used, x, wg, wu)


def _moe_down_kernel(tile_expert_ref, n_used_ref, h_ref, w_ref, o_ref):
    @pl.when(pl.program_id(0) < n_used_ref[0])
    def _():
        o_ref[...] = jnp.dot(h_ref[...], w_ref[...], preferred_element_type=jnp.float32)

    @pl.when(pl.program_id(0) >= n_used_ref[0])
    def _():
        o_ref[...] = jnp.zeros_like(o_ref)


def moe_down_sorted(hid, wd, tile_expert, n_used, expert0):
    d_e = hid.shape[1]
    n = wd.shape[1]
    n_tiles = tile_expert.shape[0]
    return pl.pallas_call(
        _moe_down_kernel,
        grid_spec=pltpu.PrefetchScalarGridSpec(
            num_scalar_prefetch=2,
            grid=(n_tiles,),
            in_specs=[pl.BlockSpec((MOE_TILE, d_e), lambda i, te, nu: (i, 0)),
                      pl.BlockSpec((d_e, n), lambda i, te, nu: (expert0 + te[i], 0))],
            out_specs=pl.BlockSpec((MOE_TILE, n), lambda i, te, nu: (i, 0))),
        out_shape=jax.ShapeDtypeStruct((n_tiles * MOE_TILE, n), jnp.float32),
        compiler_params=_params("arbitrary"),
        name="moe_down_sorted",
    )(tile_expert, n_used, hid, wd)


def _moe_combine_kernel(pos_ref, y_hbm, route_ref, o_ref, ybuf, sem):
    tm = o_ref.shape[0]
    i = pl.program_id(0)
    slot = i % 2

    def start_gather(t_idx, s):
        def body(r, carry):
            _row_copy(y_hbm, pos_ref[t_idx * 2 * tm + r], ybuf, s, r, sem).start()
            return carry
        lax.fori_loop(0, 2 * tm, body, 0)

    @pl.when(i == 0)
    def _():
        start_gather(0, 0)

    @pl.when(i + 1 < pl.num_programs(0))
    def _():
        start_gather(i + 1, 1 - slot)

    def wait_row(r, carry):
        _row_copy(y_hbm, 0, ybuf, slot, r, sem).wait()
        return carry
    lax.fori_loop(0, 2 * tm, wait_row, 0)
    route = route_ref[...]
    o_ref[...] = (route[:, 2:3] * ybuf[slot, 0:tm, :] + route[:, 3:4] * ybuf[slot, tm:2 * tm, :]).astype(o_ref.dtype)


def moe_combine(y_sorted, pos, route, tm=256):
    m = route.shape[0]
    n = y_sorted.shape[1]
    return pl.pallas_call(
        _moe_combine_kernel,
        grid_spec=pltpu.PrefetchScalarGridSpec(
            num_scalar_prefetch=1,
            grid=(m // tm,),
            in_specs=[pl.BlockSpec(memory_space=pl.ANY),
                      pl.BlockSpec((tm, LANES), lambda i, pos: (i, 0))],
            out_specs=pl.BlockSpec((tm, n), lambda i, pos: (i, 0)),
            scratch_shapes=[pltpu.VMEM((2, 2 * tm, n), jnp.float32), pltpu.SemaphoreType.DMA((2,))]),
        out_shape=jax.ShapeDtypeStruct((m, n), jnp.bfloat16),
        compiler_params=_params("arbitrary"),
        name="moe_combine",
    )(pos, y_sorted, route)


def moe_plan(route, n_experts):
    m = route.shape[0]
    n_tiles = (2 * m) // MOE_TILE + n_experts
    experts = route[:, :2].astype(jnp.int32)
    e_flat = experts.reshape(-1)
    onehot = (e_flat[:, None] == jnp.arange(n_experts, dtype=jnp.int32)[None, :]).astype(jnp.int32)
    running = jnp.cumsum(onehot, axis=0)
    rank = jnp.sum((running - onehot) * onehot, axis=1)
    counts = running[-1]
    padded = (counts + MOE_TILE - 1) // MOE_TILE * MOE_TILE
    ends = jnp.cumsum(padded)
    starts = ends - padded
    dest = starts[e_flat] + rank
    row = jnp.arange(n_tiles * MOE_TILE, dtype=jnp.int32)
    tile_expert = jnp.minimum(jnp.searchsorted(ends, jnp.arange(n_tiles, dtype=jnp.int32) * MOE_TILE, side="right"),
                              n_experts - 1).astype(jnp.int32)
    order = jnp.argsort(e_flat, stable=True).astype(jnp.int32)
    first = jnp.cumsum(counts) - counts
    e_row = tile_expert[row // MOE_TILE]
    src_assign = order[jnp.clip(first[e_row] + row - starts[e_row], 0, 2 * m - 1)]
    src_row = src_assign // 2
    n_used = (ends[-1] // MOE_TILE).astype(jnp.int32).reshape(1)
    tm = 256
    pos = dest.reshape(m // tm, tm, 2).transpose(0, 2, 1).reshape(-1).astype(jnp.int32)
    return tile_expert, src_row.astype(jnp.int32), n_used, pos


def _ada_kernel(c_ref, w_ref, b_ref, o_ref):
    c = c_ref[...]
    a = c * jax.nn.sigmoid(c)
    o_ref[...] = jnp.dot(a, w_ref[...], preferred_element_type=jnp.float32,
                         precision=lax.Precision.HIGHEST) + b_ref[...]


def ada_projection(c, w_ada, b_ada):
    b, d = c.shape
    n = w_ada.shape[1]
    rows = _round_up(b, 8)
    c_pad = jnp.pad(c, ((0, rows - b), (0, 0)))
    tn = 512
    out = pl.pallas_call(
        _ada_kernel,
        grid=(n // tn,),
        in_specs=[pl.BlockSpec((rows, d), lambda j: (0, 0)),
                  pl.BlockSpec((d, tn), lambda j: (0, j)),
                  pl.BlockSpec((1, tn), lambda j: (0, j))],
        out_specs=pl.BlockSpec((rows, tn), lambda j: (0, j)),
        out_shape=jax.ShapeDtypeStruct((rows, n), jnp.float32),
        compiler_params=_params("parallel"),
        name="ada_projection",
    )(c_pad, w_ada, b_ada.reshape(1, n))
    return out[:b]


def _modulate_kernel(x_ref, sc_ref, sh_ref, h_ref):
    h_ref[...] = (x_ref[...] * (1.0 + sc_ref[0]) + sh_ref[0]).astype(h_ref.dtype)


def modulate(x, sc, sh, seq, tr=256):
    t, d = x.shape
    per_b = seq // tr
    vec = pl.BlockSpec((1, 1, d), lambda i: (i // per_b, 0, 0))
    return pl.pallas_call(
        _modulate_kernel,
        grid=(t // tr,),
        in_specs=[pl.BlockSpec((tr, d), lambda i: (i, 0)), vec, vec],
        out_specs=pl.BlockSpec((tr, d), lambda i: (i, 0)),
        out_shape=jax.ShapeDtypeStruct((t, d), jnp.bfloat16),
        compiler_params=_params("parallel"),
        name="modulate",
    )(x, sc, sh)


def _res_ln_kernel(x_ref, y_ref, gate_ref, g_ref, b_ref, sc_ref, sh_ref, xn_ref, h_ref):
    v = DEEPNORM_ALPHA * x_ref[...] + (1.0 + gate_ref[0]) * y_ref[...].astype(jnp.float32)
    mu = jnp.mean(v, axis=-1, keepdims=True)
    vc = v - mu
    var = jnp.mean(vc * vc, axis=-1, keepdims=True)
    xn = vc * lax.rsqrt(var + LN_EPS) * g_ref[...] + b_ref[...]
    xn_ref[...] = xn
    h_ref[...] = (xn * (1.0 + sc_ref[0]) + sh_ref[0]).astype(h_ref.dtype)


def residual_layer_norm(x, y, gate, ln_g, ln_b, sc_next, sh_next, seq, tr=256, h_dtype=jnp.bfloat16):
    t, d = x.shape
    per_b = seq // tr
    row = pl.BlockSpec((tr, d), lambda i: (i, 0))
    vec_b = pl.BlockSpec((1, 1, d), lambda i: (i // per_b, 0, 0))
    vec = pl.BlockSpec((1, d), lambda i: (0, 0))
    return pl.pallas_call(
        _res_ln_kernel,
        grid=(t // tr,),
        in_specs=[row, row, vec_b, vec, vec, vec_b, vec_b],
        out_specs=[row, row],
        out_shape=[jax.ShapeDtypeStruct((t, d), jnp.float32), jax.ShapeDtypeStruct((t, d), h_dtype)],
        compiler_params=_params("parallel"),
        name="residual_layer_norm",
    )(x, y, gate, ln_g.reshape(1, d), ln_b.reshape(1, d), sc_next, sh_next)


def _rope_kernel(pos_ref, inv_ref, sign_ref, q_ref, k1_ref, k2_ref, qo_ref, k1o_ref, k2o_ref, *, q_scale):
    ang = pos_ref[...].astype(jnp.float32) * inv_ref[...]
    cos = jnp.cos(ang)
    sin = jnp.sin(ang) * sign_ref[...]

    def rot(x_ref, o_ref, scale):
        for h in range(x_ref.shape[1] // HEAD_DIM):
            x = x_ref[:, h * HEAD_DIM:(h + 1) * HEAD_DIM].astype(jnp.float32)
            y = x * cos + pltpu.roll(x, HEAD_DIM // 2, 1) * sin
            o_ref[:, h * HEAD_DIM:(h + 1) * HEAD_DIM] = (y * scale).astype(o_ref.dtype)

    rot(q_ref, qo_ref, q_scale)
    rot(k1_ref, k1o_ref, 1.0)
    rot(k2_ref, k2o_ref, 1.0)


def rope_qkk(proj, pos_col, q_cols, k1_cols, k2_cols, tr=512):
    t = proj.shape[0]
    half = HEAD_DIM // 2
    inv = ROPE_THETA ** (-jnp.arange(half, dtype=jnp.float32) / half)
    inv = jnp.concatenate([inv, inv]).reshape(1, HEAD_DIM)
    sign = jnp.concatenate([-jnp.ones((half,), jnp.float32), jnp.ones((half,), jnp.float32)]).reshape(1, HEAD_DIM)

    def cols(c):
        start, width = c
        assert start % width == 0
        return pl.BlockSpec((tr, width), lambda i: (i, start // width))

    def out(c):
        return pl.BlockSpec((tr, c[1]), lambda i: (i, 0))

    vec = pl.BlockSpec((1, HEAD_DIM), lambda i: (0, 0))
    return pl.pallas_call(
        functools.partial(_rope_kernel, q_scale=HEAD_DIM ** -0.5),
        grid=(t // tr,),
        in_specs=[pl.BlockSpec((tr, 1), lambda i: (i, 0)), vec, vec, cols(q_cols), cols(k1_cols), cols(k2_cols)],
        out_specs=[out(q_cols), out(k1_cols), out(k2_cols)],
        out_shape=[jax.ShapeDtypeStruct((t, c[1]), jnp.bfloat16) for c in (q_cols, k1_cols, k2_cols)],
        compiler_params=_params("parallel"),
        name="rope_qkk",
    )(pos_col, inv, sign, proj, proj, proj)


MASK_VALUE = -1e30
ATT_TILE = 512


def _flash_kernel(qi_ref, kt_ref, first_ref, last_ref, *refs, mode, heads, shared_kv, tile, q_scale):
    if mode == "fox":
        q_ref, k_ref, v_ref, f_ref, o_ref, m_sc, l_sc, acc_sc = refs
    elif mode == "sel":
        q_ref, k_ref, v_ref, sel_ref, o_ref, m_sc, l_sc, acc_sc = refs
    else:
        q_ref, k_ref, v_ref, o_ref, m_sc, l_sc, acc_sc = refs
    step_id = pl.program_id(2)
    qi = qi_ref[step_id]
    kt = kt_ref[step_id]

    @pl.when(first_ref[step_id] == 1)
    def _():
        m_sc[...] = jnp.full_like(m_sc, MASK_VALUE)
        l_sc[...] = jnp.zeros_like(l_sc)
        acc_sc[...] = jnp.zeros_like(acc_sc)

    def step(masked):
        mask = None
        if masked:
            row = qi * tile + lax.broadcasted_iota(jnp.int32, (tile, tile), 0)
            col = kt * tile + lax.broadcasted_iota(jnp.int32, (tile, tile), 1)
            mask = col <= row
            if mode == "win":
                mask = mask & (row - col < WINDOW)
            if mode == "sel":
                n_blk = sel_ref.shape[-1]
                blk = lax.broadcasted_iota(jnp.int32, (n_blk, tile), 0)
                key_blk = (kt * tile + lax.broadcasted_iota(jnp.int32, (n_blk, tile), 1)) // SLC_BLOCK
                expand = jnp.where(blk == key_blk, 1.0, 0.0).astype(jnp.bfloat16)
                member = jnp.dot(sel_ref[0, 0], expand, preferred_element_type=jnp.float32)
                mask = mask & (member > 0.5)
        for r in range(heads):
            q = q_ref[:, r * HEAD_DIM:(r + 1) * HEAD_DIM]
            if q_scale != 1.0:
                q = (q.astype(jnp.float32) * q_scale).astype(q.dtype)
            kv = 0 if shared_kv else r
            k = k_ref[:, kv * HEAD_DIM:(kv + 1) * HEAD_DIM]
            v = v_ref[:, kv * HEAD_DIM:(kv + 1) * HEAD_DIM]
            s = lax.dot_general(q, k, (((1,), (1,)), ((), ())), preferred_element_type=jnp.float32)
            if mode == "fox":
                s = s - f_ref[0, r]
            if masked:
                s = jnp.where(mask, s, MASK_VALUE)
            m_prev = m_sc[r]
            m_new = jnp.maximum(m_prev, jnp.max(s, axis=1, keepdims=True))
            alpha = jnp.exp(m_prev - m_new)
            p = jnp.exp(s - m_new[:, :1])
            if masked:
                p = jnp.where(mask, p, 0.0)
            l_sc[r] = alpha * l_sc[r] + jnp.sum(p, axis=1, keepdims=True)
            acc_sc[r] = alpha * acc_sc[r] + jnp.dot(p.astype(v.dtype), v, preferred_element_type=jnp.float32)
            m_sc[r] = m_new

    if mode == "fox":
        @pl.when(kt < qi)
        def _():
            step(False)

        @pl.when(kt == qi)
        def _():
            step(True)
    else:
        step(True)

    @pl.when(last_ref[step_id] == 1)
    def _():
        for r in range(heads):
            inv_l = 1.0 / l_sc[r]
            o_ref[:, r * HEAD_DIM:(r + 1) * HEAD_DIM] = (acc_sc[r] * inv_l).astype(o_ref.dtype)


def flash_attention(mode, q_arr, q_col0, k_arr, k_col0, v_arr, v_col0, batch, seq, n_groups, heads,
                    shared_kv, extra=None, q_scale=1.0, tile=ATT_TILE):
    t = batch * seq
    nq = seq // tile
    qw = heads * HEAD_DIM
    kvw = HEAD_DIM if shared_kv else qw
    assert q_col0 % qw == 0 and k_col0 % kvw == 0 and v_col0 % kvw == 0 and seq % tile == 0
    reach = WINDOW // tile if mode == "win" else nq
    pairs = [(qi, kt) for qi in range(nq) for kt in range(max(0, qi - reach), qi + 1)]
    tables = [jnp.asarray(col, jnp.int32) for col in (
        [qi for qi, _ in pairs], [kt for _, kt in pairs],
        [int(kt == max(0, qi - reach)) for qi, kt in pairs], [int(kt == qi) for qi, kt in pairs])]

    in_specs = [
        pl.BlockSpec((tile, qw), lambda b, g, s, qi, kt, fi, la: (b * nq + qi[s], q_col0 // qw + g)),
        pl.BlockSpec((tile, kvw), lambda b, g, s, qi, kt, fi, la: (b * nq + kt[s], k_col0 // kvw + g)),
        pl.BlockSpec((tile, kvw), lambda b, g, s, qi, kt, fi, la: (b * nq + kt[s], v_col0 // kvw + g)),
    ]
    args = [q_arr, k_arr, v_arr]
    if mode == "sel":
        in_specs.append(pl.BlockSpec((1, 1, tile, extra.shape[-1]),
                                     lambda b, g, s, qi, kt, fi, la: (b, g, qi[s], 0)))
        args.append(extra)
    elif mode == "fox":
        in_specs.append(pl.BlockSpec((1, heads, 1, tile), lambda b, g, s, qi, kt, fi, la: (b, g, 0, kt[s])))
        args.append(extra)
    stat = pltpu.VMEM((heads, tile, HEAD_DIM), jnp.float32)
    return pl.pallas_call(
        functools.partial(_flash_kernel, mode=mode, heads=heads, shared_kv=shared_kv, tile=tile, q_scale=q_scale),
        grid_spec=pltpu.PrefetchScalarGridSpec(
            num_scalar_prefetch=len(tables),
            grid=(batch, n_groups, len(pairs)),
            in_specs=in_specs,
            out_specs=pl.BlockSpec((tile, qw), lambda b, g, s, qi, kt, fi, la: (b * nq + qi[s], g)),
            scratch_shapes=[stat, stat, stat]),
        out_shape=jax.ShapeDtypeStruct((t, n_groups * qw), jnp.bfloat16),
        compiler_params=_params("parallel", "parallel", "arbitrary"),
        name="flash_" + mode,
    )(*tables, *args)


SSD_COL_TILE = 1024
HALO_ROWS = 8


def _cumsum_rows(x):
    n = x.shape[0]
    row = lax.broadcasted_iota(jnp.int32, x.shape, 0)
    k = 1
    while k < n:
        x = x + jnp.where(row >= k, pltpu.roll(x, k, 0), 0.0)
        k *= 2
    return x


def _ssd_kernel(xs0_ref, xs1_ref, bm_ref, cm_ref, z0_ref, z1_ref, small_ref, conv_w_ref, conv_b_ref,
                dt_bias_ref, a_log_ref, d_skip_ref, norm_w_ref, o_ref, buf_sc, act_sc, state_sc, *, dt_lane0):
    L, P, N = SSM_CHUNK, SSM_HEADDIM, SSM_STATE
    heads_per_group = SSM_HEADS // SSM_GROUPS
    c = pl.program_id(1)

    @pl.when(c == 0)
    def _():
        buf_sc[0:HALO_ROWS, :] = jnp.zeros((HALO_ROWS, SSM_CONV_DIM), jnp.float32)
        state_sc[...] = jnp.zeros_like(state_sc)

    srcs = (xs0_ref, xs1_ref, bm_ref, cm_ref)
    cw = 512
    for cb in range(SSM_CONV_DIM // cw):
        src = srcs[cb * cw // SSD_COL_TILE]
        off = cb * cw % SSD_COL_TILE
        cols = slice(cb * cw, (cb + 1) * cw)
        buf_sc[HALO_ROWS:HALO_ROWS + L, cols] = src[:, off:off + cw].astype(jnp.float32)
        acc = jnp.broadcast_to(conv_b_ref[:, cols], (L, cw))
        for k in range(SSM_CONV):
            start = HALO_ROWS - (SSM_CONV - 1) + k
            acc = acc + conv_w_ref[k:k + 1, cols] * buf_sc[start:start + L, cols]
        act_sc[:, cols] = acc * jax.nn.sigmoid(acc)
        buf_sc[0:HALO_ROWS, cols] = buf_sc[L:L + HALO_ROWS, cols]

    pre = small_ref[...] + dt_bias_ref[...]
    dt = jnp.maximum(pre, 0.0) + jnp.log(1.0 + jnp.exp(-jnp.abs(pre)))
    a_cum = _cumsum_rows(dt * (-jnp.exp(a_log_ref[...])))
    a_cum_t = a_cum.T
    a_last = a_cum[L - 1:L, :]
    tril = lax.broadcasted_iota(jnp.int32, (L, L), 0) >= lax.broadcasted_iota(jnp.int32, (L, L), 1)

    for g in range(SSM_GROUPS):
        bc = act_sc[:, SSM_DIM + g * N:SSM_DIM + (g + 1) * N]
        cc = act_sc[:, SSM_DIM + SSM_GROUPS * N + g * N:SSM_DIM + SSM_GROUPS * N + (g + 1) * N].astype(jnp.bfloat16)
        cb_mat = lax.dot_general(cc, bc.astype(jnp.bfloat16), (((1,), (1,)), ((), ())),
                                 preferred_element_type=jnp.float32)
        bc_t = bc.T.astype(jnp.bfloat16)
        yz = []
        ss = jnp.zeros((L, 1), jnp.float32)
        for e in range(heads_per_group):
            h = g * heads_per_group + e
            lane = dt_lane0 + h
            dt_col = dt[:, lane:lane + 1]
            ac_col = a_cum[:, lane:lane + 1]
            ac_row = a_cum_t[lane:lane + 1, :]
            last = a_last[:, lane:lane + 1]
            decay = jnp.where(tril, jnp.exp(jnp.minimum(ac_col - ac_row, 0.0)), 0.0)
            xs = act_sc[:, h * P:(h + 1) * P]
            xd = xs * dt_col
            y = jnp.dot((cb_mat * decay).astype(jnp.bfloat16), xd.astype(jnp.bfloat16),
                        preferred_element_type=jnp.float32)
            st = state_sc[h]
            y = y + jnp.dot(cc, st.astype(jnp.bfloat16), preferred_element_type=jnp.float32) * jnp.exp(ac_col)
            xdd = (xd * jnp.exp(last - ac_col)).astype(jnp.bfloat16)
            state_sc[h] = jnp.exp(last) * st + jnp.dot(bc_t, xdd, preferred_element_type=jnp.float32)
            y = y + d_skip_ref[:, h:h + 1] * xs
            z_ref = z0_ref if h * P < SSD_COL_TILE else z1_ref
            zc = h * P % SSD_COL_TILE
            z = z_ref[:, zc:zc + P].astype(jnp.float32)
            y = y * (z * jax.nn.sigmoid(z))
            ss = ss + jnp.sum(y * y, axis=1, keepdims=True)
            yz.append(y)
        scale = lax.rsqrt(ss / (heads_per_group * P) + LN_EPS)
        for e in range(heads_per_group):
            h = g * heads_per_group + e
            o_ref[:, h * P:(h + 1) * P] = (yz[e] * scale * norm_w_ref[:, h * P:(h + 1) * P]).astype(o_ref.dtype)


def ssd_mixer(proj_main, proj_small, batch, seq, xbc_col0, z_col0, dt_lane0, conv_w, conv_b, dt_bias, a_log,
              d_skip, norm_w):
    t = batch * seq
    nc = seq // SSM_CHUNK
    ct = SSD_COL_TILE
    assert xbc_col0 % ct == 0 and z_col0 % ct == 0 and SSM_DIM == 2 * ct and SSM_GROUPS * SSM_STATE == ct

    def view(col0):
        return pl.BlockSpec((SSM_CHUNK, ct), lambda b, c: (b * nc + c, col0 // ct))

    def lane_row(v):
        return jnp.pad(v.astype(jnp.float32), (dt_lane0, LANES - dt_lane0 - v.shape[0])).reshape(1, LANES)

    full = lambda a: pl.BlockSpec(a.shape, lambda b, c: (0, 0))
    consts = [conv_w, conv_b.reshape(1, -1), lane_row(dt_bias), lane_row(a_log),
              jnp.pad(d_skip, (0, LANES - SSM_HEADS)).reshape(1, LANES), norm_w.reshape(1, -1)]
    return pl.pallas_call(
        functools.partial(_ssd_kernel, dt_lane0=dt_lane0),
        grid=(batch, nc),
        in_specs=[view(xbc_col0), view(xbc_col0 + ct), view(xbc_col0 + 2 * ct), view(xbc_col0 + 3 * ct),
                  view(z_col0), view(z_col0 + ct),
                  pl.BlockSpec((SSM_CHUNK, LANES), lambda b, c: (b * nc + c, 0))] + [full(a) for a in consts],
        out_specs=pl.BlockSpec((SSM_CHUNK, SSM_DIM), lambda b, c: (b * nc + c, 0)),
        out_shape=jax.ShapeDtypeStruct((t, SSM_DIM), jnp.bfloat16),
        scratch_shapes=[pltpu.VMEM((SSM_CHUNK + HALO_ROWS, SSM_CONV_DIM), jnp.float32),
                        pltpu.VMEM((SSM_CHUNK, SSM_CONV_DIM), jnp.float32),
                        pltpu.VMEM((SSM_HEADS, SSM_STATE, SSM_HEADDIM), jnp.float32)],
        compiler_params=_params("parallel", "arbitrary"),
        name="ssd_mixer",
    )(proj_main, proj_main, proj_main, proj_main, proj_main, proj_main, proj_small, *consts)


BIG = 3.0e38


def _rope_rows(x, pos_col, inv_row, sign_row):
    ang = pos_col.astype(jnp.float32) * inv_row
    return x * jnp.cos(ang) + pltpu.roll(x, HEAD_DIM // 2, 1) * (jnp.sin(ang) * sign_row)


def _gelu_tanh(x):
    return 0.5 * x * (1.0 + jnp.tanh(0.7978845608028654 * (x + 0.044715 * x * x * x)))


def _compress_kernel(kc_ref, vc_ref, pos_ref, emb_ref, w1_ref, w2_ref, inv_ref, sign_ref, ko_ref, vo_ref, x_sc):
    n = ko_ref.shape[2]
    half = CMP_BLOCK // 2
    assert half == CMP_STRIDE
    for which, src, dst in ((0, kc_ref, ko_ref), (1, vc_ref, vo_ref)):
        x_sc[...] = src[...].astype(jnp.float32)
        lo = jnp.zeros((n, HEAD_DIM), jnp.float32)
        hi = jnp.zeros((n, HEAD_DIM), jnp.float32)
        for p in range(half):
            xp = x_sc[pl.ds(p, n, stride=CMP_STRIDE), :]
            for part, tok in ((0, p), (1, half + p)):
                xb = (xp + emb_ref[which, tok:tok + 1, :]).astype(jnp.bfloat16)
                term = jnp.dot(xb, w1_ref[which, tok * HEAD_DIM:(tok + 1) * HEAD_DIM, :],
                               preferred_element_type=jnp.float32)
                if part == 0:
                    lo = lo + term
                else:
                    hi = hi + term
        pre = lo + pltpu.roll(hi, n - 1, 0)
        out = jnp.dot(_gelu_tanh(pre).astype(jnp.bfloat16), w2_ref[which], preferred_element_type=jnp.float32)
        if which == 0:
            out = _rope_rows(out, pos_ref[0], inv_ref[...], sign_ref[...])
        dst[0, 0] = out.astype(dst.dtype)


def nsa_compress(proj_main, kc_col0, vc_col0, pos_cmp, cmp_pos, cmp_w1, cmp_w2, batch, seq):
    n = seq // CMP_STRIDE
    half = HEAD_DIM // 2
    inv = ROPE_THETA ** (-jnp.arange(half, dtype=jnp.float32) / half)
    inv = jnp.concatenate([inv, inv]).reshape(1, HEAD_DIM)
    sign = jnp.concatenate([-jnp.ones((half,), jnp.float32), jnp.ones((half,), jnp.float32)]).reshape(1, HEAD_DIM)
    w1 = cmp_w1.astype(jnp.bfloat16)
    w2 = cmp_w2.astype(jnp.bfloat16)
    full = lambda a: pl.BlockSpec(a.shape, lambda b, g: (0,) * a.ndim)
    out = jax.ShapeDtypeStruct((batch, NSA_KV_HEADS, n, HEAD_DIM), jnp.bfloat16)
    out_spec = pl.BlockSpec((1, 1, n, HEAD_DIM), lambda b, g: (b, g, 0, 0))
    return pl.pallas_call(
        _compress_kernel,
        grid=(batch, NSA_KV_HEADS),
        in_specs=[pl.BlockSpec((seq, HEAD_DIM), lambda b, g: (b, kc_col0 // HEAD_DIM + g)),
                  pl.BlockSpec((seq, HEAD_DIM), lambda b, g: (b, vc_col0 // HEAD_DIM + g)),
                  pl.BlockSpec((1, n, 1), lambda b, g: (b, 0, 0)),
                  full(cmp_pos), full(w1), full(w2), full(inv), full(sign)],
        out_specs=[out_spec, out_spec],
        out_shape=[out, out],
        scratch_shapes=[pltpu.VMEM((seq, HEAD_DIM), jnp.float32)],
        compiler_params=_params("parallel", "parallel"),
        name="nsa_compress",
    )(proj_main, proj_main, pos_cmp, cmp_pos, w1, w2, inv, sign)


def _softmax_rows_or_zero(s, mask, axis):
    s = jnp.where(mask, s, MASK_VALUE)
    e = jnp.where(mask, jnp.exp(s - jnp.max(s, axis=axis, keepdims=True)), 0.0)
    d = jnp.sum(e, axis=axis, keepdims=True)
    return e / jnp.where(d > 0.0, d, 1.0)


def _cmp_select_kernel(q_ref, k_ref, v_ref, o_ref, sel_ref, *, tq, n_slc):
    qi = pl.program_id(2)
    k = k_ref[0, 0]
    v = v_ref[0, 0]
    n = k.shape[0]
    last_tok = CMP_BLOCK - 1
    t_row = qi * tq + lax.broadcasted_iota(jnp.int32, (tq, n), 0)
    mask = CMP_STRIDE * lax.broadcasted_iota(jnp.int32, (tq, n), 1) + last_tok <= t_row
    t_lane = qi * tq + lax.broadcasted_iota(jnp.int32, (n, tq), 1)
    mask_t = CMP_STRIDE * lax.broadcasted_iota(jnp.int32, (n, tq), 0) + last_tok <= t_lane
    p_sum_t = jnp.zeros((n, tq), jnp.float32)
    for r in range(NSA_REP):
        q = q_ref[:, r * HEAD_DIM:(r + 1) * HEAD_DIM]
        s = lax.dot_general(q, k, (((1,), (1,)), ((), ())), preferred_element_type=jnp.float32)
        p = _softmax_rows_or_zero(s, mask, 1)
        o_ref[:, r * HEAD_DIM:(r + 1) * HEAD_DIM] = jnp.dot(
            p.astype(v.dtype), v, preferred_element_type=jnp.float32).astype(o_ref.dtype)
        s_t = lax.dot_general(k, q, (((1,), (1,)), ((), ())), preferred_element_type=jnp.float32)
        p_sum_t = p_sum_t + _softmax_rows_or_zero(s_t, mask_t, 0)
    blk_n = lax.broadcasted_iota(jnp.int32, (n_slc, n), 0) * SLC_BLOCK
    start_n = lax.broadcasted_iota(jnp.int32, (n_slc, n), 1) * CMP_STRIDE
    overlap_t = jnp.where((start_n < blk_n + SLC_BLOCK) & (start_n + CMP_BLOCK > blk_n), 1.0, 0.0).astype(jnp.bfloat16)
    p_hi = p_sum_t.astype(jnp.bfloat16)
    p_lo = (p_sum_t - p_hi.astype(jnp.float32)).astype(jnp.bfloat16)
    imp = (jnp.dot(overlap_t, p_hi, preferred_element_type=jnp.float32)
           + jnp.dot(overlap_t, p_lo, preferred_element_type=jnp.float32))
    blk = lax.broadcasted_iota(jnp.int32, (n_slc, tq), 0)
    qblk = (qi * tq + lax.broadcasted_iota(jnp.int32, (n_slc, tq), 1)) // SLC_BLOCK
    forced = (blk == 0) | (blk == qblk) | (blk == qblk - 1)
    imp = jnp.where(forced, BIG, jnp.where(blk > qblk, -BIG, imp))
    rank = jnp.zeros((n_slc, tq), jnp.float32)
    for j in range(n_slc):
        other = imp[j:j + 1, :]
        beats = (other > imp) | ((other == imp) & (blk > j))
        rank = rank + jnp.where(beats, 1.0, 0.0)
    member = jnp.where(rank < min(N_SELECT, n_slc), 1.0, 0.0)
    sel_ref[0, 0] = member.T.astype(sel_ref.dtype)


def nsa_compressed_attention(q_rot, k_cmp, v_cmp, batch, seq, tq=256):
    t = batch * seq
    nq = seq // tq
    n_slc = seq // SLC_BLOCK
    n = k_cmp.shape[2]
    qw = NSA_REP * HEAD_DIM
    kv_spec = pl.BlockSpec((1, 1, n, HEAD_DIM), lambda b, g, qi: (b, g, 0, 0))
    return pl.pallas_call(
        functools.partial(_cmp_select_kernel, tq=tq, n_slc=n_slc),
        grid=(batch, NSA_KV_HEADS, nq),
        in_specs=[pl.BlockSpec((tq, qw), lambda b, g, qi: (b * nq + qi, g)), kv_spec, kv_spec],
        out_specs=[pl.BlockSpec((tq, qw), lambda b, g, qi: (b * nq + qi, g)),
                   pl.BlockSpec((1, 1, tq, n_slc), lambda b, g, qi: (b, g, qi, 0))],
        out_shape=[jax.ShapeDtypeStruct((t, NSA_Q_DIM), jnp.bfloat16),
                   jax.ShapeDtypeStruct((batch, NSA_KV_HEADS, seq, n_slc), jnp.bfloat16)],
        compiler_params=_params("parallel", "parallel", "parallel"),
        name="nsa_cmp_select",
    )(q_rot, k_cmp, v_cmp)


def _nsa_combine_kernel(cmp_ref, slc_ref, win_ref, gate_ref, ssm_ref, o_ref):
    g = jax.nn.sigmoid(gate_ref[...])
    for h in range(NSA_HEADS):
        c = slice(h * HEAD_DIM, (h + 1) * HEAD_DIM)
        o = (g[:, 3 * h:3 * h + 1] * cmp_ref[:, c].astype(jnp.float32)
             + g[:, 3 * h + 1:3 * h + 2] * slc_ref[:, c].astype(jnp.float32)
             + g[:, 3 * h + 2:3 * h + 3] * win_ref[:, c].astype(jnp.float32))
        o_ref[:, c] = o.astype(o_ref.dtype)
    o_ref[:, NSA_Q_DIM:] = ssm_ref[...]


def nsa_combine(o_cmp, o_slc, o_win, proj_small, o_ssm, tr=512):
    t = o_cmp.shape[0]
    row = lambda w: pl.BlockSpec((tr, w), lambda i: (i, 0))
    return pl.pallas_call(
        _nsa_combine_kernel,
        grid=(t // tr,),
        in_specs=[row(NSA_Q_DIM), row(NSA_Q_DIM), row(NSA_Q_DIM), row(LANES), row(SSM_DIM)],
        out_specs=row(MIX_DIM),
        out_shape=jax.ShapeDtypeStruct((t, MIX_DIM), jnp.bfloat16),
        compiler_params=_params("parallel"),
        name="nsa_combine",
    )(o_cmp, o_slc, o_win, proj_small, o_ssm)


CONF_HALO = 32


def _conformer_kernel(val_ref, gate_ref, fox_ref, w_ref, b_ref, lng_ref, lnb_ref, o_ref, buf_sc, act_sc, *, ts):
    @pl.when(pl.program_id(1) == 0)
    def _():
        buf_sc[0:CONF_HALO, :] = jnp.zeros((CONF_HALO, CONF_DIM), jnp.float32)

    cw = LANES
    total = jnp.zeros((ts, 1), jnp.float32)
    for cb in range(CONF_DIM // cw):
        cols = slice(cb * cw, (cb + 1) * cw)
        gate = gate_ref[:, cols].astype(jnp.float32)
        buf_sc[CONF_HALO:CONF_HALO + ts, cols] = val_ref[:, cols].astype(jnp.float32) * jax.nn.sigmoid(gate)
        acc = jnp.broadcast_to(b_ref[:, cols], (ts, cw))
        for k in range(CONF_KERNEL):
            start = CONF_HALO - (CONF_KERNEL - 1) + k
            acc = acc + w_ref[k:k + 1, cols] * buf_sc[start:start + ts, cols]
        act_sc[:, cols] = acc
        total = total + jnp.sum(acc, axis=1, keepdims=True)
        buf_sc[0:CONF_HALO, cols] = buf_sc[ts:ts + CONF_HALO, cols]
    u = act_sc[...]
    mu = total / CONF_DIM
    uc = u - mu
    var = jnp.mean(uc * uc, axis=1, keepdims=True)
    y = uc * lax.rsqrt(var + LN_EPS) * lng_ref[...] + lnb_ref[...]
    o_ref[:, FOX_DIM:] = (y * jax.nn.sigmoid(y)).astype(o_ref.dtype)
    o_ref[:, :FOX_DIM] = fox_ref[...]


def conformer_mixer(proj_main, glu_col0, o_fox, batch, seq, conv_w, conv_b, ln_g, ln_b, ts=256):
    t = batch * seq
    ns = seq // ts
    assert glu_col0 % CONF_DIM == 0
    full = lambda a: pl.BlockSpec(a.shape, lambda b, s: (0, 0))
    consts = [conv_w, conv_b.reshape(1, -1), ln_g.reshape(1, -1), ln_b.reshape(1, -1)]
    return pl.pallas_call(
        functools.partial(_conformer_kernel, ts=ts),
        grid=(batch, ns),
        in_specs=[pl.BlockSpec((ts, CONF_DIM), lambda b, s: (b * ns + s, glu_col0 // CONF_DIM)),
                  pl.BlockSpec((ts, CONF_DIM), lambda b, s: (b * ns + s, glu_col0 // CONF_DIM + 1)),
                  pl.BlockSpec((ts, FOX_DIM), lambda b, s: (b * ns + s, 0))] + [full(a) for a in consts],
        out_specs=pl.BlockSpec((ts, MIX_DIM), lambda b, s: (b * ns + s, 0)),
        out_shape=jax.ShapeDtypeStruct((t, MIX_DIM), jnp.bfloat16),
        scratch_shapes=[pltpu.VMEM((ts + CONF_HALO, CONF_DIM), jnp.float32),
                        pltpu.VMEM((ts, CONF_DIM), jnp.float32)],
        compiler_params=_params("parallel", "arbitrary"),
        name="conformer_mixer",
    )(proj_main, proj_main, o_fox, *consts)


def _fox_decay_kernel(f_ref, bias_ref, o_ref):
    pre = f_ref[...] + bias_ref[...]
    logf = jnp.minimum(pre, 0.0) - jnp.log(1.0 + jnp.exp(-jnp.abs(pre)))
    o_ref[0] = _cumsum_rows(logf).T[:FOX_HEADS, :]


def fox_decay(proj_small, f_bias, batch, seq):
    bias = jnp.pad(f_bias.astype(jnp.float32), (0, LANES - FOX_HEADS)).reshape(1, LANES)
    out = pl.pallas_call(
        _fox_decay_kernel,
        grid=(batch,),
        in_specs=[pl.BlockSpec((seq, LANES), lambda b: (b, 0)), pl.BlockSpec((1, LANES), lambda b: (0, 0))],
        out_specs=pl.BlockSpec((1, FOX_HEADS, seq), lambda b: (b, 0, 0)),
        out_shape=jax.ShapeDtypeStruct((batch, FOX_HEADS, seq), jnp.float32),
        compiler_params=_params("parallel"),
        name="fox_decay",
    )(proj_small, bias)
    return out.reshape(batch, FOX_HEADS, 1, seq)


TM = 512
FFN_TN = 256
FOX_HEADS_PER_STEP = 4
E_MAIN = NSA_Q_DIM + 6 * NSA_KV_DIM + SSM_DIM + SSM_CONV_DIM
O_MAIN = 3 * FOX_DIM + 2 * CONF_DIM
assert TOP_K == 2


def _bf16(w):
    return w.astype(jnp.bfloat16)


def _col_offsets(sizes):
    return np.concatenate([[0], np.cumsum(sizes)]).tolist()


def _nsa_ssd_heads(proj_main, proj_small, positions, cmp_pos, cmp_w1, cmp_w2, conv_w, conv_b, dt_bias, a_log,
                   d_skip, norm_w):
    B, S = positions.shape
    T = B * S
    G, R = NSA_KV_HEADS, NSA_REP
    kv0 = [NSA_Q_DIM + i * NSA_KV_DIM for i in range(6)]
    z0 = NSA_Q_DIM + 6 * NSA_KV_DIM
    q_rot, ksl_rot, kw_rot = rope_qkk(proj_main, positions.reshape(T, 1), (0, NSA_Q_DIM),
                                      (kv0[2], NSA_KV_DIM), (kv0[4], NSA_KV_DIM))
    n = S // CMP_STRIDE
    pos_cmp = positions[:, CMP_BLOCK - 1::CMP_STRIDE]
    pos_cmp = jnp.pad(pos_cmp, ((0, 0), (0, n - pos_cmp.shape[1]))).reshape(B, n, 1)
    k_cmp, v_cmp = nsa_compress(proj_main, kv0[0], kv0[1], pos_cmp, cmp_pos, cmp_w1, cmp_w2, B, S)
    o_cmp, member = nsa_compressed_attention(q_rot, k_cmp, v_cmp, B, S)
    o_slc = flash_attention("sel", q_rot, 0, ksl_rot, 0, proj_main, kv0[3], B, S, G, R, True, extra=member)
    o_win = flash_attention("win", q_rot, 0, kw_rot, 0, proj_main, kv0[5], B, S, G, R, True)
    o_ssm = ssd_mixer(proj_main, proj_small, B, S, z0 + SSM_DIM, z0, 3 * NSA_HEADS, conv_w, conv_b, dt_bias,
                      a_log, d_skip, norm_w)
    return nsa_combine(o_cmp, o_slc, o_win, proj_small, o_ssm)


def _fox_conformer_heads(proj_main, proj_small, B, S, f_bias, conv_w, conv_b, conf_ln_g, conf_ln_b):
    F = fox_decay(proj_small, f_bias, B, S)
    o_fox = flash_attention("fox", proj_main, 0, proj_main, FOX_DIM, proj_main, 2 * FOX_DIM, B, S,
                            FOX_HEADS // FOX_HEADS_PER_STEP, FOX_HEADS_PER_STEP, False, extra=F,
                            q_scale=HEAD_DIM ** -0.5)
    return conformer_mixer(proj_main, 3 * FOX_DIM, o_fox, B, S, conv_w, conv_b, conf_ln_g, conf_ln_b)


def kernel(x, c, positions, w_ada, b_ada, e_ada_table, e_ln1_g, e_ln1_b, e_ln2_g, e_ln2_b, e_w_in, e_cmp_pos, e_cmp_w1, e_cmp_w2, e_conv_w, e_conv_b, e_dt_bias, e_a_log, e_d_skip, e_ssm_norm_w, e_w_out, e_ffn_w_gate, e_ffn_w_up, e_ffn_w_down, o_ada_table, o_ln1_g, o_ln1_b, o_ln2_g, o_ln2_b, o_w_in, o_fox_f_bias, o_conf_conv_w, o_conf_conv_b, o_conf_ln_g, o_conf_ln_b, o_w_out, o_moe_router, o_moe_w_gate, o_moe_w_up, o_moe_w_down):
    B, S, D = x.shape
    T = B * S
    mod = ada_projection(c, w_ada, b_ada).reshape(B, 6, D)

    def mods(table):
        m = mod + table[None]
        return [m[:, k, None, :] for k in range(6)]

    layer_mods = [mods(e_ada_table[i // 2]) if i % 2 == 0 else mods(o_ada_table[i // 2]) for i in range(DEPTH)]
    eo = _col_offsets(E_SPLIT_SIZES)
    oo = _col_offsets(O_SPLIT_SIZES)
    e_small = eo[8] - eo[7] + eo[11] - eo[10]
    e_in = cast_cols(e_w_in, E_MAIN + LANES,
                     [(0, eo[7], 0), (eo[8], eo[10] - eo[8], eo[7]), (eo[7], eo[8] - eo[7], E_MAIN),
                      (eo[10], eo[11] - eo[10], E_MAIN + eo[8] - eo[7])])
    assert e_small <= LANES
    o_in = cast_cols(o_w_in, O_MAIN + LANES,
                     [(0, oo[3], 0), (oo[4], oo[5] - oo[4], oo[3]), (oo[3], oo[4] - oo[3], O_MAIN)])
    e_out, o_out = cast_bf16(e_w_out), cast_bf16(o_w_out)
    ffn_g, ffn_u, ffn_d = cast_bf16(e_ffn_w_gate), cast_bf16(e_ffn_w_up), cast_bf16(e_ffn_w_down)
    moe_g, moe_u = cast_bf16(o_moe_w_gate, tr=1024), cast_bf16(o_moe_w_up, tr=1024)
    moe_d = cast_bf16(o_moe_w_down, tr=D_EXPERT // 2)
    xf = x.reshape(T, D)
    h = modulate(xf, layer_mods[0][1], layer_mods[0][0], S)
    for i in range(DEPTH):
        j = i // 2
        sh1, sc1, g1, sh2, sc2, g2 = layer_mods[i]
        if i % 2 == 0:
            proj_main = matmul(h, e_in, jnp.bfloat16, TM, 1024, n=E_MAIN, w_row0=j * D)
            proj_small = matmul(h, e_in, jnp.float32, TM, LANES, n=LANES, w_row0=j * D, w_col0=E_MAIN)
            mix = _nsa_ssd_heads(proj_main, proj_small, positions, e_cmp_pos[j], e_cmp_w1[j], e_cmp_w2[j],
                                 e_conv_w[j], e_conv_b[j], e_dt_bias[j], e_a_log[j], e_d_skip[j], e_ssm_norm_w[j])
            y = matmul(mix, e_out, jnp.bfloat16, TM, 1024, w_row0=j * MIX_DIM)
            xf, h = residual_layer_norm(xf, y, g1, e_ln1_g[j], e_ln1_b[j], sc2, sh2, S)
            hid = swiglu_up(h, ffn_g, ffn_u, 2 * TM, FFN_TN, w_row0=j * D)
            y = matmul(hid, ffn_d, jnp.bfloat16, TM, 1024, tk=D_FF // 2, w_row0=j * D_FF)
            ln_g, ln_b = e_ln2_g[j], e_ln2_b[j]
        else:
            proj_main = matmul(h, o_in, jnp.bfloat16, TM, 1024, n=O_MAIN, w_row0=j * D)
            proj_small = matmul(h, o_in, jnp.float32, TM, LANES, n=LANES, w_row0=j * D, w_col0=O_MAIN)
            mix = _fox_conformer_heads(proj_main, proj_small, B, S, o_fox_f_bias[j], o_conf_conv_w[j],
                                       o_conf_conv_b[j], o_conf_ln_g[j], o_conf_ln_b[j])
            y = matmul(mix, o_out, jnp.bfloat16, TM, 1024, w_row0=j * MIX_DIM)
            xf, h = residual_layer_norm(xf, y, g1, o_ln1_g[j], o_ln1_b[j], sc2, sh2, S, h_dtype=jnp.float32)
            router = _bf16(jnp.pad(o_moe_router[j], ((0, 0), (0, LANES - N_EXPERTS))))
            route = moe_route(h, router, N_EXPERTS)
            tile_expert, src_row, n_used, pos = moe_plan(route, N_EXPERTS)
            hid = moe_up_sorted(h, moe_g, moe_u, tile_expert, src_row, n_used, j * N_EXPERTS)
            y_sorted = moe_down_sorted(hid, moe_d, tile_expert, n_used, j * N_EXPERTS)
            y = moe_combine(y_sorted, pos, route)
            ln_g, ln_b = o_ln2_g[j], o_ln2_b[j]
        if i + 1 < DEPTH:
            nsh, nsc = layer_mods[i + 1][0], layer_mods[i + 1][1]
        else:
            nsh, nsc = sh2, sc2
        xf, h = residual_layer_norm(xf, y, g2, ln_g, ln_b, nsc, nsh, S)
    return xf.reshape(B, S, D)
```

```python
import functools

import jax
import jax.numpy as jnp
import numpy as np
from jax import lax
from jax.experimental import pallas as pl
from jax.experimental.pallas import tpu as pltpu

D_MODEL = 4096
DEPTH = 4
HEAD_DIM = 128
MIX_DIM = D_MODEL

NSA_HEADS = 16
NSA_KV_HEADS = 4
NSA_REP = NSA_HEADS // NSA_KV_HEADS
NSA_Q_DIM = NSA_HEADS * HEAD_DIM
NSA_KV_DIM = NSA_KV_HEADS * HEAD_DIM
CMP_BLOCK = 32
CMP_STRIDE = 16
SLC_BLOCK = 64
N_SELECT = 16
WINDOW = 512
ROPE_THETA = 10000.0

SSM_DIM = MIX_DIM - NSA_Q_DIM
SSM_HEADDIM = 64
SSM_HEADS = SSM_DIM // SSM_HEADDIM
SSM_GROUPS = 8
SSM_STATE = 128
SSM_CONV = 4
SSM_CHUNK = 128
SSM_CONV_DIM = SSM_DIM + 2 * SSM_GROUPS * SSM_STATE

FOX_HEADS = 16
FOX_DIM = FOX_HEADS * HEAD_DIM

CONF_DIM = MIX_DIM - FOX_DIM
CONF_KERNEL = 31

D_FF = 11008
N_EXPERTS = 8
TOP_K = 2
D_EXPERT = 1408

DEEPNORM_ALPHA = (2.0 * DEPTH) ** 0.25
LN_EPS = 1e-5

E_SPLIT_SIZES = (NSA_Q_DIM,) + (NSA_KV_DIM,) * 6 + (3 * NSA_HEADS, SSM_DIM, SSM_CONV_DIM, SSM_HEADS)
O_SPLIT_SIZES = (FOX_DIM, FOX_DIM, FOX_DIM, FOX_HEADS, 2 * CONF_DIM)

LANES = 128
VMEM_LIMIT_BYTES = 56 * 1024 * 1024


def _round_up(n, m):
    return (n + m - 1) // m * m


def _params(*sem):
    return pltpu.CompilerParams(dimension_semantics=sem, vmem_limit_bytes=VMEM_LIMIT_BYTES)


def _cast_cols_kernel(w_ref, o_ref, *, segments, fill):
    if fill:
        o_ref[...] = jnp.zeros_like(o_ref)
    for src, width, dst in segments:
        o_ref[:, dst:dst + width] = w_ref[0, :, src:src + width].astype(o_ref.dtype)


def cast_cols(w, out_cols, segments, tr=256):
    layers, rows, cols = w.shape
    assert rows % tr == 0
    per_layer = rows // tr
    fill = sum(width for _, width, _ in segments) != out_cols
    return pl.pallas_call(
        functools.partial(_cast_cols_kernel, segments=tuple(segments), fill=fill),
        grid=(layers, per_layer),
        in_specs=[pl.BlockSpec((1, tr, cols), lambda l, i: (l, i, 0))],
        out_specs=pl.BlockSpec((tr, out_cols), lambda l, i: (l * per_layer + i, 0)),
        out_shape=jax.ShapeDtypeStruct((layers * rows, out_cols), jnp.bfloat16),
        compiler_params=_params("parallel", "parallel"),
        name="cast_cols",
    )(w)


def cast_bf16(w, tr=256):
    w3 = w.reshape((-1,) + w.shape[-2:])
    return cast_cols(w3, w3.shape[2], [(0, w3.shape[2], 0)], tr)


def _matmul_kernel(x_ref, w_ref, o_ref):
    o_ref[...] = jnp.dot(x_ref[...], w_ref[...], preferred_element_type=jnp.float32).astype(o_ref.dtype)


def _matmul_acc_kernel(x_ref, w_ref, o_ref, acc_ref):
    k = pl.program_id(2)

    @pl.when(k == 0)
    def _():
        acc_ref[...] = jnp.zeros_like(acc_ref)

    acc_ref[...] += jnp.dot(x_ref[...], w_ref[...], preferred_element_type=jnp.float32)

    @pl.when(k == pl.num_programs(2) - 1)
    def _():
        o_ref[...] = acc_ref[...].astype(o_ref.dtype)


def matmul(x, w, out_dtype, tm, tn, tk=None, n=None, w_row0=0, w_col0=0, rows_resident=False):
    m, k = x.shape
    n = w.shape[1] if n is None else n
    tk = k if tk is None else tk
    assert m % tm == 0 and n % tn == 0 and k % tk == 0 and w_row0 % tk == 0 and w_col0 % tn == 0
    r0, c0 = w_row0 // tk, w_col0 // tn
    if rows_resident:
        assert tk == k
        return pl.pallas_call(
            _matmul_kernel,
            grid=(m // tm, n // tn),
            in_specs=[pl.BlockSpec((tm, k), lambda i, j: (i, 0)),
                      pl.BlockSpec((k, tn), lambda i, j: (r0, c0 + j))],
            out_specs=pl.BlockSpec((tm, tn), lambda i, j: (i, j)),
            out_shape=jax.ShapeDtypeStruct((m, n), out_dtype),
            compiler_params=_params("parallel", "parallel"),
            name="matmul_rows_resident",
        )(x, w)
    if tk == k:
        return pl.pallas_call(
            _matmul_kernel,
            grid=(n // tn, m // tm),
            in_specs=[pl.BlockSpec((tm, k), lambda j, i: (i, 0)),
                      pl.BlockSpec((k, tn), lambda j, i: (r0, c0 + j))],
            out_specs=pl.BlockSpec((tm, tn), lambda j, i: (i, j)),
            out_shape=jax.ShapeDtypeStruct((m, n), out_dtype),
            compiler_params=_params("parallel", "parallel"),
            name="matmul",
        )(x, w)
    return pl.pallas_call(
        _matmul_acc_kernel,
        grid=(n // tn, m // tm, k // tk),
        in_specs=[pl.BlockSpec((tm, tk), lambda j, i, kk: (i, kk)),
                  pl.BlockSpec((tk, tn), lambda j, i, kk: (r0 + kk, c0 + j))],
        out_specs=pl.BlockSpec((tm, tn), lambda j, i, kk: (i, j)),
        out_shape=jax.ShapeDtypeStruct((m, n), out_dtype),
        scratch_shapes=[pltpu.VMEM((tm, tn), jnp.float32)],
        compiler_params=_params("parallel", "parallel", "arbitrary"),
        name="matmul_acc",
    )(x, w)


def _swiglu_up_kernel(x_ref, wg_ref, wu_ref, o_ref):
    x = x_ref[...]
    g = jnp.dot(x, wg_ref[...], preferred_element_type=jnp.float32)
    u = jnp.dot(x, wu_ref[...], preferred_element_type=jnp.float32)
    o_ref[...] = (g * jax.nn.sigmoid(g) * u).astype(o_ref.dtype)


def swiglu_up(x, wg, wu, tm, tn, w_row0=0):
    m, k = x.shape
    n = wg.shape[1]
    assert m % tm == 0 and n % tn == 0 and w_row0 % k == 0
    r0 = w_row0 // k
    return pl.pallas_call(
        _swiglu_up_kernel,
        grid=(m // tm, n // tn),
        in_specs=[pl.BlockSpec((tm, k), lambda i, j: (i, 0)),
                  pl.BlockSpec((k, tn), lambda i, j: (r0, j)),
                  pl.BlockSpec((k, tn), lambda i, j: (r0, j))],
        out_specs=pl.BlockSpec((tm, tn), lambda i, j: (i, j)),
        out_shape=jax.ShapeDtypeStruct((m, n), jnp.bfloat16),
        compiler_params=_params("parallel", "parallel"),
        name="swiglu_up",
    )(x, wg, wu)


MOE_TILE = 256


def _route_kernel(x_ref, w_ref, o_ref, *, n_experts):
    logits = jnp.dot(x_ref[...].astype(jnp.bfloat16), w_ref[...], preferred_element_type=jnp.float32)
    lane = lax.broadcasted_iota(jnp.int32, logits.shape, 1)
    logits = jnp.where(lane < n_experts, logits, -jnp.inf)
    e = jnp.exp(logits - jnp.max(logits, axis=1, keepdims=True))
    p = e / jnp.sum(e, axis=1, keepdims=True)
    big = jnp.int32(LANES)
    v1 = jnp.max(p, axis=1, keepdims=True)
    i1 = jnp.min(jnp.where(p == v1, lane, big), axis=1, keepdims=True)
    p2 = jnp.where(lane == i1, -1.0, p)
    v2 = jnp.max(p2, axis=1, keepdims=True)
    i2 = jnp.min(jnp.where(p2 == v2, lane, big), axis=1, keepdims=True)
    denom = v1 + v2
    o_ref[...] = jnp.where(lane == 0, i1.astype(jnp.float32),
                           jnp.where(lane == 1, i2.astype(jnp.float32),
                                     jnp.where(lane == 2, v1 / denom, jnp.where(lane == 3, v2 / denom, 0.0))))


def moe_route(x, router_padded, n_experts, tm=512):
    m, k = x.shape
    return pl.pallas_call(
        functools.partial(_route_kernel, n_experts=n_experts),
        grid=(m // tm,),
        in_specs=[pl.BlockSpec((tm, k), lambda i: (i, 0)),
                  pl.BlockSpec((k, LANES), lambda i: (0, 0))],
        out_specs=pl.BlockSpec((tm, LANES), lambda i: (i, 0)),
        out_shape=jax.ShapeDtypeStruct((m, LANES), jnp.float32),
        compiler_params=_params("parallel"),
        name="moe_route",
    )(x, router_padded)


def _row_copy(src_hbm, row, buf, slot, r, sem):
    return pltpu.make_async_copy(src_hbm.at[pl.ds(row, 1), :], buf.at[slot, pl.ds(r, 1), :], sem.at[slot])


def _moe_up_kernel(tile_expert_ref, src_ref, n_used_ref, x_hbm, wg_ref, wu_ref, o_ref, xbuf, sem):
    tile = xbuf.shape[1]
    i = pl.program_id(0)
    n_used = n_used_ref[0]
    slot = i % 2

    def start_gather(t_idx, s):
        def body(r, carry):
            _row_copy(x_hbm, src_ref[t_idx * tile + r], xbuf, s, r, sem).start()
            return carry
        lax.fori_loop(0, tile, body, 0)

    def wait_gather(s):
        def body(r, carry):
            _row_copy(x_hbm, 0, xbuf, s, r, sem).wait()
            return carry
        lax.fori_loop(0, tile, body, 0)

    @pl.when(i == 0)
    def _():
        start_gather(0, 0)

    @pl.when(i < n_used)
    def _():
        wait_gather(slot)
        nxt = jnp.minimum(i + 1, n_used - 1)
        for r in range(tile):
            _row_copy(x_hbm, src_ref[nxt * tile + r], xbuf, 1 - slot, r, sem).start()
        x = xbuf[slot].astype(jnp.bfloat16)
        g = jnp.dot(x, wg_ref[...], preferred_element_type=jnp.float32)
        u = jnp.dot(x, wu_ref[...], preferred_element_type=jnp.float32)
        o_ref[...] = (g * jax.nn.sigmoid(g) * u).astype(o_ref.dtype)

        @pl.when(i + 1 >= n_used)
        def _():
            wait_gather(1 - slot)

    @pl.when(i >= n_used)
    def _():
        o_ref[...] = jnp.zeros_like(o_ref)


def moe_up_sorted(x, wg, wu, tile_expert, src_row, n_used, expert0):
    k = x.shape[1]
    d_e = wg.shape[1]
    n_tiles = tile_expert.shape[0]
    w_spec = pl.BlockSpec((k, d_e), lambda i, te, src, nu: (expert0 + te[i], 0), pipeline_mode=pl.Buffered(1))
    return pl.pallas_call(
        _moe_up_kernel,
        grid_spec=pltpu.PrefetchScalarGridSpec(
            num_scalar_prefetch=3,
            grid=(n_tiles,),
            in_specs=[pl.BlockSpec(memory_space=pl.ANY), w_spec, w_spec],
            out_specs=pl.BlockSpec((MOE_TILE, d_e), lambda i, te, src, nu: (i, 0)),
            scratch_shapes=[pltpu.VMEM((2, MOE_TILE, k), jnp.float32), pltpu.SemaphoreType.DMA((2,))]),
        out_shape=jax.ShapeDtypeStruct((n_tiles * MOE_TILE, d_e), jnp.bfloat16),
        compiler_params=_params("arbitrary"),
        name="moe_up_sorted",
    )(tile_expert, src_row, n_used, x, wg, wu)


def _moe_down_kernel(tile_expert_ref, n_used_ref, h_ref, w_ref, o_ref):
    @pl.when(pl.program_id(0) < n_used_ref[0])
    def _():
        o_ref[...] = jnp.dot(h_ref[...], w_ref[...], preferred_element_type=jnp.float32)

    @pl.when(pl.program_id(0) >= n_used_ref[0])
    def _():
        o_ref[...] = jnp.zeros_like(o_ref)


def moe_down_sorted(hid, wd, tile_expert, n_used, expert0):
    d_e = hid.shape[1]
    n = wd.shape[1]
    n_tiles = tile_expert.shape[0]
    return pl.pallas_call(
        _moe_down_kernel,
        grid_spec=pltpu.PrefetchScalarGridSpec(
            num_scalar_prefetch=2,
            grid=(n_tiles,),
            in_specs=[pl.BlockSpec((MOE_TILE, d_e), lambda i, te, nu: (i, 0)),
                      pl.BlockSpec((d_e, n), lambda i, te, nu: (expert0 + te[i], 0))],
            out_specs=pl.BlockSpec((MOE_TILE, n), lambda i, te, nu: (i, 0))),
        out_shape=jax.ShapeDtypeStruct((n_tiles * MOE_TILE, n), jnp.float32),
        compiler_params=_params("arbitrary"),
        name="moe_down_sorted",
    )(tile_expert, n_used, hid, wd)


def _moe_combine_kernel(pos_ref, y_hbm, route_ref, o_ref, ybuf, sem):
    tm = o_ref.shape[0]
    i = pl.program_id(0)
    slot = i % 2

    def start_gather(t_idx, s):
        def body(r, carry):
            _row_copy(y_hbm, pos_ref[t_idx * 2 * tm + r], ybuf, s, r, sem).start()
            return carry
        lax.fori_loop(0, 2 * tm, body, 0)

    @pl.when(i == 0)
    def _():
        start_gather(0, 0)

    @pl.when(i + 1 < pl.num_programs(0))
    def _():
        start_gather(i + 1, 1 - slot)

    def wait_row(r, carry):
        _row_copy(y_hbm, 0, ybuf, slot, r, sem).wait()
        return carry
    lax.fori_loop(0, 2 * tm, wait_row, 0)
    route = route_ref[...]
    o_ref[...] = (route[:, 2:3] * ybuf[slot, 0:tm, :] + route[:, 3:4] * ybuf[slot, tm:2 * tm, :]).astype(o_ref.dtype)


def moe_combine(y_sorted, pos, route, tm=256):
    m = route.shape[0]
    n = y_sorted.shape[1]
    return pl.pallas_call(
        _moe_combine_kernel,
        grid_spec=pltpu.PrefetchScalarGridSpec(
            num_scalar_prefetch=1,
            grid=(m // tm,),
            in_specs=[pl.BlockSpec(memory_space=pl.ANY),
                      pl.BlockSpec((tm, LANES), lambda i, pos: (i, 0))],
            out_specs=pl.BlockSpec((tm, n), lambda i, pos: (i, 0)),
            scratch_shapes=[pltpu.VMEM((2, 2 * tm, n), jnp.float32), pltpu.SemaphoreType.DMA((2,))]),
        out_shape=jax.ShapeDtypeStruct((m, n), jnp.bfloat16),
        compiler_params=_params("arbitrary"),
        name="moe_combine",
    )(pos, y_sorted, route)


def moe_plan(route, n_experts):
    m = route.shape[0]
    n_tiles = (2 * m) // MOE_TILE + n_experts
    experts = route[:, :2].astype(jnp.int32)
    e_flat = experts.reshape(-1)
    onehot = (e_flat[:, None] == jnp.arange(n_experts, dtype=jnp.int32)[None, :]).astype(jnp.int32)
    running = jnp.cumsum(onehot, axis=0)
    rank = jnp.sum((running - onehot) * onehot, axis=1)
    counts = running[-1]
    padded = (counts + MOE_TILE - 1) // MOE_TILE * MOE_TILE
    ends = jnp.cumsum(padded)
    starts = ends - padded
    dest = jnp.sum(onehot * starts[None, :], axis=1) + rank
    tile_start = jnp.arange(n_tiles, dtype=jnp.int32) * MOE_TILE
    tile_expert = jnp.minimum(jnp.sum((tile_start[:, None] >= ends[None, :]).astype(jnp.int32), axis=1),
                              n_experts - 1)
    order = jnp.argsort(e_flat, stable=True).astype(jnp.int32)
    first = jnp.cumsum(counts) - counts
    tile_onehot = (tile_expert[:, None] == jnp.arange(n_experts, dtype=jnp.int32)[None, :]).astype(jnp.int32)
    tile_shift = jnp.sum(tile_onehot * (first - starts)[None, :], axis=1)
    idx = (tile_start + tile_shift)[:, None] + jnp.arange(MOE_TILE, dtype=jnp.int32)[None, :]
    src_assign = order[jnp.clip(idx.reshape(-1), 0, 2 * m - 1)]
    src_row = src_assign // 2
    n_used = (ends[-1] // MOE_TILE).astype(jnp.int32).reshape(1)
    tm = 256
    pos = dest.reshape(m // tm, tm, 2).transpose(0, 2, 1).reshape(-1).astype(jnp.int32)
    return tile_expert, src_row.astype(jnp.int32), n_used, pos


def _ada_kernel(c_ref, w_ref, b_ref, o_ref):
    c = c_ref[...]
    a = c * jax.nn.sigmoid(c)
    o_ref[...] = jnp.dot(a, w_ref[...], preferred_element_type=jnp.float32,
                         precision=lax.Precision.HIGHEST) + b_ref[...]


def ada_projection(c, w_ada, b_ada):
    b, d = c.shape
    n = w_ada.shape[1]
    rows = _round_up(b, 8)
    c_pad = jnp.pad(c, ((0, rows - b), (0, 0)))
    tn = 512
    out = pl.pallas_call(
        _ada_kernel,
        grid=(n // tn,),
        in_specs=[pl.BlockSpec((rows, d), lambda j: (0, 0)),
                  pl.BlockSpec((d, tn), lambda j: (0, j)),
                  pl.BlockSpec((1, tn), lambda j: (0, j))],
        out_specs=pl.BlockSpec((rows, tn), lambda j: (0, j)),
        out_shape=jax.ShapeDtypeStruct((rows, n), jnp.float32),
        compiler_params=_params("parallel"),
        name="ada_projection",
    )(c_pad, w_ada, b_ada.reshape(1, n))
    return out[:b]


def _modulate_kernel(x_ref, sc_ref, sh_ref, h_ref):
    h_ref[...] = (x_ref[...] * (1.0 + sc_ref[0]) + sh_ref[0]).astype(h_ref.dtype)


def modulate(x, sc, sh, seq, tr=256):
    t, d = x.shape
    per_b = seq // tr
    vec = pl.BlockSpec((1, 1, d), lambda i: (i // per_b, 0, 0))
    return pl.pallas_call(
        _modulate_kernel,
        grid=(t // tr,),
        in_specs=[pl.BlockSpec((tr, d), lambda i: (i, 0)), vec, vec],
        out_specs=pl.BlockSpec((tr, d), lambda i: (i, 0)),
        out_shape=jax.ShapeDtypeStruct((t, d), jnp.bfloat16),
        compiler_params=_params("parallel"),
        name="modulate",
    )(x, sc, sh)


def _res_ln_kernel(x_ref, y_ref, gate_ref, g_ref, b_ref, sc_ref, sh_ref, xn_ref, h_ref):
    v = DEEPNORM_ALPHA * x_ref[...] + (1.0 + gate_ref[0]) * y_ref[...].astype(jnp.float32)
    mu = jnp.mean(v, axis=-1, keepdims=True)
    vc = v - mu
    var = jnp.mean(vc * vc, axis=-1, keepdims=True)
    xn = vc * lax.rsqrt(var + LN_EPS) * g_ref[...] + b_ref[...]
    xn_ref[...] = xn
    h_ref[...] = (xn * (1.0 + sc_ref[0]) + sh_ref[0]).astype(h_ref.dtype)


def residual_layer_norm(x, y, gate, ln_g, ln_b, sc_next, sh_next, seq, tr=256, h_dtype=jnp.bfloat16):
    t, d = x.shape
    per_b = seq // tr
    row = pl.BlockSpec((tr, d), lambda i: (i, 0))
    vec_b = pl.BlockSpec((1, 1, d), lambda i: (i // per_b, 0, 0))
    vec = pl.BlockSpec((1, d), lambda i: (0, 0))
    return pl.pallas_call(
        _res_ln_kernel,
        grid=(t // tr,),
        in_specs=[row, row, vec_b, vec, vec, vec_b, vec_b],
        out_specs=[row, row],
        out_shape=[jax.ShapeDtypeStruct((t, d), jnp.float32), jax.ShapeDtypeStruct((t, d), h_dtype)],
        compiler_params=_params("parallel"),
        name="residual_layer_norm",
    )(x, y, gate, ln_g.reshape(1, d), ln_b.reshape(1, d), sc_next, sh_next)


def _rope_kernel(pos_ref, inv_ref, sign_ref, q_ref, k1_ref, k2_ref, qo_ref, k1o_ref, k2o_ref, *, q_scale):
    ang = pos_ref[...].astype(jnp.float32) * inv_ref[...]
    cos = jnp.cos(ang)
    sin = jnp.sin(ang) * sign_ref[...]

    def rot(x_ref, o_ref, scale):
        for h in range(x_ref.shape[1] // HEAD_DIM):
            x = x_ref[:, h * HEAD_DIM:(h + 1) * HEAD_DIM].astype(jnp.float32)
            y = x * cos + pltpu.roll(x, HEAD_DIM // 2, 1) * sin
            o_ref[:, h * HEAD_DIM:(h + 1) * HEAD_DIM] = (y * scale).astype(o_ref.dtype)

    rot(q_ref, qo_ref, q_scale)
    rot(k1_ref, k1o_ref, 1.0)
    rot(k2_ref, k2o_ref, 1.0)


def rope_qkk(proj, pos_col, q_cols, k1_cols, k2_cols, tr=512):
    t = proj.shape[0]
    half = HEAD_DIM // 2
    inv = ROPE_THETA ** (-jnp.arange(half, dtype=jnp.float32) / half)
    inv = jnp.concatenate([inv, inv]).reshape(1, HEAD_DIM)
    sign = jnp.concatenate([-jnp.ones((half,), jnp.float32), jnp.ones((half,), jnp.float32)]).reshape(1, HEAD_DIM)

    def cols(c):
        start, width = c
        assert start % width == 0
        return pl.BlockSpec((tr, width), lambda i: (i, start // width))

    def out(c):
        return pl.BlockSpec((tr, c[1]), lambda i: (i, 0))

    vec = pl.BlockSpec((1, HEAD_DIM), lambda i: (0, 0))
    return pl.pallas_call(
        functools.partial(_rope_kernel, q_scale=HEAD_DIM ** -0.5),
        grid=(t // tr,),
        in_specs=[pl.BlockSpec((tr, 1), lambda i: (i, 0)), vec, vec, cols(q_cols), cols(k1_cols), cols(k2_cols)],
        out_specs=[out(q_cols), out(k1_cols), out(k2_cols)],
        out_shape=[jax.ShapeDtypeStruct((t, c[1]), jnp.bfloat16) for c in (q_cols, k1_cols, k2_cols)],
        compiler_params=_params("parallel"),
        name="rope_qkk",
    )(pos_col, inv, sign, proj, proj, proj)


MASK_VALUE = -1e30
ATT_TILE = 512


def _flash_kernel(qi_ref, kt_ref, first_ref, last_ref, *refs, mode, heads, shared_kv, tile, q_scale):
    if mode == "fox":
        q_ref, k_ref, v_ref, f_ref, o_ref, m_sc, l_sc, acc_sc = refs
    elif mode == "sel":
        q_ref, k_ref, v_ref, sel_ref, o_ref, m_sc, l_sc, acc_sc = refs
    else:
        q_ref, k_ref, v_ref, o_ref, m_sc, l_sc, acc_sc = refs
    step_id = pl.program_id(2)
    qi = qi_ref[step_id]
    kt = kt_ref[step_id]

    @pl.when(first_ref[step_id] == 1)
    def _():
        m_sc[...] = jnp.full_like(m_sc, MASK_VALUE)
        l_sc[...] = jnp.zeros_like(l_sc)
        acc_sc[...] = jnp.zeros_like(acc_sc)

    def step(masked):
        mask = None
        if masked:
            row = qi * tile + lax.broadcasted_iota(jnp.int32, (tile, tile), 0)
            col = kt * tile + lax.broadcasted_iota(jnp.int32, (tile, tile), 1)
            mask = col <= row
            if mode == "win":
                mask = mask & (row - col < WINDOW)
            if mode == "sel":
                n_blk = sel_ref.shape[-1]
                blk = lax.broadcasted_iota(jnp.int32, (n_blk, tile), 0)
                key_blk = (kt * tile + lax.broadcasted_iota(jnp.int32, (n_blk, tile), 1)) // SLC_BLOCK
                expand = jnp.where(blk == key_blk, 1.0, 0.0).astype(jnp.bfloat16)
                member = jnp.dot(sel_ref[0, 0], expand, preferred_element_type=jnp.float32)
                mask = mask & (member > 0.5)
        for r in range(heads):
            q = q_ref[:, r * HEAD_DIM:(r + 1) * HEAD_DIM]
            if q_scale != 1.0:
                q = (q.astype(jnp.float32) * q_scale).astype(q.dtype)
            kv = 0 if shared_kv else r
            k = k_ref[:, kv * HEAD_DIM:(kv + 1) * HEAD_DIM]
            v = v_ref[:, kv * HEAD_DIM:(kv + 1) * HEAD_DIM]
            s = lax.dot_general(q, k, (((1,), (1,)), ((), ())), preferred_element_type=jnp.float32)
            if mode == "fox":
                s = s - f_ref[0, r]
            if masked:
                s = jnp.where(mask, s, MASK_VALUE)
            m_prev = m_sc[r]
            m_new = jnp.maximum(m_prev, jnp.max(s, axis=1, keepdims=True))
            alpha = jnp.exp(m_prev - m_new)
            p = jnp.exp(s - m_new[:, :1])
            if masked:
                p = jnp.where(mask, p, 0.0)
            l_sc[r] = alpha * l_sc[r] + jnp.sum(p, axis=1, keepdims=True)
            acc_sc[r] = alpha * acc_sc[r] + jnp.dot(p.astype(v.dtype), v, preferred_element_type=jnp.float32)
            m_sc[r] = m_new

    if mode == "fox":
        @pl.when(kt < qi)
        def _():
            step(False)

        @pl.when(kt == qi)
        def _():
            step(True)
    else:
        step(True)

    @pl.when(last_ref[step_id] == 1)
    def _():
        for r in range(heads):
            inv_l = 1.0 / l_sc[r]
            o_ref[:, r * HEAD_DIM:(r + 1) * HEAD_DIM] = (acc_sc[r] * inv_l).astype(o_ref.dtype)


def flash_attention(mode, q_arr, q_col0, k_arr, k_col0, v_arr, v_col0, batch, seq, n_groups, heads,
                    shared_kv, extra=None, q_scale=1.0, tile=ATT_TILE):
    t = batch * seq
    nq = seq // tile
    qw = heads * HEAD_DIM
    kvw = HEAD_DIM if shared_kv else qw
    assert q_col0 % qw == 0 and k_col0 % kvw == 0 and v_col0 % kvw == 0 and seq % tile == 0
    reach = WINDOW // tile if mode == "win" else nq
    pairs = [(qi, kt) for qi in range(nq) for kt in range(max(0, qi - reach), qi + 1)]
    tables = [jnp.asarray(col, jnp.int32) for col in (
        [qi for qi, _ in pairs], [kt for _, kt in pairs],
        [int(kt == max(0, qi - reach)) for qi, kt in pairs], [int(kt == qi) for qi, kt in pairs])]

    in_specs = [
        pl.BlockSpec((tile, qw), lambda b, g, s, qi, kt, fi, la: (b * nq + qi[s], q_col0 // qw + g)),
        pl.BlockSpec((tile, kvw), lambda b, g, s, qi, kt, fi, la: (b * nq + kt[s], k_col0 // kvw + g)),
        pl.BlockSpec((tile, kvw), lambda b, g, s, qi, kt, fi, la: (b * nq + kt[s], v_col0 // kvw + g)),
    ]
    args = [q_arr, k_arr, v_arr]
    if mode == "sel":
        in_specs.append(pl.BlockSpec((1, 1, tile, extra.shape[-1]),
                                     lambda b, g, s, qi, kt, fi, la: (b, g, qi[s], 0)))
        args.append(extra)
    elif mode == "fox":
        in_specs.append(pl.BlockSpec((1, heads, 1, tile), lambda b, g, s, qi, kt, fi, la: (b, g, 0, kt[s])))
        args.append(extra)
    stat = pltpu.VMEM((heads, tile, HEAD_DIM), jnp.float32)
    return pl.pallas_call(
        functools.partial(_flash_kernel, mode=mode, heads=heads, shared_kv=shared_kv, tile=tile, q_scale=q_scale),
        grid_spec=pltpu.PrefetchScalarGridSpec(
            num_scalar_prefetch=len(tables),
            grid=(batch, n_groups, len(pairs)),
            in_specs=in_specs,
            out_specs=pl.BlockSpec((tile, qw), lambda b, g, s, qi, kt, fi, la: (b * nq + qi[s], g)),
            scratch_shapes=[stat, stat, stat]),
        out_shape=jax.ShapeDtypeStruct((t, n_groups * qw), jnp.bfloat16),
        compiler_params=_params("parallel", "parallel", "arbitrary"),
        name="flash_" + mode,
    )(*tables, *args)


SSD_COL_TILE = 1024
HALO_ROWS = 8


def _cumsum_rows(x):
    n = x.shape[0]
    row = lax.broadcasted_iota(jnp.int32, x.shape, 0)
    k = 1
    while k < n:
        x = x + jnp.where(row >= k, pltpu.roll(x, k, 0), 0.0)
        k *= 2
    return x


def _ssd_kernel(xs0_ref, xs1_ref, bm_ref, cm_ref, z0_ref, z1_ref, small_ref, conv_w_ref, conv_b_ref,
                dt_bias_ref, a_log_ref, d_skip_ref, norm_w_ref, o_ref, buf_sc, act_sc, state_sc, *, dt_lane0):
    L, P, N = SSM_CHUNK, SSM_HEADDIM, SSM_STATE
    heads_per_group = SSM_HEADS // SSM_GROUPS
    c = pl.program_id(1)

    @pl.when(c == 0)
    def _():
        buf_sc[0:HALO_ROWS, :] = jnp.zeros((HALO_ROWS, SSM_CONV_DIM), jnp.float32)
        state_sc[...] = jnp.zeros_like(state_sc)

    srcs = (xs0_ref, xs1_ref, bm_ref, cm_ref)
    cw = 512
    for cb in range(SSM_CONV_DIM // cw):
        src = srcs[cb * cw // SSD_COL_TILE]
        off = cb * cw % SSD_COL_TILE
        cols = slice(cb * cw, (cb + 1) * cw)
        buf_sc[HALO_ROWS:HALO_ROWS + L, cols] = src[:, off:off + cw].astype(jnp.float32)
        acc = jnp.broadcast_to(conv_b_ref[:, cols], (L, cw))
        for k in range(SSM_CONV):
            start = HALO_ROWS - (SSM_CONV - 1) + k
            acc = acc + conv_w_ref[k:k + 1, cols] * buf_sc[start:start + L, cols]
        act_sc[:, cols] = acc * jax.nn.sigmoid(acc)
        buf_sc[0:HALO_ROWS, cols] = buf_sc[L:L + HALO_ROWS, cols]

    pre = small_ref[...] + dt_bias_ref[...]
    dt = jnp.maximum(pre, 0.0) + jnp.log(1.0 + jnp.exp(-jnp.abs(pre)))
    a_cum = _cumsum_rows(dt * (-jnp.exp(a_log_ref[...])))
    a_cum_t = a_cum.T
    a_last = a_cum[L - 1:L, :]
    tril = lax.broadcasted_iota(jnp.int32, (L, L), 0) >= lax.broadcasted_iota(jnp.int32, (L, L), 1)

    for g in range(SSM_GROUPS):
        bc = act_sc[:, SSM_DIM + g * N:SSM_DIM + (g + 1) * N]
        cc = act_sc[:, SSM_DIM + SSM_GROUPS * N + g * N:SSM_DIM + SSM_GROUPS * N + (g + 1) * N].astype(jnp.bfloat16)
        cb_mat = lax.dot_general(cc, bc.astype(jnp.bfloat16), (((1,), (1,)), ((), ())),
                                 preferred_element_type=jnp.float32)
        bc_t = bc.T.astype(jnp.bfloat16)
        yz = []
        ss = jnp.zeros((L, 1), jnp.float32)
        for e in range(heads_per_group):
            h = g * heads_per_group + e
            lane = dt_lane0 + h
            dt_col = dt[:, lane:lane + 1]
            ac_col = a_cum[:, lane:lane + 1]
            ac_row = a_cum_t[lane:lane + 1, :]
            last = a_last[:, lane:lane + 1]
            decay = jnp.where(tril, jnp.exp(jnp.minimum(ac_col - ac_row, 0.0)), 0.0)
            xs = act_sc[:, h * P:(h + 1) * P]
            xd = xs * dt_col
            y = jnp.dot((cb_mat * decay).astype(jnp.bfloat16), xd.astype(jnp.bfloat16),
                        preferred_element_type=jnp.float32)
            st = state_sc[h]
            y = y + jnp.dot(cc, st.astype(jnp.bfloat16), preferred_element_type=jnp.float32) * jnp.exp(ac_col)
            xdd = (xd * jnp.exp(last - ac_col)).astype(jnp.bfloat16)
            state_sc[h] = jnp.exp(last) * st + jnp.dot(bc_t, xdd, preferred_element_type=jnp.float32)
            y = y + d_skip_ref[:, h:h + 1] * xs
            z_ref = z0_ref if h * P < SSD_COL_TILE else z1_ref
            zc = h * P % SSD_COL_TILE
            z = z_ref[:, zc:zc + P].astype(jnp.float32)
            y = y * (z * jax.nn.sigmoid(z))
            ss = ss + jnp.sum(y * y, axis=1, keepdims=True)
            yz.append(y)
        scale = lax.rsqrt(ss / (heads_per_group * P) + LN_EPS)
        for e in range(heads_per_group):
            h = g * heads_per_group + e
            o_ref[:, h * P:(h + 1) * P] = (yz[e] * scale * norm_w_ref[:, h * P:(h + 1) * P]).astype(o_ref.dtype)


def ssd_mixer(proj_main, proj_small, batch, seq, xbc_col0, z_col0, dt_lane0, conv_w, conv_b, dt_bias, a_log,
              d_skip, norm_w):
    t = batch * seq
    nc = seq // SSM_CHUNK
    ct = SSD_COL_TILE
    assert xbc_col0 % ct == 0 and z_col0 % ct == 0 and SSM_DIM == 2 * ct and SSM_GROUPS * SSM_STATE == ct

    def view(col0):
        return pl.BlockSpec((SSM_CHUNK, ct), lambda b, c: (b * nc + c, col0 // ct))

    def lane_row(v):
        return jnp.pad(v.astype(jnp.float32), (dt_lane0, LANES - dt_lane0 - v.shape[0])).reshape(1, LANES)

    full = lambda a: pl.BlockSpec(a.shape, lambda b, c: (0, 0))
    consts = [conv_w, conv_b.reshape(1, -1), lane_row(dt_bias), lane_row(a_log),
              jnp.pad(d_skip, (0, LANES - SSM_HEADS)).reshape(1, LANES), norm_w.reshape(1, -1)]
    return pl.pallas_call(
        functools.partial(_ssd_kernel, dt_lane0=dt_lane0),
        grid=(batch, nc),
        in_specs=[view(xbc_col0), view(xbc_col0 + ct), view(xbc_col0 + 2 * ct), view(xbc_col0 + 3 * ct),
                  view(z_col0), view(z_col0 + ct),
                  pl.BlockSpec((SSM_CHUNK, LANES), lambda b, c: (b * nc + c, 0))] + [full(a) for a in consts],
        out_specs=pl.BlockSpec((SSM_CHUNK, SSM_DIM), lambda b, c: (b * nc + c, 0)),
        out_shape=jax.ShapeDtypeStruct((t, SSM_DIM), jnp.bfloat16),
        scratch_shapes=[pltpu.VMEM((SSM_CHUNK + HALO_ROWS, SSM_CONV_DIM), jnp.float32),
                        pltpu.VMEM((SSM_CHUNK, SSM_CONV_DIM), jnp.float32),
                        pltpu.VMEM((SSM_HEADS, SSM_STATE, SSM_HEADDIM), jnp.float32)],
        compiler_params=_params("parallel", "arbitrary"),
        name="ssd_mixer",
    )(proj_main, proj_main, proj_main, proj_main, proj_main, proj_main, proj_small, *consts)


BIG = 3.0e38


def _rope_rows(x, pos_col, inv_row, sign_row):
    ang = pos_col.astype(jnp.float32) * inv_row
    return x * jnp.cos(ang) + pltpu.roll(x, HEAD_DIM // 2, 1) * (jnp.sin(ang) * sign_row)


def _gelu_tanh(x):
    return 0.5 * x * (1.0 + jnp.tanh(0.7978845608028654 * (x + 0.044715 * x * x * x)))


def _compress_kernel(kc_ref, vc_ref, pos_ref, emb_ref, w1_ref, w2_ref, inv_ref, sign_ref, ko_ref, vo_ref, x_sc):
    n = ko_ref.shape[2]
    half = CMP_BLOCK // 2
    assert half == CMP_STRIDE
    for which, src, dst in ((0, kc_ref, ko_ref), (1, vc_ref, vo_ref)):
        x_sc[...] = src[...].astype(jnp.float32)
        lo = jnp.zeros((n, HEAD_DIM), jnp.float32)
        hi = jnp.zeros((n, HEAD_DIM), jnp.float32)
        for p in range(half):
            xp = x_sc[pl.ds(p, n, stride=CMP_STRIDE), :]
            for part, tok in ((0, p), (1, half + p)):
                xb = (xp + emb_ref[which, tok:tok + 1, :]).astype(jnp.bfloat16)
                term = jnp.dot(xb, w1_ref[which, tok * HEAD_DIM:(tok + 1) * HEAD_DIM, :],
                               preferred_element_type=jnp.float32)
                if part == 0:
                    lo = lo + term
                else:
                    hi = hi + term
        pre = lo + pltpu.roll(hi, n - 1, 0)
        out = jnp.dot(_gelu_tanh(pre).astype(jnp.bfloat16), w2_ref[which], preferred_element_type=jnp.float32)
        if which == 0:
            out = _rope_rows(out, pos_ref[0], inv_ref[...], sign_ref[...])
        dst[0, 0] = out.astype(dst.dtype)


def nsa_compress(proj_main, kc_col0, vc_col0, pos_cmp, cmp_pos, cmp_w1, cmp_w2, batch, seq):
    n = seq // CMP_STRIDE
    half = HEAD_DIM // 2
    inv = ROPE_THETA ** (-jnp.arange(half, dtype=jnp.float32) / half)
    inv = jnp.concatenate([inv, inv]).reshape(1, HEAD_DIM)
    sign = jnp.concatenate([-jnp.ones((half,), jnp.float32), jnp.ones((half,), jnp.float32)]).reshape(1, HEAD_DIM)
    w1 = cmp_w1.astype(jnp.bfloat16)
    w2 = cmp_w2.astype(jnp.bfloat16)
    full = lambda a: pl.BlockSpec(a.shape, lambda b, g: (0,) * a.ndim)
    out = jax.ShapeDtypeStruct((batch, NSA_KV_HEADS, n, HEAD_DIM), jnp.bfloat16)
    out_spec = pl.BlockSpec((1, 1, n, HEAD_DIM), lambda b, g: (b, g, 0, 0))
    return pl.pallas_call(
        _compress_kernel,
        grid=(batch, NSA_KV_HEADS),
        in_specs=[pl.BlockSpec((seq, HEAD_DIM), lambda b, g: (b, kc_col0 // HEAD_DIM + g)),
                  pl.BlockSpec((seq, HEAD_DIM), lambda b, g: (b, vc_col0 // HEAD_DIM + g)),
                  pl.BlockSpec((1, n, 1), lambda b, g: (b, 0, 0)),
                  full(cmp_pos), full(w1), full(w2), full(inv), full(sign)],
        out_specs=[out_spec, out_spec],
        out_shape=[out, out],
        scratch_shapes=[pltpu.VMEM((seq, HEAD_DIM), jnp.float32)],
        compiler_params=_params("parallel", "parallel"),
        name="nsa_compress",
    )(proj_main, proj_main, pos_cmp, cmp_pos, w1, w2, inv, sign)


def _softmax_rows_or_zero(s, mask, axis):
    s = jnp.where(mask, s, MASK_VALUE)
    e = jnp.where(mask, jnp.exp(s - jnp.max(s, axis=axis, keepdims=True)), 0.0)
    d = jnp.sum(e, axis=axis, keepdims=True)
    return e / jnp.where(d > 0.0, d, 1.0)


def _cmp_select_kernel(q_ref, k_ref, v_ref, o_ref, sel_ref, *, tq, n_slc):
    qi = pl.program_id(2)
    k = k_ref[0, 0]
    v = v_ref[0, 0]
    n = k.shape[0]
    last_tok = CMP_BLOCK - 1
    t_row = qi * tq + lax.broadcasted_iota(jnp.int32, (tq, n), 0)
    mask = CMP_STRIDE * lax.broadcasted_iota(jnp.int32, (tq, n), 1) + last_tok <= t_row
    t_lane = qi * tq + lax.broadcasted_iota(jnp.int32, (n, tq), 1)
    mask_t = CMP_STRIDE * lax.broadcasted_iota(jnp.int32, (n, tq), 0) + last_tok <= t_lane
    p_sum_t = jnp.zeros((n, tq), jnp.float32)
    for r in range(NSA_REP):
        q = q_ref[:, r * HEAD_DIM:(r + 1) * HEAD_DIM]
        s = lax.dot_general(q, k, (((1,), (1,)), ((), ())), preferred_element_type=jnp.float32)
        p = _softmax_rows_or_zero(s, mask, 1)
        o_ref[:, r * HEAD_DIM:(r + 1) * HEAD_DIM] = jnp.dot(
            p.astype(v.dtype), v, preferred_element_type=jnp.float32).astype(o_ref.dtype)
        s_t = lax.dot_general(k, q, (((1,), (1,)), ((), ())), preferred_element_type=jnp.float32)
        p_sum_t = p_sum_t + _softmax_rows_or_zero(s_t, mask_t, 0)
    blk_n = lax.broadcasted_iota(jnp.int32, (n_slc, n), 0) * SLC_BLOCK
    start_n = lax.broadcasted_iota(jnp.int32, (n_slc, n), 1) * CMP_STRIDE
    overlap_t = jnp.where((start_n < blk_n + SLC_BLOCK) & (start_n + CMP_BLOCK > blk_n), 1.0, 0.0).astype(jnp.bfloat16)
    p_hi = p_sum_t.astype(jnp.bfloat16)
    p_lo = (p_sum_t - p_hi.astype(jnp.float32)).astype(jnp.bfloat16)
    imp = (jnp.dot(overlap_t, p_hi, preferred_element_type=jnp.float32)
           + jnp.dot(overlap_t, p_lo, preferred_element_type=jnp.float32))
    blk = lax.broadcasted_iota(jnp.int32, (n_slc, tq), 0)
    qblk = (qi * tq + lax.broadcasted_iota(jnp.int32, (n_slc, tq), 1)) // SLC_BLOCK
    forced = (blk == 0) | (blk == qblk) | (blk == qblk - 1)
    imp = jnp.where(forced, BIG, jnp.where(blk > qblk, -BIG, imp))
    rank = jnp.zeros((n_slc, tq), jnp.float32)
    for j in range(n_slc):
        other = imp[j:j + 1, :]
        beats = (other > imp) | ((other == imp) & (blk > j))
        rank = rank + jnp.where(beats, 1.0, 0.0)
    member = jnp.where(rank < min(N_SELECT, n_slc), 1.0, 0.0)
    sel_ref[0, 0] = member.T.astype(sel_ref.dtype)


def nsa_compressed_attention(q_rot, k_cmp, v_cmp, batch, seq, tq=256):
    t = batch * seq
    nq = seq // tq
    n_slc = seq // SLC_BLOCK
    n = k_cmp.shape[2]
    qw = NSA_REP * HEAD_DIM
    kv_spec = pl.BlockSpec((1, 1, n, HEAD_DIM), lambda b, g, qi: (b, g, 0, 0))
    return pl.pallas_call(
        functools.partial(_cmp_select_kernel, tq=tq, n_slc=n_slc),
        grid=(batch, NSA_KV_HEADS, nq),
        in_specs=[pl.BlockSpec((tq, qw), lambda b, g, qi: (b * nq + qi, g)), kv_spec, kv_spec],
        out_specs=[pl.BlockSpec((tq, qw), lambda b, g, qi: (b * nq + qi, g)),
                   pl.BlockSpec((1, 1, tq, n_slc), lambda b, g, qi: (b, g, qi, 0))],
        out_shape=[jax.ShapeDtypeStruct((t, NSA_Q_DIM), jnp.bfloat16),
                   jax.ShapeDtypeStruct((batch, NSA_KV_HEADS, seq, n_slc), jnp.bfloat16)],
        compiler_params=_params("parallel", "parallel", "parallel"),
        name="nsa_cmp_select",
    )(q_rot, k_cmp, v_cmp)


def _nsa_combine_kernel(cmp_ref, slc_ref, win_ref, gate_ref, ssm_ref, o_ref):
    g = jax.nn.sigmoid(gate_ref[...])
    for h in range(NSA_HEADS):
        c = slice(h * HEAD_DIM, (h + 1) * HEAD_DIM)
        o = (g[:, 3 * h:3 * h + 1] * cmp_ref[:, c].astype(jnp.float32)
             + g[:, 3 * h + 1:3 * h + 2] * slc_ref[:, c].astype(jnp.float32)
             + g[:, 3 * h + 2:3 * h + 3] * win_ref[:, c].astype(jnp.float32))
        o_ref[:, c] = o.astype(o_ref.dtype)
    o_ref[:, NSA_Q_DIM:] = ssm_ref[...]


def nsa_combine(o_cmp, o_slc, o_win, proj_small, o_ssm, tr=512):
    t = o_cmp.shape[0]
    row = lambda w: pl.BlockSpec((tr, w), lambda i: (i, 0))
    return pl.pallas_call(
        _nsa_combine_kernel,
        grid=(t // tr,),
        in_specs=[row(NSA_Q_DIM), row(NSA_Q_DIM), row(NSA_Q_DIM), row(LANES), row(SSM_DIM)],
        out_specs=row(MIX_DIM),
        out_shape=jax.ShapeDtypeStruct((t, MIX_DIM), jnp.bfloat16),
        compiler_params=_params("parallel"),
        name="nsa_combine",
    )(o_cmp, o_slc, o_win, proj_small, o_ssm)


CONF_HALO = 32


def _conformer_kernel(val_ref, gate_ref, fox_ref, w_ref, b_ref, lng_ref, lnb_ref, o_ref, buf_sc, act_sc, *, ts):
    @pl.when(pl.program_id(1) == 0)
    def _():
        buf_sc[0:CONF_HALO, :] = jnp.zeros((CONF_HALO, CONF_DIM), jnp.float32)

    cw = LANES
    total = jnp.zeros((ts, 1), jnp.float32)
    for cb in range(CONF_DIM // cw):
        cols = slice(cb * cw, (cb + 1) * cw)
        gate = gate_ref[:, cols].astype(jnp.float32)
        buf_sc[CONF_HALO:CONF_HALO + ts, cols] = val_ref[:, cols].astype(jnp.float32) * jax.nn.sigmoid(gate)
        acc = jnp.broadcast_to(b_ref[:, cols], (ts, cw))
        for k in range(CONF_KERNEL):
            start = CONF_HALO - (CONF_KERNEL - 1) + k
            acc = acc + w_ref[k:k + 1, cols] * buf_sc[start:start + ts, cols]
        act_sc[:, cols] = acc
        total = total + jnp.sum(acc, axis=1, keepdims=True)
        buf_sc[0:CONF_HALO, cols] = buf_sc[ts:ts + CONF_HALO, cols]
    u = act_sc[...]
    mu = total / CONF_DIM
    uc = u - mu
    var = jnp.mean(uc * uc, axis=1, keepdims=True)
    y = uc * lax.rsqrt(var + LN_EPS) * lng_ref[...] + lnb_ref[...]
    o_ref[:, FOX_DIM:] = (y * jax.nn.sigmoid(y)).astype(o_ref.dtype)
    o_ref[:, :FOX_DIM] = fox_ref[...]


def conformer_mixer(proj_main, glu_col0, o_fox, batch, seq, conv_w, conv_b, ln_g, ln_b, ts=256):
    t = batch * seq
    ns = seq // ts
    assert glu_col0 % CONF_DIM == 0
    full = lambda a: pl.BlockSpec(a.shape, lambda b, s: (0, 0))
    consts = [conv_w, conv_b.reshape(1, -1), ln_g.reshape(1, -1), ln_b.reshape(1, -1)]
    return pl.pallas_call(
        functools.partial(_conformer_kernel, ts=ts),
        grid=(batch, ns),
        in_specs=[pl.BlockSpec((ts, CONF_DIM), lambda b, s: (b * ns + s, glu_col0 // CONF_DIM)),
                  pl.BlockSpec((ts, CONF_DIM), lambda b, s: (b * ns + s, glu_col0 // CONF_DIM + 1)),
                  pl.BlockSpec((ts, FOX_DIM), lambda b, s: (b * ns + s, 0))] + [full(a) for a in consts],
        out_specs=pl.BlockSpec((ts, MIX_DIM), lambda b, s: (b * ns + s, 0)),
        out_shape=jax.ShapeDtypeStruct((t, MIX_DIM), jnp.bfloat16),
        scratch_shapes=[pltpu.VMEM((ts + CONF_HALO, CONF_DIM), jnp.float32),
                        pltpu.VMEM((ts, CONF_DIM), jnp.float32)],
        compiler_params=_params("parallel", "arbitrary"),
        name="conformer_mixer",
    )(proj_main, proj_main, o_fox, *consts)


def _fox_decay_kernel(f_ref, bias_ref, o_ref):
    pre = f_ref[...] + bias_ref[...]
    logf = jnp.minimum(pre, 0.0) - jnp.log(1.0 + jnp.exp(-jnp.abs(pre)))
    o_ref[0] = _cumsum_rows(logf).T[:FOX_HEADS, :]


def fox_decay(proj_small, f_bias, batch, seq):
    bias = jnp.pad(f_bias.astype(jnp.float32), (0, LANES - FOX_HEADS)).reshape(1, LANES)
    out = pl.pallas_call(
        _fox_decay_kernel,
        grid=(batch,),
        in_specs=[pl.BlockSpec((seq, LANES), lambda b: (b, 0)), pl.BlockSpec((1, LANES), lambda b: (0, 0))],
        out_specs=pl.BlockSpec((1, FOX_HEADS, seq), lambda b: (b, 0, 0)),
        out_shape=jax.ShapeDtypeStruct((batch, FOX_HEADS, seq), jnp.float32),
        compiler_params=_params("parallel"),
        name="fox_decay",
    )(proj_small, bias)
    return out.reshape(batch, FOX_HEADS, 1, seq)


TM = 512
FFN_TN = 256
FOX_HEADS_PER_STEP = 4
E_MAIN = NSA_Q_DIM + 6 * NSA_KV_DIM + SSM_DIM + SSM_CONV_DIM
O_MAIN = 3 * FOX_DIM + 2 * CONF_DIM
assert TOP_K == 2


def _bf16(w):
    return w.astype(jnp.bfloat16)


def _col_offsets(sizes):
    return np.concatenate([[0], np.cumsum(sizes)]).tolist()


def _nsa_ssd_heads(proj_main, proj_small, positions, cmp_pos, cmp_w1, cmp_w2, conv_w, conv_b, dt_bias, a_log,
                   d_skip, norm_w):
    B, S = positions.shape
    T = B * S
    G, R = NSA_KV_HEADS, NSA_REP
    kv0 = [NSA_Q_DIM + i * NSA_KV_DIM for i in range(6)]
    z0 = NSA_Q_DIM + 6 * NSA_KV_DIM
    q_rot, ksl_rot, kw_rot = rope_qkk(proj_main, positions.reshape(T, 1), (0, NSA_Q_DIM),
                                      (kv0[2], NSA_KV_DIM), (kv0[4], NSA_KV_DIM))
    n = S // CMP_STRIDE
    pos_cmp = positions[:, CMP_BLOCK - 1::CMP_STRIDE]
    pos_cmp = jnp.pad(pos_cmp, ((0, 0), (0, n - pos_cmp.shape[1]))).reshape(B, n, 1)
    k_cmp, v_cmp = nsa_compress(proj_main, kv0[0], kv0[1], pos_cmp, cmp_pos, cmp_w1, cmp_w2, B, S)
    o_cmp, member = nsa_compressed_attention(q_rot, k_cmp, v_cmp, B, S)
    o_slc = flash_attention("sel", q_rot, 0, ksl_rot, 0, proj_main, kv0[3], B, S, G, R, True, extra=member)
    o_win = flash_attention("win", q_rot, 0, kw_rot, 0, proj_main, kv0[5], B, S, G, R, True)
    o_ssm = ssd_mixer(proj_main, proj_small, B, S, z0 + SSM_DIM, z0, 3 * NSA_HEADS, conv_w, conv_b, dt_bias,
                      a_log, d_skip, norm_w)
    return nsa_combine(o_cmp, o_slc, o_win, proj_small, o_ssm)


def _fox_conformer_heads(proj_main, proj_small, B, S, f_bias, conv_w, conv_b, conf_ln_g, conf_ln_b):
    F = fox_decay(proj_small, f_bias, B, S)
    o_fox = flash_attention("fox", proj_main, 0, proj_main, FOX_DIM, proj_main, 2 * FOX_DIM, B, S,
                            FOX_HEADS // FOX_HEADS_PER_STEP, FOX_HEADS_PER_STEP, False, extra=F,
                            q_scale=HEAD_DIM ** -0.5)
    return conformer_mixer(proj_main, 3 * FOX_DIM, o_fox, B, S, conv_w, conv_b, conf_ln_g, conf_ln_b)


def kernel(x, c, positions, w_ada, b_ada, e_ada_table, e_ln1_g, e_ln1_b, e_ln2_g, e_ln2_b, e_w_in, e_cmp_pos, e_cmp_w1, e_cmp_w2, e_conv_w, e_conv_b, e_dt_bias, e_a_log, e_d_skip, e_ssm_norm_w, e_w_out, e_ffn_w_gate, e_ffn_w_up, e_ffn_w_down, o_ada_table, o_ln1_g, o_ln1_b, o_ln2_g, o_ln2_b, o_w_in, o_fox_f_bias, o_conf_conv_w, o_conf_conv_b, o_conf_ln_g, o_conf_ln_b, o_w_out, o_moe_router, o_moe_w_gate, o_moe_w_up, o_moe_w_down):
    B, S, D = x.shape
    T = B * S
    mod = ada_projection(c, w_ada, b_ada).reshape(B, 6, D)

    def mods(table):
        m = mod + table[None]
        return [m[:, k, None, :] for k in range(6)]

    layer_mods = [mods(e_ada_table[i // 2]) if i % 2 == 0 else mods(o_ada_table[i // 2]) for i in range(DEPTH)]
    eo = _col_offsets(E_SPLIT_SIZES)
    oo = _col_offsets(O_SPLIT_SIZES)
    e_small = eo[8] - eo[7] + eo[11] - eo[10]
    e_in = cast_cols(e_w_in, E_MAIN + LANES,
                     [(0, eo[7], 0), (eo[8], eo[10] - eo[8], eo[7]), (eo[7], eo[8] - eo[7], E_MAIN),
                      (eo[10], eo[11] - eo[10], E_MAIN + eo[8] - eo[7])])
    assert e_small <= LANES
    o_in = cast_cols(o_w_in, O_MAIN + LANES,
                     [(0, oo[3], 0), (oo[4], oo[5] - oo[4], oo[3]), (oo[3], oo[4] - oo[3], O_MAIN)])
    e_out, o_out = cast_bf16(e_w_out), cast_bf16(o_w_out)
    ffn_g, ffn_u, ffn_d = cast_bf16(e_ffn_w_gate), cast_bf16(e_ffn_w_up), cast_bf16(e_ffn_w_down)
    moe_g, moe_u = cast_bf16(o_moe_w_gate, tr=1024), cast_bf16(o_moe_w_up, tr=1024)
    moe_d = cast_bf16(o_moe_w_down, tr=D_EXPERT // 2)
    xf = x.reshape(T, D)
    h = modulate(xf, layer_mods[0][1], layer_mods[0][0], S)
    for i in range(DEPTH):
        j = i // 2
        sh1, sc1, g1, sh2, sc2, g2 = layer_mods[i]
        if i % 2 == 0:
            proj_main = matmul(h, e_in, jnp.bfloat16, TM, 1024, n=E_MAIN, w_row0=j * D)
            proj_small = matmul(h, e_in, jnp.float32, TM, LANES, n=LANES, w_row0=j * D, w_col0=E_MAIN)
            mix = _nsa_ssd_heads(proj_main, proj_small, positions, e_cmp_pos[j], e_cmp_w1[j], e_cmp_w2[j],
                                 e_conv_w[j], e_conv_b[j], e_dt_bias[j], e_a_log[j], e_d_skip[j], e_ssm_norm_w[j])
            y = matmul(mix, e_out, jnp.bfloat16, TM, 1024, w_row0=j * MIX_DIM)
            xf, h = residual_layer_norm(xf, y, g1, e_ln1_g[j], e_ln1_b[j], sc2, sh2, S)
            hid = swiglu_up(h, ffn_g, ffn_u, 2 * TM, FFN_TN, w_row0=j * D)
            y = matmul(hid, ffn_d, jnp.bfloat16, TM, 512, w_row0=j * D_FF, rows_resident=True)
            ln_g, ln_b = e_ln2_g[j], e_ln2_b[j]
        else:
            proj_main = matmul(h, o_in, jnp.bfloat16, TM, 1024, n=O_MAIN, w_row0=j * D)
            proj_small = matmul(h, o_in, jnp.float32, TM, LANES, n=LANES, w_row0=j * D, w_col0=O_MAIN)
            mix = _fox_conformer_heads(proj_main, proj_small, B, S, o_fox_f_bias[j], o_conf_conv_w[j],
                                       o_conf_conv_b[j], o_conf_ln_g[j], o_conf_ln_b[j])
            y = matmul(mix, o_out, jnp.bfloat16, TM, 1024, w_row0=j * MIX_DIM)
            xf, h = residual_layer_norm(xf, y, g1, o_ln1_g[j], o_ln1_b[j], sc2, sh2, S, h_dtype=jnp.float32)
            router = _bf16(jnp.pad(o_moe_router[j], ((0, 0), (0, LANES - N_EXPERTS))))
            route = moe_route(h, router, N_EXPERTS)
            tile_expert, src_row, n_used, pos = moe_plan(route, N_EXPERTS)
            hid = moe_up_sorted(h, moe_g, moe_u, tile_expert, src_row, n_used, j * N_EXPERTS)
            y_sorted = moe_down_sorted(hid, moe_d, tile_expert, n_used, j * N_EXPERTS)
            y = moe_combine(y_sorted, pos, route)
            ln_g, ln_b = o_ln2_g[j], o_ln2_b[j]
        if i + 1 < DEPTH:
            nsh, nsc = layer_mods[i + 1][0], layer_mods[i + 1][1]
        else:
            nsh, nsc = sh2, sc2
        xf, h = residual_layer_norm(xf, y, g2, ln_g, ln_b, nsc, nsh, S)
    return xf.reshape(B, S, D)
```

```python
import functools

import jax
import jax.numpy as jnp
import numpy as np
from jax import lax
from jax.experimental import pallas as pl
from jax.experimental.pallas import tpu as pltpu

D_MODEL = 4096
DEPTH = 4
HEAD_DIM = 128
MIX_DIM = D_MODEL

NSA_HEADS = 16
NSA_KV_HEADS = 4
NSA_REP = NSA_HEADS // NSA_KV_HEADS
NSA_Q_DIM = NSA_HEADS * HEAD_DIM
NSA_KV_DIM = NSA_KV_HEADS * HEAD_DIM
CMP_BLOCK = 32
CMP_STRIDE = 16
SLC_BLOCK = 64
N_SELECT = 16
WINDOW = 512
ROPE_THETA = 10000.0

SSM_DIM = MIX_DIM - NSA_Q_DIM
SSM_HEADDIM = 64
SSM_HEADS = SSM_DIM // SSM_HEADDIM
SSM_GROUPS = 8
SSM_STATE = 128
SSM_CONV = 4
SSM_CHUNK = 128
SSM_CONV_DIM = SSM_DIM + 2 * SSM_GROUPS * SSM_STATE

FOX_HEADS = 16
FOX_DIM = FOX_HEADS * HEAD_DIM

CONF_DIM = MIX_DIM - FOX_DIM
CONF_KERNEL = 31

D_FF = 11008
N_EXPERTS = 8
TOP_K = 2
D_EXPERT = 1408

DEEPNORM_ALPHA = (2.0 * DEPTH) ** 0.25
LN_EPS = 1e-5

E_SPLIT_SIZES = (NSA_Q_DIM,) + (NSA_KV_DIM,) * 6 + (3 * NSA_HEADS, SSM_DIM, SSM_CONV_DIM, SSM_HEADS)
O_SPLIT_SIZES = (FOX_DIM, FOX_DIM, FOX_DIM, FOX_HEADS, 2 * CONF_DIM)

LANES = 128
VMEM_LIMIT_BYTES = 56 * 1024 * 1024


def _round_up(n, m):
    return (n + m - 1) // m * m


def _params(*sem):
    return pltpu.CompilerParams(dimension_semantics=sem, vmem_limit_bytes=VMEM_LIMIT_BYTES)


def _cast_cols_kernel(w_ref, o_ref, *, segments, fill):
    if fill:
        o_ref[...] = jnp.zeros_like(o_ref)
    for src, width, dst in segments:
        o_ref[:, dst:dst + width] = w_ref[0, :, src:src + width].astype(o_ref.dtype)


def cast_cols(w, out_cols, segments, tr=256):
    layers, rows, cols = w.shape
    assert rows % tr == 0
    per_layer = rows // tr
    fill = sum(width for _, width, _ in segments) != out_cols
    return pl.pallas_call(
        functools.partial(_cast_cols_kernel, segments=tuple(segments), fill=fill),
        grid=(layers, per_layer),
        in_specs=[pl.BlockSpec((1, tr, cols), lambda l, i: (l, i, 0))],
        out_specs=pl.BlockSpec((tr, out_cols), lambda l, i: (l * per_layer + i, 0)),
        out_shape=jax.ShapeDtypeStruct((layers * rows, out_cols), jnp.bfloat16),
        compiler_params=_params("parallel", "parallel"),
        name="cast_cols",
    )(w)


def cast_bf16(w, tr=256):
    w3 = w.reshape((-1,) + w.shape[-2:])
    return cast_cols(w3, w3.shape[2], [(0, w3.shape[2], 0)], tr)


def _matmul_kernel(x_ref, w_ref, o_ref):
    o_ref[...] = jnp.dot(x_ref[...], w_ref[...], preferred_element_type=jnp.float32).astype(o_ref.dtype)


def _matmul_acc_kernel(x_ref, w_ref, o_ref, acc_ref):
    k = pl.program_id(2)

    @pl.when(k == 0)
    def _():
        acc_ref[...] = jnp.zeros_like(acc_ref)

    acc_ref[...] += jnp.dot(x_ref[...], w_ref[...], preferred_element_type=jnp.float32)

    @pl.when(k == pl.num_programs(2) - 1)
    def _():
        o_ref[...] = acc_ref[...].astype(o_ref.dtype)


def matmul(x, w, out_dtype, tm, tn, tk=None, n=None, w_row0=0, w_col0=0, rows_resident=False):
    m, k = x.shape
    n = w.shape[1] if n is None else n
    tk = k if tk is None else tk
    assert m % tm == 0 and n % tn == 0 and k % tk == 0 and w_row0 % tk == 0 and w_col0 % tn == 0
    r0, c0 = w_row0 // tk, w_col0 // tn
    if rows_resident:
        assert tk == k
        return pl.pallas_call(
            _matmul_kernel,
            grid=(m // tm, n // tn),
            in_specs=[pl.BlockSpec((tm, k), lambda i, j: (i, 0)),
                      pl.BlockSpec((k, tn), lambda i, j: (r0, c0 + j))],
            out_specs=pl.BlockSpec((tm, tn), lambda i, j: (i, j)),
            out_shape=jax.ShapeDtypeStruct((m, n), out_dtype),
            compiler_params=_params("parallel", "parallel"),
            name="matmul_rows_resident",
        )(x, w)
    if tk == k:
        return pl.pallas_call(
            _matmul_kernel,
            grid=(n // tn, m // tm),
            in_specs=[pl.BlockSpec((tm, k), lambda j, i: (i, 0)),
                      pl.BlockSpec((k, tn), lambda j, i: (r0, c0 + j))],
            out_specs=pl.BlockSpec((tm, tn), lambda j, i: (i, j)),
            out_shape=jax.ShapeDtypeStruct((m, n), out_dtype),
            compiler_params=_params("parallel", "parallel"),
            name="matmul",
        )(x, w)
    return pl.pallas_call(
        _matmul_acc_kernel,
        grid=(n // tn, m // tm, k // tk),
        in_specs=[pl.BlockSpec((tm, tk), lambda j, i, kk: (i, kk)),
                  pl.BlockSpec((tk, tn), lambda j, i, kk: (r0 + kk, c0 + j))],
        out_specs=pl.BlockSpec((tm, tn), lambda j, i, kk: (i, j)),
        out_shape=jax.ShapeDtypeStruct((m, n), out_dtype),
        scratch_shapes=[pltpu.VMEM((tm, tn), jnp.float32)],
        compiler_params=_params("parallel", "parallel", "arbitrary"),
        name="matmul_acc",
    )(x, w)


def _swiglu_up_kernel(x_ref, wg_ref, wu_ref, o_ref):
    x = x_ref[...]
    g = jnp.dot(x, wg_ref[...], preferred_element_type=jnp.float32)
    u = jnp.dot(x, wu_ref[...], preferred_element_type=jnp.float32)
    o_ref[...] = (g * jax.nn.sigmoid(g) * u).astype(o_ref.dtype)


def swiglu_up(x, wg, wu, tm, tn, w_row0=0):
    m, k = x.shape
    n = wg.shape[1]
    assert m % tm == 0 and n % tn == 0 and w_row0 % k == 0
    r0 = w_row0 // k
    return pl.pallas_call(
        _swiglu_up_kernel,
        grid=(m // tm, n // tn),
        in_specs=[pl.BlockSpec((tm, k), lambda i, j: (i, 0)),
                  pl.BlockSpec((k, tn), lambda i, j: (r0, j)),
                  pl.BlockSpec((k, tn), lambda i, j: (r0, j))],
        out_specs=pl.BlockSpec((tm, tn), lambda i, j: (i, j)),
        out_shape=jax.ShapeDtypeStruct((m, n), jnp.bfloat16),
        compiler_params=_params("parallel", "parallel"),
        name="swiglu_up",
    )(x, wg, wu)


MOE_TILE = 256


def _route_kernel(x_ref, w_ref, o_ref, *, n_experts):
    logits = jnp.dot(x_ref[...].astype(jnp.bfloat16), w_ref[...], preferred_element_type=jnp.float32)
    lane = lax.broadcasted_iota(jnp.int32, logits.shape, 1)
    logits = jnp.where(lane < n_experts, logits, -jnp.inf)
    e = jnp.exp(logits - jnp.max(logits, axis=1, keepdims=True))
    p = e / jnp.sum(e, axis=1, keepdims=True)
    big = jnp.int32(LANES)
    v1 = jnp.max(p, axis=1, keepdims=True)
    i1 = jnp.min(jnp.where(p == v1, lane, big), axis=1, keepdims=True)
    p2 = jnp.where(lane == i1, -1.0, p)
    v2 = jnp.max(p2, axis=1, keepdims=True)
    i2 = jnp.min(jnp.where(p2 == v2, lane, big), axis=1, keepdims=True)
    denom = v1 + v2
    o_ref[...] = jnp.where(lane == 0, i1.astype(jnp.float32),
                           jnp.where(lane == 1, i2.astype(jnp.float32),
                                     jnp.where(lane == 2, v1 / denom, jnp.where(lane == 3, v2 / denom, 0.0))))


def moe_route(x, router_padded, n_experts, tm=512):
    m, k = x.shape
    return pl.pallas_call(
        functools.partial(_route_kernel, n_experts=n_experts),
        grid=(m // tm,),
        in_specs=[pl.BlockSpec((tm, k), lambda i: (i, 0)),
                  pl.BlockSpec((k, LANES), lambda i: (0, 0))],
        out_specs=pl.BlockSpec((tm, LANES), lambda i: (i, 0)),
        out_shape=jax.ShapeDtypeStruct((m, LANES), jnp.float32),
        compiler_params=_params("parallel"),
        name="moe_route",
    )(x, router_padded)


def _row_copy(src_hbm, row, buf, slot, r, sem):
    return pltpu.make_async_copy(src_hbm.at[pl.ds(row, 1), :], buf.at[slot, pl.ds(r, 1), :], sem.at[slot])


def _moe_up_kernel(tile_expert_ref, src_ref, n_used_ref, x_hbm, wg_ref, wu_ref, o_ref, xbuf, sem):
    tile = xbuf.shape[1]
    i = pl.program_id(0)
    n_used = n_used_ref[0]
    slot = i % 2

    def start_gather(t_idx, s):
        def body(r, carry):
            _row_copy(x_hbm, src_ref[t_idx * tile + r], xbuf, s, r, sem).start()
            return carry
        lax.fori_loop(0, tile, body, 0)

    def wait_gather(s):
        def body(r, carry):
            _row_copy(x_hbm, 0, xbuf, s, r, sem).wait()
            return carry
        lax.fori_loop(0, tile, body, 0)

    @pl.when(i == 0)
    def _():
        start_gather(0, 0)

    @pl.when(i < n_used)
    def _():
        wait_gather(slot)
        nxt = jnp.minimum(i + 1, n_used - 1)
        for r in range(tile):
            _row_copy(x_hbm, src_ref[nxt * tile + r], xbuf, 1 - slot, r, sem).start()
        x = xbuf[slot].astype(jnp.bfloat16)
        g = jnp.dot(x, wg_ref[...], preferred_element_type=jnp.float32)
        u = jnp.dot(x, wu_ref[...], preferred_element_type=jnp.float32)
        o_ref[...] = (g * jax.nn.sigmoid(g) * u).astype(o_ref.dtype)

        @pl.when(i + 1 >= n_used)
        def _():
            wait_gather(1 - slot)

    @pl.when(i >= n_used)
    def _():
        o_ref[...] = jnp.zeros_like(o_ref)


def moe_up_sorted(x, wg, wu, tile_expert, src_row, n_used, expert0):
    k = x.shape[1]
    d_e = wg.shape[1]
    n_tiles = tile_expert.shape[0]
    w_spec = pl.BlockSpec((k, d_e), lambda i, te, src, nu: (expert0 + te[i], 0), pipeline_mode=pl.Buffered(1))
    return pl.pallas_call(
        _moe_up_kernel,
        grid_spec=pltpu.PrefetchScalarGridSpec(
            num_scalar_prefetch=3,
            grid=(n_tiles,),
            in_specs=[pl.BlockSpec(memory_space=pl.ANY), w_spec, w_spec],
            out_specs=pl.BlockSpec((MOE_TILE, d_e), lambda i, te, src, nu: (i, 0)),
            scratch_shapes=[pltpu.VMEM((2, MOE_TILE, k), jnp.float32), pltpu.SemaphoreType.DMA((2,))]),
        out_shape=jax.ShapeDtypeStruct((n_tiles * MOE_TILE, d_e), jnp.bfloat16),
        compiler_params=_params("arbitrary"),
        name="moe_up_sorted",
    )(tile_expert, src_row, n_used, x, wg, wu)


def _moe_down_kernel(tile_expert_ref, n_used_ref, h_ref, w_ref, o_ref):
    @pl.when(pl.program_id(0) < n_used_ref[0])
    def _():
        o_ref[...] = jnp.dot(h_ref[...], w_ref[...], preferred_element_type=jnp.float32)

    @pl.when(pl.program_id(0) >= n_used_ref[0])
    def _():
        o_ref[...] = jnp.zeros_like(o_ref)


def moe_down_sorted(hid, wd, tile_expert, n_used, expert0):
    d_e = hid.shape[1]
    n = wd.shape[1]
    n_tiles = tile_expert.shape[0]
    return pl.pallas_call(
        _moe_down_kernel,
        grid_spec=pltpu.PrefetchScalarGridSpec(
            num_scalar_prefetch=2,
            grid=(n_tiles,),
            in_specs=[pl.BlockSpec((MOE_TILE, d_e), lambda i, te, nu: (i, 0)),
                      pl.BlockSpec((d_e, n), lambda i, te, nu: (expert0 + te[i], 0))],
            out_specs=pl.BlockSpec((MOE_TILE, n), lambda i, te, nu: (i, 0))),
        out_shape=jax.ShapeDtypeStruct((n_tiles * MOE_TILE, n), jnp.float32),
        compiler_params=_params("arbitrary"),
        name="moe_down_sorted",
    )(tile_expert, n_used, hid, wd)


def _moe_combine_kernel(pos_ref, y_hbm, route_ref, o_ref, ybuf, sem):
    tm = o_ref.shape[0]
    i = pl.program_id(0)
    slot = i % 2

    def start_gather(t_idx, s):
        def body(r, carry):
            _row_copy(y_hbm, pos_ref[t_idx * 2 * tm + r], ybuf, s, r, sem).start()
            return carry
        lax.fori_loop(0, 2 * tm, body, 0)

    @pl.when(i == 0)
    def _():
        start_gather(0, 0)

    @pl.when(i + 1 < pl.num_programs(0))
    def _():
        start_gather(i + 1, 1 - slot)

    def wait_row(r, carry):
        _row_copy(y_hbm, 0, ybuf, slot, r, sem).wait()
        return carry
    lax.fori_loop(0, 2 * tm, wait_row, 0)
    route = route_ref[...]
    o_ref[...] = (route[:, 2:3] * ybuf[slot, 0:tm, :] + route[:, 3:4] * ybuf[slot, tm:2 * tm, :]).astype(o_ref.dtype)


def moe_combine(y_sorted, pos, route, tm=256):
    m = route.shape[0]
    n = y_sorted.shape[1]
    return pl.pallas_call(
        _moe_combine_kernel,
        grid_spec=pltpu.PrefetchScalarGridSpec(
            num_scalar_prefetch=1,
            grid=(m // tm,),
            in_specs=[pl.BlockSpec(memory_space=pl.ANY),
                      pl.BlockSpec((tm, LANES), lambda i, pos: (i, 0))],
            out_specs=pl.BlockSpec((tm, n), lambda i, pos: (i, 0)),
            scratch_shapes=[pltpu.VMEM((2, 2 * tm, n), jnp.float32), pltpu.SemaphoreType.DMA((2,))]),
        out_shape=jax.ShapeDtypeStruct((m, n), jnp.bfloat16),
        compiler_params=_params("arbitrary"),
        name="moe_combine",
    )(pos, y_sorted, route)


def moe_plan(route, n_experts):
    m = route.shape[0]
    n_tiles = (2 * m) // MOE_TILE + n_experts
    experts = route[:, :2].astype(jnp.int32)
    e_flat = experts.reshape(-1)
    onehot = (e_flat[:, None] == jnp.arange(n_experts, dtype=jnp.int32)[None, :]).astype(jnp.int32)
    running = jnp.cumsum(onehot, axis=0)
    rank = jnp.sum((running - onehot) * onehot, axis=1)
    counts = running[-1]
    padded = (counts + MOE_TILE - 1) // MOE_TILE * MOE_TILE
    ends = jnp.cumsum(padded)
    starts = ends - padded
    dest = jnp.sum(onehot * starts[None, :], axis=1) + rank
    tile_start = jnp.arange(n_tiles, dtype=jnp.int32) * MOE_TILE
    tile_expert = jnp.minimum(jnp.sum((tile_start[:, None] >= ends[None, :]).astype(jnp.int32), axis=1),
                              n_experts - 1)
    order = jnp.argsort(e_flat, stable=True).astype(jnp.int32)
    first = jnp.cumsum(counts) - counts
    tile_onehot = (tile_expert[:, None] == jnp.arange(n_experts, dtype=jnp.int32)[None, :]).astype(jnp.int32)
    tile_shift = jnp.sum(tile_onehot * (first - starts)[None, :], axis=1)
    idx = (tile_start + tile_shift)[:, None] + jnp.arange(MOE_TILE, dtype=jnp.int32)[None, :]
    src_assign = order[jnp.clip(idx.reshape(-1), 0, 2 * m - 1)]
    src_row = src_assign // 2
    n_used = (ends[-1] // MOE_TILE).astype(jnp.int32).reshape(1)
    tm = 256
    pos = dest.reshape(m // tm, tm, 2).transpose(0, 2, 1).reshape(-1).astype(jnp.int32)
    return tile_expert, src_row.astype(jnp.int32), n_used, pos


def _ada_kernel(c_ref, w_ref, b_ref, o_ref):
    c = c_ref[...]
    a = c * jax.nn.sigmoid(c)
    o_ref[...] = jnp.dot(a, w_ref[...], preferred_element_type=jnp.float32,
                         precision=lax.Precision.HIGHEST) + b_ref[...]


def ada_projection(c, w_ada, b_ada):
    b, d = c.shape
    n = w_ada.shape[1]
    rows = _round_up(b, 8)
    c_pad = jnp.pad(c, ((0, rows - b), (0, 0)))
    tn = 512
    out = pl.pallas_call(
        _ada_kernel,
        grid=(n // tn,),
        in_specs=[pl.BlockSpec((rows, d), lambda j: (0, 0)),
                  pl.BlockSpec((d, tn), lambda j: (0, j)),
                  pl.BlockSpec((1, tn), lambda j: (0, j))],
        out_specs=pl.BlockSpec((rows, tn), lambda j: (0, j)),
        out_shape=jax.ShapeDtypeStruct((rows, n), jnp.float32),
        compiler_params=_params("parallel"),
        name="ada_projection",
    )(c_pad, w_ada, b_ada.reshape(1, n))
    return out[:b]


def _modulate_kernel(x_ref, sc_ref, sh_ref, h_ref):
    h_ref[...] = (x_ref[...] * (1.0 + sc_ref[0]) + sh_ref[0]).astype(h_ref.dtype)


def modulate(x, sc, sh, seq, tr=256):
    t, d = x.shape
    per_b = seq // tr
    vec = pl.BlockSpec((1, 1, d), lambda i: (i // per_b, 0, 0))
    return pl.pallas_call(
        _modulate_kernel,
        grid=(t // tr,),
        in_specs=[pl.BlockSpec((tr, d), lambda i: (i, 0)), vec, vec],
        out_specs=pl.BlockSpec((tr, d), lambda i: (i, 0)),
        out_shape=jax.ShapeDtypeStruct((t, d), jnp.bfloat16),
        compiler_params=_params("parallel"),
        name="modulate",
    )(x, sc, sh)


def _res_ln_kernel(x_ref, y_ref, gate_ref, g_ref, b_ref, sc_ref, sh_ref, xn_ref, h_ref):
    v = DEEPNORM_ALPHA * x_ref[...] + (1.0 + gate_ref[0]) * y_ref[...].astype(jnp.float32)
    mu = jnp.mean(v, axis=-1, keepdims=True)
    vc = v - mu
    var = jnp.mean(vc * vc, axis=-1, keepdims=True)
    xn = vc * lax.rsqrt(var + LN_EPS) * g_ref[...] + b_ref[...]
    xn_ref[...] = xn
    h_ref[...] = (xn * (1.0 + sc_ref[0]) + sh_ref[0]).astype(h_ref.dtype)


def residual_layer_norm(x, y, gate, ln_g, ln_b, sc_next, sh_next, seq, tr=256, h_dtype=jnp.bfloat16):
    t, d = x.shape
    per_b = seq // tr
    row = pl.BlockSpec((tr, d), lambda i: (i, 0))
    vec_b = pl.BlockSpec((1, 1, d), lambda i: (i // per_b, 0, 0))
    vec = pl.BlockSpec((1, d), lambda i: (0, 0))
    return pl.pallas_call(
        _res_ln_kernel,
        grid=(t // tr,),
        in_specs=[row, row, vec_b, vec, vec, vec_b, vec_b],
        out_specs=[row, row],
        out_shape=[jax.ShapeDtypeStruct((t, d), jnp.float32), jax.ShapeDtypeStruct((t, d), h_dtype)],
        compiler_params=_params("parallel"),
        name="residual_layer_norm",
    )(x, y, gate, ln_g.reshape(1, d), ln_b.reshape(1, d), sc_next, sh_next)


def _rope_kernel(pos_ref, inv_ref, sign_ref, q_ref, k1_ref, k2_ref, qo_ref, k1o_ref, k2o_ref, *, q_scale):
    ang = pos_ref[...].astype(jnp.float32) * inv_ref[...]
    cos = jnp.cos(ang)
    sin = jnp.sin(ang) * sign_ref[...]

    def rot(x_ref, o_ref, scale):
        for h in range(x_ref.shape[1] // HEAD_DIM):
            x = x_ref[:, h * HEAD_DIM:(h + 1) * HEAD_DIM].astype(jnp.float32)
            y = x * cos + pltpu.roll(x, HEAD_DIM // 2, 1) * sin
            o_ref[:, h * HEAD_DIM:(h + 1) * HEAD_DIM] = (y * scale).astype(o_ref.dtype)

    rot(q_ref, qo_ref, q_scale)
    rot(k1_ref, k1o_ref, 1.0)
    rot(k2_ref, k2o_ref, 1.0)


def rope_qkk(proj, pos_col, q_cols, k1_cols, k2_cols, tr=512):
    t = proj.shape[0]
    half = HEAD_DIM // 2
    inv = ROPE_THETA ** (-jnp.arange(half, dtype=jnp.float32) / half)
    inv = jnp.concatenate([inv, inv]).reshape(1, HEAD_DIM)
    sign = jnp.concatenate([-jnp.ones((half,), jnp.float32), jnp.ones((half,), jnp.float32)]).reshape(1, HEAD_DIM)

    def cols(c):
        start, width = c
        assert start % width == 0
        return pl.BlockSpec((tr, width), lambda i: (i, start // width))

    def out(c):
        return pl.BlockSpec((tr, c[1]), lambda i: (i, 0))

    vec = pl.BlockSpec((1, HEAD_DIM), lambda i: (0, 0))
    return pl.pallas_call(
        functools.partial(_rope_kernel, q_scale=HEAD_DIM ** -0.5),
        grid=(t // tr,),
        in_specs=[pl.BlockSpec((tr, 1), lambda i: (i, 0)), vec, vec, cols(q_cols), cols(k1_cols), cols(k2_cols)],
        out_specs=[out(q_cols), out(k1_cols), out(k2_cols)],
        out_shape=[jax.ShapeDtypeStruct((t, c[1]), jnp.bfloat16) for c in (q_cols, k1_cols, k2_cols)],
        compiler_params=_params("parallel"),
        name="rope_qkk",
    )(pos_col, inv, sign, proj, proj, proj)


MASK_VALUE = -1e30
ATT_TILE = 512
FULL_CAUSAL_TILE = 1024


def _flash_kernel(qi_ref, kt_ref, first_ref, last_ref, *refs, mode, heads, shared_kv, tile, q_scale):
    if mode == "fox":
        q_ref, k_ref, v_ref, f_ref, o_ref, m_sc, l_sc, acc_sc = refs
    elif mode == "sel":
        q_ref, k_ref, v_ref, sel_ref, o_ref, m_sc, l_sc, acc_sc = refs
    else:
        q_ref, k_ref, v_ref, o_ref, m_sc, l_sc, acc_sc = refs
    step_id = pl.program_id(2)
    qi = qi_ref[step_id]
    kt = kt_ref[step_id]

    @pl.when(first_ref[step_id] == 1)
    def _():
        m_sc[...] = jnp.full_like(m_sc, MASK_VALUE)
        l_sc[...] = jnp.zeros_like(l_sc)
        acc_sc[...] = jnp.zeros_like(acc_sc)

    def step(masked):
        mask = None
        if masked:
            row = qi * tile + lax.broadcasted_iota(jnp.int32, (tile, tile), 0)
            col = kt * tile + lax.broadcasted_iota(jnp.int32, (tile, tile), 1)
            mask = col <= row
            if mode == "win":
                mask = mask & (row - col < WINDOW)
            if mode == "sel":
                n_blk = sel_ref.shape[-1]
                blk = lax.broadcasted_iota(jnp.int32, (n_blk, tile), 0)
                key_blk = (kt * tile + lax.broadcasted_iota(jnp.int32, (n_blk, tile), 1)) // SLC_BLOCK
                expand = jnp.where(blk == key_blk, 1.0, 0.0).astype(jnp.bfloat16)
                member = jnp.dot(sel_ref[0, 0], expand, preferred_element_type=jnp.float32)
                mask = mask & (member > 0.5)
        for r in range(heads):
            q = q_ref[:, r * HEAD_DIM:(r + 1) * HEAD_DIM]
            if q_scale != 1.0:
                q = (q.astype(jnp.float32) * q_scale).astype(q.dtype)
            kv = 0 if shared_kv else r
            k = k_ref[:, kv * HEAD_DIM:(kv + 1) * HEAD_DIM]
            v = v_ref[:, kv * HEAD_DIM:(kv + 1) * HEAD_DIM]
            s = lax.dot_general(q, k, (((1,), (1,)), ((), ())), preferred_element_type=jnp.float32)
            if mode == "fox":
                s = s - f_ref[0, r]
            if masked:
                s = jnp.where(mask, s, MASK_VALUE)
            m_prev = m_sc[r]
            m_new = jnp.maximum(m_prev, jnp.max(s, axis=1, keepdims=True))
            alpha = jnp.exp(m_prev - m_new)
            p = jnp.exp(s - m_new[:, :1])
            if masked:
                p = jnp.where(mask, p, 0.0)
            l_sc[r] = alpha * l_sc[r] + jnp.sum(p, axis=1, keepdims=True)
            acc_sc[r] = alpha * acc_sc[r] + jnp.dot(p.astype(v.dtype), v, preferred_element_type=jnp.float32)
            m_sc[r] = m_new

    if mode == "fox":
        @pl.when(kt < qi)
        def _():
            step(False)

        @pl.when(kt == qi)
        def _():
            step(True)
    else:
        step(True)

    @pl.when(last_ref[step_id] == 1)
    def _():
        for r in range(heads):
            inv_l = 1.0 / l_sc[r]
            o_ref[:, r * HEAD_DIM:(r + 1) * HEAD_DIM] = (acc_sc[r] * inv_l).astype(o_ref.dtype)


def flash_attention(mode, q_arr, q_col0, k_arr, k_col0, v_arr, v_col0, batch, seq, n_groups, heads,
                    shared_kv, extra=None, q_scale=1.0, tile=ATT_TILE):
    t = batch * seq
    nq = seq // tile
    qw = heads * HEAD_DIM
    kvw = HEAD_DIM if shared_kv else qw
    assert q_col0 % qw == 0 and k_col0 % kvw == 0 and v_col0 % kvw == 0 and seq % tile == 0
    reach = WINDOW // tile if mode == "win" else nq
    pairs = [(qi, kt) for qi in range(nq) for kt in range(max(0, qi - reach), qi + 1)]
    tables = [jnp.asarray(col, jnp.int32) for col in (
        [qi for qi, _ in pairs], [kt for _, kt in pairs],
        [int(kt == max(0, qi - reach)) for qi, kt in pairs], [int(kt == qi) for qi, kt in pairs])]

    in_specs = [
        pl.BlockSpec((tile, qw), lambda b, g, s, qi, kt, fi, la: (b * nq + qi[s], q_col0 // qw + g)),
        pl.BlockSpec((tile, kvw), lambda b, g, s, qi, kt, fi, la: (b * nq + kt[s], k_col0 // kvw + g)),
        pl.BlockSpec((tile, kvw), lambda b, g, s, qi, kt, fi, la: (b * nq + kt[s], v_col0 // kvw + g)),
    ]
    args = [q_arr, k_arr, v_arr]
    if mode == "sel":
        in_specs.append(pl.BlockSpec((1, 1, tile, extra.shape[-1]),
                                     lambda b, g, s, qi, kt, fi, la: (b, g, qi[s], 0)))
        args.append(extra)
    elif mode == "fox":
        in_specs.append(pl.BlockSpec((1, heads, 1, tile), lambda b, g, s, qi, kt, fi, la: (b, g, 0, kt[s])))
        args.append(extra)
    stat = pltpu.VMEM((heads, tile, HEAD_DIM), jnp.float32)
    return pl.pallas_call(
        functools.partial(_flash_kernel, mode=mode, heads=heads, shared_kv=shared_kv, tile=tile, q_scale=q_scale),
        grid_spec=pltpu.PrefetchScalarGridSpec(
            num_scalar_prefetch=len(tables),
            grid=(batch, n_groups, len(pairs)),
            in_specs=in_specs,
            out_specs=pl.BlockSpec((tile, qw), lambda b, g, s, qi, kt, fi, la: (b * nq + qi[s], g)),
            scratch_shapes=[stat, stat, stat]),
        out_shape=jax.ShapeDtypeStruct((t, n_groups * qw), jnp.bfloat16),
        compiler_params=_params("parallel", "parallel", "arbitrary"),
        name="flash_" + mode,
    )(*tables, *args)


SSD_COL_TILE = 1024
HALO_ROWS = 8


def _cumsum_rows(x):
    n = x.shape[0]
    row = lax.broadcasted_iota(jnp.int32, x.shape, 0)
    k = 1
    while k < n:
        x = x + jnp.where(row >= k, pltpu.roll(x, k, 0), 0.0)
        k *= 2
    return x


def _ssd_kernel(xs0_ref, xs1_ref, bm_ref, cm_ref, z0_ref, z1_ref, small_ref, conv_w_ref, conv_b_ref,
                dt_bias_ref, a_log_ref, d_skip_ref, norm_w_ref, o_ref, buf_sc, act_sc, state_sc, *, dt_lane0):
    L, P, N = SSM_CHUNK, SSM_HEADDIM, SSM_STATE
    heads_per_group = SSM_HEADS // SSM_GROUPS
    c = pl.program_id(1)

    @pl.when(c == 0)
    def _():
        buf_sc[0:HALO_ROWS, :] = jnp.zeros((HALO_ROWS, SSM_CONV_DIM), jnp.float32)
        state_sc[...] = jnp.zeros_like(state_sc)

    srcs = (xs0_ref, xs1_ref, bm_ref, cm_ref)
    cw = 512
    for cb in range(SSM_CONV_DIM // cw):
        src = srcs[cb * cw // SSD_COL_TILE]
        off = cb * cw % SSD_COL_TILE
        cols = slice(cb * cw, (cb + 1) * cw)
        buf_sc[HALO_ROWS:HALO_ROWS + L, cols] = src[:, off:off + cw].astype(jnp.float32)
        acc = jnp.broadcast_to(conv_b_ref[:, cols], (L, cw))
        for k in range(SSM_CONV):
            start = HALO_ROWS - (SSM_CONV - 1) + k
            acc = acc + conv_w_ref[k:k + 1, cols] * buf_sc[start:start + L, cols]
        act_sc[:, cols] = acc * jax.nn.sigmoid(acc)
        buf_sc[0:HALO_ROWS, cols] = buf_sc[L:L + HALO_ROWS, cols]

    pre = small_ref[...] + dt_bias_ref[...]
    dt = jnp.maximum(pre, 0.0) + jnp.log(1.0 + jnp.exp(-jnp.abs(pre)))
    a_cum = _cumsum_rows(dt * (-jnp.exp(a_log_ref[...])))
    a_cum_t = a_cum.T
    a_last = a_cum[L - 1:L, :]
    tril = lax.broadcasted_iota(jnp.int32, (L, L), 0) >= lax.broadcasted_iota(jnp.int32, (L, L), 1)

    for g in range(SSM_GROUPS):
        bc = act_sc[:, SSM_DIM + g * N:SSM_DIM + (g + 1) * N]
        cc = act_sc[:, SSM_DIM + SSM_GROUPS * N + g * N:SSM_DIM + SSM_GROUPS * N + (g + 1) * N].astype(jnp.bfloat16)
        cb_mat = lax.dot_general(cc, bc.astype(jnp.bfloat16), (((1,), (1,)), ((), ())),
                                 preferred_element_type=jnp.float32)
        bc_t = bc.T.astype(jnp.bfloat16)
        yz = []
        ss = jnp.zeros((L, 1), jnp.float32)
        for e in range(heads_per_group):
            h = g * heads_per_group + e
            lane = dt_lane0 + h
            dt_col = dt[:, lane:lane + 1]
            ac_col = a_cum[:, lane:lane + 1]
            ac_row = a_cum_t[lane:lane + 1, :]
            last = a_last[:, lane:lane + 1]
            decay = jnp.where(tril, jnp.exp(jnp.minimum(ac_col - ac_row, 0.0)), 0.0)
            xs = act_sc[:, h * P:(h + 1) * P]
            xd = xs * dt_col
            y = jnp.dot((cb_mat * decay).astype(jnp.bfloat16), xd.astype(jnp.bfloat16),
                        preferred_element_type=jnp.float32)
            st = state_sc[h]
            y = y + jnp.dot(cc, st.astype(jnp.bfloat16), preferred_element_type=jnp.float32) * jnp.exp(ac_col)
            xdd = (xd * jnp.exp(last - ac_col)).astype(jnp.bfloat16)
            state_sc[h] = jnp.exp(last) * st + jnp.dot(bc_t, xdd, preferred_element_type=jnp.float32)
            y = y + d_skip_ref[:, h:h + 1] * xs
            z_ref = z0_ref if h * P < SSD_COL_TILE else z1_ref
            zc = h * P % SSD_COL_TILE
            z = z_ref[:, zc:zc + P].astype(jnp.float32)
            y = y * (z * jax.nn.sigmoid(z))
            ss = ss + jnp.sum(y * y, axis=1, keepdims=True)
            yz.append(y)
        scale = lax.rsqrt(ss / (heads_per_group * P) + LN_EPS)
        for e in range(heads_per_group):
            h = g * heads_per_group + e
            o_ref[:, h * P:(h + 1) * P] = (yz[e] * scale * norm_w_ref[:, h * P:(h + 1) * P]).astype(o_ref.dtype)


def ssd_mixer(proj_main, proj_small, batch, seq, xbc_col0, z_col0, dt_lane0, conv_w, conv_b, dt_bias, a_log,
              d_skip, norm_w):
    t = batch * seq
    nc = seq // SSM_CHUNK
    ct = SSD_COL_TILE
    assert xbc_col0 % ct == 0 and z_col0 % ct == 0 and SSM_DIM == 2 * ct and SSM_GROUPS * SSM_STATE == ct

    def view(col0):
        return pl.BlockSpec((SSM_CHUNK, ct), lambda b, c: (b * nc + c, col0 // ct))

    def lane_row(v):
        return jnp.pad(v.astype(jnp.float32), (dt_lane0, LANES - dt_lane0 - v.shape[0])).reshape(1, LANES)

    full = lambda a: pl.BlockSpec(a.shape, lambda b, c: (0, 0))
    consts = [conv_w, conv_b.reshape(1, -1), lane_row(dt_bias), lane_row(a_log),
              jnp.pad(d_skip, (0, LANES - SSM_HEADS)).reshape(1, LANES), norm_w.reshape(1, -1)]
    return pl.pallas_call(
        functools.partial(_ssd_kernel, dt_lane0=dt_lane0),
        grid=(batch, nc),
        in_specs=[view(xbc_col0), view(xbc_col0 + ct), view(xbc_col0 + 2 * ct), view(xbc_col0 + 3 * ct),
                  view(z_col0), view(z_col0 + ct),
                  pl.BlockSpec((SSM_CHUNK, LANES), lambda b, c: (b * nc + c, 0))] + [full(a) for a in consts],
        out_specs=pl.BlockSpec((SSM_CHUNK, SSM_DIM), lambda b, c: (b * nc + c, 0)),
        out_shape=jax.ShapeDtypeStruct((t, SSM_DIM), jnp.bfloat16),
        scratch_shapes=[pltpu.VMEM((SSM_CHUNK + HALO_ROWS, SSM_CONV_DIM), jnp.float32),
                        pltpu.VMEM((SSM_CHUNK, SSM_CONV_DIM), jnp.float32),
                        pltpu.VMEM((SSM_HEADS, SSM_STATE, SSM_HEADDIM), jnp.float32)],
        compiler_params=_params("parallel", "arbitrary"),
        name="ssd_mixer",
    )(proj_main, proj_main, proj_main, proj_main, proj_main, proj_main, proj_small, *consts)


BIG = 3.0e38


def _rope_rows(x, pos_col, inv_row, sign_row):
    ang = pos_col.astype(jnp.float32) * inv_row
    return x * jnp.cos(ang) + pltpu.roll(x, HEAD_DIM // 2, 1) * (jnp.sin(ang) * sign_row)


def _gelu_tanh(x):
    return 0.5 * x * (1.0 + jnp.tanh(0.7978845608028654 * (x + 0.044715 * x * x * x)))


def _compress_kernel(kc_ref, vc_ref, pos_ref, emb_ref, w1_ref, w2_ref, inv_ref, sign_ref, ko_ref, vo_ref, x_sc):
    n = ko_ref.shape[2]
    half = CMP_BLOCK // 2
    assert half == CMP_STRIDE
    for which, src, dst in ((0, kc_ref, ko_ref), (1, vc_ref, vo_ref)):
        x_sc[...] = src[...].astype(jnp.float32)
        lo = jnp.zeros((n, HEAD_DIM), jnp.float32)
        hi = jnp.zeros((n, HEAD_DIM), jnp.float32)
        for p in range(half):
            xp = x_sc[pl.ds(p, n, stride=CMP_STRIDE), :]
            for part, tok in ((0, p), (1, half + p)):
                xb = (xp + emb_ref[which, tok:tok + 1, :]).astype(jnp.bfloat16)
                term = jnp.dot(xb, w1_ref[which, tok * HEAD_DIM:(tok + 1) * HEAD_DIM, :],
                               preferred_element_type=jnp.float32)
                if part == 0:
                    lo = lo + term
                else:
                    hi = hi + term
        pre = lo + pltpu.roll(hi, n - 1, 0)
        out = jnp.dot(_gelu_tanh(pre).astype(jnp.bfloat16), w2_ref[which], preferred_element_type=jnp.float32)
        if which == 0:
            out = _rope_rows(out, pos_ref[0], inv_ref[...], sign_ref[...])
        dst[0, 0] = out.astype(dst.dtype)


def nsa_compress(proj_main, kc_col0, vc_col0, pos_cmp, cmp_pos, cmp_w1, cmp_w2, batch, seq):
    n = seq // CMP_STRIDE
    half = HEAD_DIM // 2
    inv = ROPE_THETA ** (-jnp.arange(half, dtype=jnp.float32) / half)
    inv = jnp.concatenate([inv, inv]).reshape(1, HEAD_DIM)
    sign = jnp.concatenate([-jnp.ones((half,), jnp.float32), jnp.ones((half,), jnp.float32)]).reshape(1, HEAD_DIM)
    w1 = cmp_w1.astype(jnp.bfloat16)
    w2 = cmp_w2.astype(jnp.bfloat16)
    full = lambda a: pl.BlockSpec(a.shape, lambda b, g: (0,) * a.ndim)
    out = jax.ShapeDtypeStruct((batch, NSA_KV_HEADS, n, HEAD_DIM), jnp.bfloat16)
    out_spec = pl.BlockSpec((1, 1, n, HEAD_DIM), lambda b, g: (b, g, 0, 0))
    return pl.pallas_call(
        _compress_kernel,
        grid=(batch, NSA_KV_HEADS),
        in_specs=[pl.BlockSpec((seq, HEAD_DIM), lambda b, g: (b, kc_col0 // HEAD_DIM + g)),
                  pl.BlockSpec((seq, HEAD_DIM), lambda b, g: (b, vc_col0 // HEAD_DIM + g)),
                  pl.BlockSpec((1, n, 1), lambda b, g: (b, 0, 0)),
                  full(cmp_pos), full(w1), full(w2), full(inv), full(sign)],
        out_specs=[out_spec, out_spec],
        out_shape=[out, out],
        scratch_shapes=[pltpu.VMEM((seq, HEAD_DIM), jnp.float32)],
        compiler_params=_params("parallel", "parallel"),
        name="nsa_compress",
    )(proj_main, proj_main, pos_cmp, cmp_pos, w1, w2, inv, sign)


def _softmax_rows_or_zero(s, mask, axis):
    s = jnp.where(mask, s, MASK_VALUE)
    e = jnp.where(mask, jnp.exp(s - jnp.max(s, axis=axis, keepdims=True)), 0.0)
    d = jnp.sum(e, axis=axis, keepdims=True)
    return e / jnp.where(d > 0.0, d, 1.0)


def _cmp_select_kernel(q_ref, k_ref, v_ref, o_ref, sel_ref, *, tq, n_slc):
    qi = pl.program_id(2)
    k = k_ref[0, 0]
    v = v_ref[0, 0]
    n = k.shape[0]
    last_tok = CMP_BLOCK - 1
    t_row = qi * tq + lax.broadcasted_iota(jnp.int32, (tq, n), 0)
    mask = CMP_STRIDE * lax.broadcasted_iota(jnp.int32, (tq, n), 1) + last_tok <= t_row
    t_lane = qi * tq + lax.broadcasted_iota(jnp.int32, (n, tq), 1)
    mask_t = CMP_STRIDE * lax.broadcasted_iota(jnp.int32, (n, tq), 0) + last_tok <= t_lane
    p_sum_t = jnp.zeros((n, tq), jnp.float32)
    for r in range(NSA_REP):
        q = q_ref[:, r * HEAD_DIM:(r + 1) * HEAD_DIM]
        s = lax.dot_general(q, k, (((1,), (1,)), ((), ())), preferred_element_type=jnp.float32)
        p = _softmax_rows_or_zero(s, mask, 1)
        o_ref[:, r * HEAD_DIM:(r + 1) * HEAD_DIM] = jnp.dot(
            p.astype(v.dtype), v, preferred_element_type=jnp.float32).astype(o_ref.dtype)
        s_t = lax.dot_general(k, q, (((1,), (1,)), ((), ())), preferred_element_type=jnp.float32)
        p_sum_t = p_sum_t + _softmax_rows_or_zero(s_t, mask_t, 0)
    blk_n = lax.broadcasted_iota(jnp.int32, (n_slc, n), 0) * SLC_BLOCK
    start_n = lax.broadcasted_iota(jnp.int32, (n_slc, n), 1) * CMP_STRIDE
    overlap_t = jnp.where((start_n < blk_n + SLC_BLOCK) & (start_n + CMP_BLOCK > blk_n), 1.0, 0.0).astype(jnp.bfloat16)
    p_hi = p_sum_t.astype(jnp.bfloat16)
    p_lo = (p_sum_t - p_hi.astype(jnp.float32)).astype(jnp.bfloat16)
    imp = (jnp.dot(overlap_t, p_hi, preferred_element_type=jnp.float32)
           + jnp.dot(overlap_t, p_lo, preferred_element_type=jnp.float32))
    blk = lax.broadcasted_iota(jnp.int32, (n_slc, tq), 0)
    qblk = (qi * tq + lax.broadcasted_iota(jnp.int32, (n_slc, tq), 1)) // SLC_BLOCK
    forced = (blk == 0) | (blk == qblk) | (blk == qblk - 1)
    imp = jnp.where(forced, BIG, jnp.where(blk > qblk, -BIG, imp))
    rank = jnp.zeros((n_slc, tq), jnp.float32)
    for j in range(n_slc):
        other = imp[j:j + 1, :]
        beats = (other > imp) | ((other == imp) & (blk > j))
        rank = rank + jnp.where(beats, 1.0, 0.0)
    member = jnp.where(rank < min(N_SELECT, n_slc), 1.0, 0.0)
    sel_ref[0, 0] = member.T.astype(sel_ref.dtype)


def nsa_compressed_attention(q_rot, k_cmp, v_cmp, batch, seq, tq=256):
    t = batch * seq
    nq = seq // tq
    n_slc = seq // SLC_BLOCK
    n = k_cmp.shape[2]
    qw = NSA_REP * HEAD_DIM
    kv_spec = pl.BlockSpec((1, 1, n, HEAD_DIM), lambda b, g, qi: (b, g, 0, 0))
    return pl.pallas_call(
        functools.partial(_cmp_select_kernel, tq=tq, n_slc=n_slc),
        grid=(batch, NSA_KV_HEADS, nq),
        in_specs=[pl.BlockSpec((tq, qw), lambda b, g, qi: (b * nq + qi, g)), kv_spec, kv_spec],
        out_specs=[pl.BlockSpec((tq, qw), lambda b, g, qi: (b * nq + qi, g)),
                   pl.BlockSpec((1, 1, tq, n_slc), lambda b, g, qi: (b, g, qi, 0))],
        out_shape=[jax.ShapeDtypeStruct((t, NSA_Q_DIM), jnp.bfloat16),
                   jax.ShapeDtypeStruct((batch, NSA_KV_HEADS, seq, n_slc), jnp.bfloat16)],
        compiler_params=_params("parallel", "parallel", "parallel"),
        name="nsa_cmp_select",
    )(q_rot, k_cmp, v_cmp)


def _nsa_combine_kernel(cmp_ref, slc_ref, win_ref, gate_ref, ssm_ref, o_ref):
    g = jax.nn.sigmoid(gate_ref[...])
    for h in range(NSA_HEADS):
        c = slice(h * HEAD_DIM, (h + 1) * HEAD_DIM)
        o = (g[:, 3 * h:3 * h + 1] * cmp_ref[:, c].astype(jnp.float32)
             + g[:, 3 * h + 1:3 * h + 2] * slc_ref[:, c].astype(jnp.float32)
             + g[:, 3 * h + 2:3 * h + 3] * win_ref[:, c].astype(jnp.float32))
        o_ref[:, c] = o.astype(o_ref.dtype)
    o_ref[:, NSA_Q_DIM:] = ssm_ref[...]


def nsa_combine(o_cmp, o_slc, o_win, proj_small, o_ssm, tr=512):
    t = o_cmp.shape[0]
    row = lambda w: pl.BlockSpec((tr, w), lambda i: (i, 0))
    return pl.pallas_call(
        _nsa_combine_kernel,
        grid=(t // tr,),
        in_specs=[row(NSA_Q_DIM), row(NSA_Q_DIM), row(NSA_Q_DIM), row(LANES), row(SSM_DIM)],
        out_specs=row(MIX_DIM),
        out_shape=jax.ShapeDtypeStruct((t, MIX_DIM), jnp.bfloat16),
        compiler_params=_params("parallel"),
        name="nsa_combine",
    )(o_cmp, o_slc, o_win, proj_small, o_ssm)


CONF_HALO = 32


def _conformer_kernel(val_ref, gate_ref, fox_ref, w_ref, b_ref, lng_ref, lnb_ref, o_ref, buf_sc, act_sc, *, ts):
    @pl.when(pl.program_id(1) == 0)
    def _():
        buf_sc[0:CONF_HALO, :] = jnp.zeros((CONF_HALO, CONF_DIM), jnp.float32)

    cw = LANES
    total = jnp.zeros((ts, 1), jnp.float32)
    for cb in range(CONF_DIM // cw):
        cols = slice(cb * cw, (cb + 1) * cw)
        gate = gate_ref[:, cols].astype(jnp.float32)
        buf_sc[CONF_HALO:CONF_HALO + ts, cols] = val_ref[:, cols].astype(jnp.float32) * jax.nn.sigmoid(gate)
        acc = jnp.broadcast_to(b_ref[:, cols], (ts, cw))
        for k in range(CONF_KERNEL):
            start = CONF_HALO - (CONF_KERNEL - 1) + k
            acc = acc + w_ref[k:k + 1, cols] * buf_sc[start:start + ts, cols]
        act_sc[:, cols] = acc
        total = total + jnp.sum(acc, axis=1, keepdims=True)
        buf_sc[0:CONF_HALO, cols] = buf_sc[ts:ts + CONF_HALO, cols]
    u = act_sc[...]
    mu = total / CONF_DIM
    uc = u - mu
    var = jnp.mean(uc * uc, axis=1, keepdims=True)
    y = uc * lax.rsqrt(var + LN_EPS) * lng_ref[...] + lnb_ref[...]
    o_ref[:, FOX_DIM:] = (y * jax.nn.sigmoid(y)).astype(o_ref.dtype)
    o_ref[:, :FOX_DIM] = fox_ref[...]


def conformer_mixer(proj_main, glu_col0, o_fox, batch, seq, conv_w, conv_b, ln_g, ln_b, ts=256):
    t = batch * seq
    ns = seq // ts
    assert glu_col0 % CONF_DIM == 0
    full = lambda a: pl.BlockSpec(a.shape, lambda b, s: (0, 0))
    consts = [conv_w, conv_b.reshape(1, -1), ln_g.reshape(1, -1), ln_b.reshape(1, -1)]
    return pl.pallas_call(
        functools.partial(_conformer_kernel, ts=ts),
        grid=(batch, ns),
        in_specs=[pl.BlockSpec((ts, CONF_DIM), lambda b, s: (b * ns + s, glu_col0 // CONF_DIM)),
                  pl.BlockSpec((ts, CONF_DIM), lambda b, s: (b * ns + s, glu_col0 // CONF_DIM + 1)),
                  pl.BlockSpec((ts, FOX_DIM), lambda b, s: (b * ns + s, 0))] + [full(a) for a in consts],
        out_specs=pl.BlockSpec((ts, MIX_DIM), lambda b, s: (b * ns + s, 0)),
        out_shape=jax.ShapeDtypeStruct((t, MIX_DIM), jnp.bfloat16),
        scratch_shapes=[pltpu.VMEM((ts + CONF_HALO, CONF_DIM), jnp.float32),
                        pltpu.VMEM((ts, CONF_DIM), jnp.float32)],
        compiler_params=_params("parallel", "arbitrary"),
        name="conformer_mixer",
    )(proj_main, proj_main, o_fox, *consts)


def _fox_decay_kernel(f_ref, bias_ref, o_ref):
    pre = f_ref[...] + bias_ref[...]
    logf = jnp.minimum(pre, 0.0) - jnp.log(1.0 + jnp.exp(-jnp.abs(pre)))
    o_ref[0] = _cumsum_rows(logf).T[:FOX_HEADS, :]


def fox_decay(proj_small, f_bias, batch, seq):
    bias = jnp.pad(f_bias.astype(jnp.float32), (0, LANES - FOX_HEADS)).reshape(1, LANES)
    out = pl.pallas_call(
        _fox_decay_kernel,
        grid=(batch,),
        in_specs=[pl.BlockSpec((seq, LANES), lambda b: (b, 0)), pl.BlockSpec((1, LANES), lambda b: (0, 0))],
        out_specs=pl.BlockSpec((1, FOX_HEADS, seq), lambda b: (b, 0, 0)),
        out_shape=jax.ShapeDtypeStruct((batch, FOX_HEADS, seq), jnp.float32),
        compiler_params=_params("parallel"),
        name="fox_decay",
    )(proj_small, bias)
    return out.reshape(batch, FOX_HEADS, 1, seq)


TM = 512
FFN_TN = 256
FOX_HEADS_PER_STEP = 4
E_MAIN = NSA_Q_DIM + 6 * NSA_KV_DIM + SSM_DIM + SSM_CONV_DIM
O_MAIN = 3 * FOX_DIM + 2 * CONF_DIM
assert TOP_K == 2


def _bf16(w):
    return w.astype(jnp.bfloat16)


def _col_offsets(sizes):
    return np.concatenate([[0], np.cumsum(sizes)]).tolist()


def _nsa_ssd_heads(proj_main, proj_small, positions, cmp_pos, cmp_w1, cmp_w2, conv_w, conv_b, dt_bias, a_log,
                   d_skip, norm_w):
    B, S = positions.shape
    T = B * S
    G, R = NSA_KV_HEADS, NSA_REP
    kv0 = [NSA_Q_DIM + i * NSA_KV_DIM for i in range(6)]
    z0 = NSA_Q_DIM + 6 * NSA_KV_DIM
    q_rot, ksl_rot, kw_rot = rope_qkk(proj_main, positions.reshape(T, 1), (0, NSA_Q_DIM),
                                      (kv0[2], NSA_KV_DIM), (kv0[4], NSA_KV_DIM))
    n = S // CMP_STRIDE
    pos_cmp = positions[:, CMP_BLOCK - 1::CMP_STRIDE]
    pos_cmp = jnp.pad(pos_cmp, ((0, 0), (0, n - pos_cmp.shape[1]))).reshape(B, n, 1)
    k_cmp, v_cmp = nsa_compress(proj_main, kv0[0], kv0[1], pos_cmp, cmp_pos, cmp_w1, cmp_w2, B, S)
    o_cmp, member = nsa_compressed_attention(q_rot, k_cmp, v_cmp, B, S)
    o_slc = flash_attention("sel", q_rot, 0, ksl_rot, 0, proj_main, kv0[3], B, S, G, R, True, extra=member,
                            tile=FULL_CAUSAL_TILE)
    o_win = flash_attention("win", q_rot, 0, kw_rot, 0, proj_main, kv0[5], B, S, G, R, True)
    o_ssm = ssd_mixer(proj_main, proj_small, B, S, z0 + SSM_DIM, z0, 3 * NSA_HEADS, conv_w, conv_b, dt_bias,
                      a_log, d_skip, norm_w)
    return nsa_combine(o_cmp, o_slc, o_win, proj_small, o_ssm)


def _fox_conformer_heads(proj_main, proj_small, B, S, f_bias, conv_w, conv_b, conf_ln_g, conf_ln_b):
    F = fox_decay(proj_small, f_bias, B, S)
    o_fox = flash_attention("fox", proj_main, 0, proj_main, FOX_DIM, proj_main, 2 * FOX_DIM, B, S,
                            FOX_HEADS // FOX_HEADS_PER_STEP, FOX_HEADS_PER_STEP, False, extra=F,
                            q_scale=HEAD_DIM ** -0.5, tile=FULL_CAUSAL_TILE)
    return conformer_mixer(proj_main, 3 * FOX_DIM, o_fox, B, S, conv_w, conv_b, conf_ln_g, conf_ln_b)


def kernel(x, c, positions, w_ada, b_ada, e_ada_table, e_ln1_g, e_ln1_b, e_ln2_g, e_ln2_b, e_w_in, e_cmp_pos, e_cmp_w1, e_cmp_w2, e_conv_w, e_conv_b, e_dt_bias, e_a_log, e_d_skip, e_ssm_norm_w, e_w_out, e_ffn_w_gate, e_ffn_w_up, e_ffn_w_down, o_ada_table, o_ln1_g, o_ln1_b, o_ln2_g, o_ln2_b, o_w_in, o_fox_f_bias, o_conf_conv_w, o_conf_conv_b, o_conf_ln_g, o_conf_ln_b, o_w_out, o_moe_router, o_moe_w_gate, o_moe_w_up, o_moe_w_down):
    B, S, D = x.shape
    T = B * S
    mod = ada_projection(c, w_ada, b_ada).reshape(B, 6, D)

    def mods(table):
        m = mod + table[None]
        return [m[:, k, None, :] for k in range(6)]

    layer_mods = [mods(e_ada_table[i // 2]) if i % 2 == 0 else mods(o_ada_table[i // 2]) for i in range(DEPTH)]
    eo = _col_offsets(E_SPLIT_SIZES)
    oo = _col_offsets(O_SPLIT_SIZES)
    e_small = eo[8] - eo[7] + eo[11] - eo[10]
    e_in = cast_cols(e_w_in, E_MAIN + LANES,
                     [(0, eo[7], 0), (eo[8], eo[10] - eo[8], eo[7]), (eo[7], eo[8] - eo[7], E_MAIN),
                      (eo[10], eo[11] - eo[10], E_MAIN + eo[8] - eo[7])])
    assert e_small <= LANES
    o_in = cast_cols(o_w_in, O_MAIN + LANES,
                     [(0, oo[3], 0), (oo[4], oo[5] - oo[4], oo[3]), (oo[3], oo[4] - oo[3], O_MAIN)])
    e_out, o_out = cast_bf16(e_w_out), cast_bf16(o_w_out)
    ffn_g, ffn_u, ffn_d = cast_bf16(e_ffn_w_gate), cast_bf16(e_ffn_w_up), cast_bf16(e_ffn_w_down)
    moe_g, moe_u = cast_bf16(o_moe_w_gate, tr=1024), cast_bf16(o_moe_w_up, tr=1024)
    moe_d = cast_bf16(o_moe_w_down, tr=D_EXPERT // 2)
    xf = x.reshape(T, D)
    h = modulate(xf, layer_mods[0][1], layer_mods[0][0], S)
    for i in range(DEPTH):
        j = i // 2
        sh1, sc1, g1, sh2, sc2, g2 = layer_mods[i]
        if i % 2 == 0:
            proj_main = matmul(h, e_in, jnp.bfloat16, TM, 1024, n=E_MAIN, w_row0=j * D)
            proj_small = matmul(h, e_in, jnp.float32, TM, LANES, n=LANES, w_row0=j * D, w_col0=E_MAIN)
            mix = _nsa_ssd_heads(proj_main, proj_small, positions, e_cmp_pos[j], e_cmp_w1[j], e_cmp_w2[j],
                                 e_conv_w[j], e_conv_b[j], e_dt_bias[j], e_a_log[j], e_d_skip[j], e_ssm_norm_w[j])
            y = matmul(mix, e_out, jnp.bfloat16, TM, 1024, w_row0=j * MIX_DIM)
            xf, h = residual_layer_norm(xf, y, g1, e_ln1_g[j], e_ln1_b[j], sc2, sh2, S)
            hid = swiglu_up(h, ffn_g, ffn_u, 2 * TM, FFN_TN, w_row0=j * D)
            y = matmul(hid, ffn_d, jnp.bfloat16, TM, 512, w_row0=j * D_FF, rows_resident=True)
            ln_g, ln_b = e_ln2_g[j], e_ln2_b[j]
        else:
            proj_main = matmul(h, o_in, jnp.bfloat16, TM, 1024, n=O_MAIN, w_row0=j * D)
            proj_small = matmul(h, o_in, jnp.float32, TM, LANES, n=LANES, w_row0=j * D, w_col0=O_MAIN)
            mix = _fox_conformer_heads(proj_main, proj_small, B, S, o_fox_f_bias[j], o_conf_conv_w[j],
                                       o_conf_conv_b[j], o_conf_ln_g[j], o_conf_ln_b[j])
            y = matmul(mix, o_out, jnp.bfloat16, TM, 1024, w_row0=j * MIX_DIM)
            xf, h = residual_layer_norm(xf, y, g1, o_ln1_g[j], o_ln1_b[j], sc2, sh2, S, h_dtype=jnp.float32)
            router = _bf16(jnp.pad(o_moe_router[j], ((0, 0), (0, LANES - N_EXPERTS))))
            route = moe_route(h, router, N_EXPERTS)
            tile_expert, src_row, n_used, pos = moe_plan(route, N_EXPERTS)
            hid = moe_up_sorted(h, moe_g, moe_u, tile_expert, src_row, n_used, j * N_EXPERTS)
            y_sorted = moe_down_sorted(hid, moe_d, tile_expert, n_used, j * N_EXPERTS)
            y = moe_combine(y_sorted, pos, route)
            ln_g, ln_b = o_ln2_g[j], o_ln2_b[j]
        if i + 1 < DEPTH:
            nsh, nsc = layer_mods[i + 1][0], layer_mods[i + 1][1]
        else:
            nsh, nsc = sh2, sc2
        xf, h = residual_layer_norm(xf, y, g2, ln_g, ln_b, nsc, nsh, S)
    return xf.reshape(B, S, D)
```
